```python
import math
import jax, jax.numpy as jnp
from jax import lax
import numpy as np

D_MODEL = 1024
BATCH = 2
SEQ = 8192
DEPTH = 4

N_HEADS_A = 4
HD_A = 64
WIDTH_A = N_HEADS_A * 2 * HD_A
DIL_PAIRS = ((128, 1), (512, 4), (2048, 16))
N_DIL = len(DIL_PAIRS)
N_HEADS_B = 4
HD_B = 128
WIDTH_B = N_HEADS_B * HD_B
SSM_GROUP = 16
SSM_STATE = 64
WIDTH_C = 512
N_GROUPS_C = WIDTH_C // SSM_GROUP
N_BRANCHES = 3
A_QK_COLS = 2 * N_HEADS_A * HD_A
B_QKV_COLS = N_DIL * N_HEADS_B * HD_B
IN_SIZES = (A_QK_COLS, A_QK_COLS, WIDTH_A, B_QKV_COLS, B_QKV_COLS, B_QKV_COLS, WIDTH_C, N_BRANCHES * D_MODEL)
IN_COLS = sum(IN_SIZES)
D_FF = 2816
CONV_WIDTH = 3
ROPE_THETA = 500000.0
ROPE_FRACTION = 4
Q_BLOCK = 128
EPS = 1e-6

kernel_name = "hybrid_gated_diffattn_dilated_s5_convffn"


def rms_norm(x, g):
    xf = x.astype(jnp.float32)
    y = xf * lax.rsqrt(jnp.mean(xf * xf, axis=-1, keepdims=True) + EPS)
    return (y * g.astype(jnp.float32)).astype(x.dtype)


def rope_tables(positions, head_dim):
    rot = head_dim // ROPE_FRACTION
    inv = ROPE_THETA ** (-jnp.arange(0, rot, 2, dtype=jnp.float32) / rot)
    ang = positions.astype(jnp.float32)[..., None] * inv
    return jnp.cos(ang)[:, :, None, :], jnp.sin(ang)[:, :, None, :]


def apply_rope(x, cos, sin):
    half = cos.shape[-1]
    x1 = x[..., :half].astype(jnp.float32)
    x2 = x[..., half:2 * half].astype(jnp.float32)
    rot = jnp.concatenate([x1 * cos - x2 * sin, x2 * cos + x1 * sin], axis=-1).astype(x.dtype)
    return jnp.concatenate([rot, x[..., 2 * half:]], axis=-1)


def diff_attention(q, k, v, lam):
    bsz, s_len, h2, d = q.shape
    n_h = h2 // 2
    nb = s_len // Q_BLOCK
    scale = 1.0 / math.sqrt(d)
    qb = q.reshape(bsz, nb, Q_BLOCK, h2, d).transpose(1, 0, 2, 3, 4)
    starts = jnp.arange(nb) * Q_BLOCK
    kpos = jnp.arange(s_len)

    def one_block(args):
        qblk, st = args
        s = jnp.einsum('bqhd,bkhd->bhqk', qblk, k, preferred_element_type=jnp.float32) * scale
        mask = kpos[None, :] <= (st + jnp.arange(Q_BLOCK))[:, None]
        p = jax.nn.softmax(jnp.where(mask, s, -jnp.inf), axis=-1)
        p = p.reshape(bsz, n_h, 2, Q_BLOCK, s_len)
        a = p[:, :, 0] - lam * p[:, :, 1]
        return jnp.einsum('bhqk,bkhe->bqhe', a.astype(v.dtype), v)

    out = lax.map(one_block, (qb, starts))
    return out.transpose(1, 0, 2, 3, 4).reshape(bsz, s_len, n_h, 2 * d)


def dilated_window_attention(q, k, v, window, dilation):
    bsz, s_len, n_h, hd = q.shape
    blk = window // dilation
    unit = blk * dilation
    s_pad = -(-s_len // unit) * unit
    nb = s_pad // unit
    scale = 1.0 / math.sqrt(hd)

    def to_blocks(t):
        t = jnp.pad(t, ((0, 0), (0, s_pad - s_len), (0, 0), (0, 0)))
        return t.reshape(bsz, nb, blk, dilation, n_h, hd)

    qb, kb, vb = to_blocks(q), to_blocks(k), to_blocks(v)
    prev = lambda t: jnp.concatenate([jnp.zeros_like(t[:, :1]), t[:, :-1]], axis=1)
    kk = jnp.concatenate([prev(kb), kb], axis=2)
    vv = jnp.concatenate([prev(vb), vb], axis=2)
    s = jnp.einsum('bnqrhd,bnkrhd->bnrhqk', qb, kk, preferred_element_type=jnp.float32) * scale
    i = jnp.arange(blk)[:, None] + blk
    j = jnp.arange(2 * blk)[None, :]
    rel = i - j
    band = (rel >= 0) & (rel <= blk)
    has_prev = jnp.arange(nb)[:, None, None] > 0
    valid = band[None] & (has_prev | (j >= blk)[None])
    s = jnp.where(valid[None, :, None, None], s, -jnp.inf)
    m = jnp.max(s, axis=-1, keepdims=True)
    p = jnp.exp(s - m)
    den = jnp.sum(p, axis=-1, keepdims=True)
    o = jnp.einsum('bnrhqk,bnkrhd->bnqrhd', (p / den).astype(v.dtype), vv)
    lse = (m + jnp.log(den))[..., 0]
    o = o.reshape(bsz, s_pad, n_h, hd)[:, :s_len]
    lse = lse.transpose(0, 1, 4, 2, 3).reshape(bsz, s_pad, n_h)[:, :s_len]
    return o, lse


def _complex_scan_combine(e1, e2):
    a1r, a1i, b1r, b1i = e1
    a2r, a2i, b2r, b2i = e2
    return (a2r * a1r - a2i * a1i,
            a2r * a1i + a2i * a1r,
            a2r * b1r - a2i * b1i + b2r,
            a2r * b1i + a2i * b1r + b2i)


def s5_branch(u, a_re, a_im, log_dt, b_re, b_im, c_re, c_im, d_skip, w_glu, b_glu):
    bsz, s_len, _ = u.shape
    f32 = jnp.float32
    uf = u.astype(f32)
    ug = uf.reshape(bsz, s_len, N_GROUPS_C, SSM_GROUP)
    a_re, a_im = a_re.astype(f32), a_im.astype(f32)
    dt = jnp.exp(log_dt.astype(f32))[:, None]
    mag = jnp.exp(a_re * dt)
    lb_re, lb_im = mag * jnp.cos(a_im * dt), mag * jnp.sin(a_im * dt)
    n_re, n_im = lb_re - 1.0, lb_im
    den = a_re * a_re + a_im * a_im
    f_re = (n_re * a_re + n_im * a_im) / den
    f_im = (n_im * a_re - n_re * a_im) / den
    b_re, b_im = b_re.astype(f32), b_im.astype(f32)
    bb_re = f_re[..., None] * b_re - f_im[..., None] * b_im
    bb_im = f_re[..., None] * b_im + f_im[..., None] * b_re
    bu_re = jnp.einsum('gpc,bsgc->bsgp', bb_re, ug)
    bu_im = jnp.einsum('gpc,bsgc->bsgp', bb_im, ug)
    la_re = jnp.broadcast_to(lb_re, bu_re.shape)
    la_im = jnp.broadcast_to(lb_im, bu_re.shape)
    _, _, x_re, x_im = lax.associative_scan(_complex_scan_combine, (la_re, la_im, bu_re, bu_im), axis=1)
    y = (jnp.einsum('gcp,bsgp->bsgc', c_re.astype(f32), x_re)
         - jnp.einsum('gcp,bsgp->bsgc', c_im.astype(f32), x_im))
    y = y.reshape(bsz, s_len, WIDTH_C) + d_skip.astype(f32) * uf
    z = jax.nn.gelu(y).astype(u.dtype)
    return z * jax.nn.sigmoid(z @ w_glu + b_glu)


def conv_ffn(h, w_up, conv_w, conv_b, w_down):
    a, b = jnp.split(h @ w_up, 2, axis=-1)
    s_len = a.shape[1]
    ap = jnp.pad(a, ((0, 0), (CONV_WIDTH - 1, 0), (0, 0)))
    a = conv_b + sum(conv_w[j] * ap[:, j:j + s_len] for j in range(CONV_WIDTH))
    return (jax.nn.silu(a) * b) @ w_down


def setup_inputs(seed: int = 0) -> dict:
    key = jax.random.key(seed)
    ks = iter(jax.random.split(key, 40))
    f32 = jnp.float32
    L = DEPTH
    nrm = lambda shape, scale: scale * jax.random.normal(next(ks), shape, f32)
    gain = lambda shape: 1.0 + 0.02 * jax.random.normal(next(ks), shape, f32)
    state_idx = jnp.arange(SSM_STATE, dtype=f32)
    return {
        "x": jax.random.normal(next(ks), (BATCH, SEQ, D_MODEL), f32),
        "positions": jnp.broadcast_to(jnp.arange(SEQ, dtype=jnp.int32), (BATCH, SEQ)),
        "attn_norm_g": gain((L, D_MODEL)),
        "w_in": nrm((L, D_MODEL, IN_COLS), D_MODEL ** -0.5),
        "b_gate": nrm((L, N_BRANCHES * D_MODEL), 0.01),
        "qn_a": gain((L, HD_A)),
        "kn_a": gain((L, HD_A)),
        "lam_q1": nrm((L, HD_A), 0.1),
        "lam_k1": nrm((L, HD_A), 0.1),
        "lam_q2": nrm((L, HD_A), 0.1),
        "lam_k2": nrm((L, HD_A), 0.1),
        "subln_g": gain((L, 2 * HD_A)),
        "w_br_a": nrm((L, WIDTH_A, D_MODEL), WIDTH_A ** -0.5),
        "qn_b": gain((L, HD_B)),
        "kn_b": gain((L, HD_B)),
        "w_br_b": nrm((L, WIDTH_B, D_MODEL), WIDTH_B ** -0.5),
        "ssm_a_re": -0.5 + nrm((L, N_GROUPS_C, SSM_STATE), 0.01),
        "ssm_a_im": jnp.pi * state_idx + nrm((L, N_GROUPS_C, SSM_STATE), 0.01),
        "ssm_log_dt": jax.random.uniform(next(ks), (L, N_GROUPS_C), f32, math.log(1e-3), math.log(1e-1)),
        "ssm_b_re": nrm((L, N_GROUPS_C, SSM_STATE, SSM_GROUP), (2 * SSM_GROUP) ** -0.5),
        "ssm_b_im": nrm((L, N_GROUPS_C, SSM_STATE, SSM_GROUP), (2 * SSM_GROUP) ** -0.5),
        "ssm_c_re": nrm((L, N_GROUPS_C, SSM_GROUP, SSM_STATE), (2 * SSM_STATE) ** -0.5),
        "ssm_c_im": nrm((L, N_GROUPS_C, SSM_GROUP, SSM_STATE), (2 * SSM_STATE) ** -0.5),
        "ssm_d": nrm((L, WIDTH_C), 1.0),
        "w_glu": nrm((L, WIDTH_C, WIDTH_C), WIDTH_C ** -0.5),
        "b_glu": nrm((L, WIDTH_C), 0.01),
        "w_br_c": nrm((L, WIDTH_C, D_MODEL), WIDTH_C ** -0.5),
        "w_out": nrm((L, D_MODEL, D_MODEL), D_MODEL ** -0.5),
        "ffn_norm_g": gain((L, D_MODEL)),
        "w_up": nrm((L, D_MODEL, 2 * D_FF), D_MODEL ** -0.5),
        "conv_w": nrm((L, CONV_WIDTH, D_FF), CONV_WIDTH ** -0.5),
        "conv_b": nrm((L, D_FF), 0.01),
        "w_down": nrm((L, D_FF, D_MODEL), D_FF ** -0.5),
    }


def reference(x, positions, attn_norm_g, w_in, b_gate, qn_a, kn_a, lam_q1, lam_k1, lam_q2, lam_k2,
              subln_g, w_br_a, qn_b, kn_b, w_br_b, ssm_a_re, ssm_a_im, ssm_log_dt, ssm_b_re, ssm_b_im,
              ssm_c_re, ssm_c_im, ssm_d, w_glu, b_glu, w_br_c, w_out, ffn_norm_g, w_up, conv_w, conv_b,
              w_down):
    bsz, s_len, _ = x.shape
    split_idx = np.cumsum(IN_SIZES)[:-1].tolist()
    cos_a, sin_a = rope_tables(positions, HD_A)
    cos_b, sin_b = rope_tables(positions, HD_B)
    for l in range(DEPTH):
        h = rms_norm(x, attn_norm_g[l])
        proj = h @ w_in[l]
        a_q, a_k, a_v, b_q, b_k, b_v, c_u, gates = jnp.split(proj, split_idx, axis=-1)

        qa = apply_rope(rms_norm(a_q.reshape(bsz, s_len, 2 * N_HEADS_A, HD_A), qn_a[l]), cos_a, sin_a)
        ka = apply_rope(rms_norm(a_k.reshape(bsz, s_len, 2 * N_HEADS_A, HD_A), kn_a[l]), cos_a, sin_a)
        va = a_v.reshape(bsz, s_len, N_HEADS_A, 2 * HD_A)
        lam_init = 0.8 - 0.6 * math.exp(-0.3 * l)
        lam = (jnp.exp(jnp.sum(lam_q1[l].astype(jnp.float32) * lam_k1[l].astype(jnp.float32)))
               - jnp.exp(jnp.sum(lam_q2[l].astype(jnp.float32) * lam_k2[l].astype(jnp.float32)))
               + lam_init)
        oa = diff_attention(qa, ka, va, lam)
        oa = (rms_norm(oa, subln_g[l]) * (1.0 - lam_init)).reshape(bsz, s_len, WIDTH_A)

        qb = apply_rope(rms_norm(b_q.reshape(bsz, s_len, N_DIL * N_HEADS_B, HD_B), qn_b[l]), cos_b, sin_b)
        kb = apply_rope(rms_norm(b_k.reshape(bsz, s_len, N_DIL * N_HEADS_B, HD_B), kn_b[l]), cos_b, sin_b)
        qb = qb.reshape(bsz, s_len, N_DIL, N_HEADS_B, HD_B)
        kb = kb.reshape(bsz, s_len, N_DIL, N_HEADS_B, HD_B)
        vb = b_v.reshape(bsz, s_len, N_DIL, N_HEADS_B, HD_B)
        outs, lses = [], []
        for g, (window, dilation) in enumerate(DIL_PAIRS):
            o_g, lse_g = dilated_window_attention(qb[:, :, g], kb[:, :, g], vb[:, :, g], window, dilation)
            outs.append(o_g)
            lses.append(lse_g)
        wts = jax.nn.softmax(jnp.stack(lses, axis=0), axis=0)
        ob = jnp.einsum('gbsh,gbshd->bshd', wts, jnp.stack(outs, axis=0).astype(jnp.float32))
        ob = ob.astype(x.dtype).reshape(bsz, s_len, WIDTH_B)

        oc = s5_branch(c_u, ssm_a_re[l], ssm_a_im[l], ssm_log_dt[l], ssm_b_re[l], ssm_b_im[l],
                       ssm_c_re[l], ssm_c_im[l], ssm_d[l], w_glu[l], b_glu[l])

        g = jax.nn.sigmoid((gates + b_gate[l]).reshape(bsz, s_len, N_BRANCHES, D_MODEL))
        merged = (g[:, :, 0] * (oa @ w_br_a[l]) + g[:, :, 1] * (ob @ w_br_b[l])
                  + g[:, :, 2] * (oc @ w_br_c[l]))
        x = x + merged @ w_out[l]

        x = x + conv_ffn(rms_norm(x, ffn_norm_g[l]), w_up[l], conv_w[l], conv_b[l], w_down[l])
    return x
```

```python
import functools
import math

import jax
import jax.numpy as jnp
from jax import lax
from jax.experimental import pallas as pl
from jax.experimental.pallas import tpu as pltpu

F32 = jnp.float32
BF16 = jnp.bfloat16

LANES = 128
SUBLANES = 8

D_MODEL = 1024
N_HEADS_A = 4
HD_A = 64
N_DIL = 3
DIL_PAIRS = ((128, 1), (512, 4), (2048, 16))
N_HEADS_B = 4
HD_B = 128
WIDTH_B = N_HEADS_B * HD_B
SSM_GROUP = 16
SSM_STATE = 64
WIDTH_C = 512
N_GROUPS_C = WIDTH_C // SSM_GROUP
N_STATE = N_GROUPS_C * SSM_STATE
N_BRANCHES = 3
D_FF = 2816
CONV_WIDTH = 3
ROPE_THETA = 500000.0
ROPE_FRACTION = 4
EPS = 1e-6

COL_TILE = 512
A_Q_COLS = 2 * N_HEADS_A * HD_A
WIDTH_A = N_HEADS_A * 2 * HD_A
B_COLS = N_DIL * N_HEADS_B * HD_B
IN_COLS = 2 * A_Q_COLS + WIDTH_A + 3 * B_COLS + WIDTH_C + N_BRANCHES * D_MODEL
N_COL_TILES = IN_COLS // COL_TILE
BF_TILES = (2 * A_Q_COLS + WIDTH_A + 3 * B_COLS) // COL_TILE
BF_COLS = BF_TILES * COL_TILE
TILE_AQ, TILE_AK, TILE_AV = 0, 1, 2
TILE_BQ, TILE_BK, TILE_BV = 3, 6, 9
TILE_CU = 12

VMEM_LIMIT = 56 * 1024 * 1024


def _cparams(sem):
    return pltpu.CompilerParams(dimension_semantics=sem, vmem_limit_bytes=VMEM_LIMIT)


def _seg_ones(seg):
    r = lax.broadcasted_iota(jnp.int32, (LANES, LANES), 0) // seg
    c = lax.broadcasted_iota(jnp.int32, (LANES, LANES), 1) // seg
    return jnp.where(r == c, 1.0, 0.0).astype(BF16)


def _norm_rope_tile(acc, gain, seg, half, c_ref, s1_ref, s2_ref):
    ones = _seg_ones(seg)
    cos = c_ref[...]
    s_up = s1_ref[...]
    s_dn = s2_ref[...]
    outs = []
    for gi in range(acc.shape[1] // LANES):
        y = acc[:, gi * LANES:(gi + 1) * LANES]
        ysq = y * y
        hi = ysq.astype(BF16)
        lo = (ysq - hi.astype(F32)).astype(BF16)
        ssum = (jnp.dot(hi, ones, preferred_element_type=F32)
                + jnp.dot(lo, ones, preferred_element_type=F32))
        yn = y * lax.rsqrt(ssum * (1.0 / seg) + EPS) * gain[:, gi * LANES:(gi + 1) * LANES]
        rot = (yn * cos + pltpu.roll(yn, half, 1) * s_up
               + pltpu.roll(yn, LANES - half, 1) * s_dn)
        outs.append(rot)
    return jnp.concatenate(outs, axis=1)


def _inproj_kernel(x_ref, g_ref, w_ref, gain_ref, bias_ref,
                   ca_ref, sa1_ref, sa2_ref, cb_ref, sb1_ref, sb2_ref,
                   obf_ref, ocu_ref, ogate_ref, h_scr):
    j = pl.program_id(1)

    @pl.when(j == 0)
    def _():
        x = x_ref[...]
        ms = jnp.mean(x * x, axis=-1, keepdims=True)
        h_scr[...] = (x * lax.rsqrt(ms + EPS) * g_ref[...]).astype(BF16)

    acc = jnp.dot(h_scr[...], w_ref[...], preferred_element_type=F32)

    @pl.when(j <= TILE_AK)
    def _():
        obf_ref[...] = _norm_rope_tile(acc, gain_ref[...], HD_A, HD_A // ROPE_FRACTION // 2,
                                       ca_ref, sa1_ref, sa2_ref).astype(BF16)

    @pl.when(jnp.logical_and(j >= TILE_BQ, j < TILE_BV))
    def _():
        obf_ref[...] = _norm_rope_tile(acc, gain_ref[...], HD_B, HD_B // ROPE_FRACTION // 2,
                                       cb_ref, sb1_ref, sb2_ref).astype(BF16)

    @pl.when(jnp.logical_or(j == TILE_AV, jnp.logical_and(j >= TILE_BV, j < TILE_CU)))
    def _():
        obf_ref[...] = acc.astype(BF16)

    @pl.when(j == TILE_CU)
    def _():
        ocu_ref[...] = acc

    @pl.when(j > TILE_CU)
    def _():
        ogate_ref[...] = jax.nn.sigmoid(acc + bias_ref[...])


def _in_projection(x2d, norm_g, w_bf, layer, col_gain, col_bias, rope, tm):
    n = x2d.shape[0]
    tn = COL_TILE
    row = lambda i, j: (i, 0)
    return pl.pallas_call(
        _inproj_kernel,
        grid=(n // tm, N_COL_TILES),
        in_specs=[
            pl.BlockSpec((tm, D_MODEL), row),
            pl.BlockSpec((1, D_MODEL), lambda i, j: (0, 0)),
            pl.BlockSpec((None, D_MODEL, tn), lambda i, j: (layer, 0, j)),
            pl.BlockSpec((1, tn), lambda i, j: (0, j)),
            pl.BlockSpec((1, tn), lambda i, j: (0, j)),
        ] + [pl.BlockSpec((tm, LANES), row)] * 6,
        out_specs=[
            pl.BlockSpec((tm, tn), lambda i, j: (i, jnp.minimum(j, BF_TILES - 1))),
            pl.BlockSpec((tm, tn), lambda i, j: (i, 0)),
            pl.BlockSpec((tm, tn), lambda i, j: (i, jnp.maximum(j - TILE_CU - 1, 0))),
        ],
        out_shape=[jax.ShapeDtypeStruct((n, BF_COLS), BF16),
                   jax.ShapeDtypeStruct((n, WIDTH_C), F32),
                   jax.ShapeDtypeStruct((n, N_BRANCHES * D_MODEL), F32)],
        scratch_shapes=[pltpu.VMEM((tm, D_MODEL), BF16)],
        compiler_params=_cparams(("parallel", "arbitrary")),
        name="in_projection",
    )(x2d, norm_g, w_bf, col_gain, col_bias, *rope)


def _diffattn_kernel(q_ref, k_ref, v_ref, lam_ref, subg_ref, o_ref, *, tq, lam_init):
    i = pl.program_id(2)
    q = q_ref[...]
    lane = lax.broadcasted_iota(jnp.int32, (1, 2 * HD_A), 1)
    zero = jnp.zeros_like(q)
    qm = (jnp.where(lane < HD_A, q, zero), jnp.where(lane >= HD_A, q, zero))
    nt = (((1,), (1,)), ((), ()))

    def block(j, carry, diagonal):
        r0 = pl.multiple_of(j * tq, tq)
        kblk = k_ref[pl.ds(r0, tq), :]
        vblk = v_ref[pl.ds(r0, tq), :]
        new = []
        for mi in range(2):
            m, l, acc = carry[mi]
            s = lax.dot_general(qm[mi], kblk, nt, preferred_element_type=F32)
            if diagonal:
                rr = lax.broadcasted_iota(jnp.int32, (tq, tq), 0)
                cc = lax.broadcasted_iota(jnp.int32, (tq, tq), 1)
                s = jnp.where(cc <= rr, s, -jnp.inf)
            m_new = jnp.maximum(m, jnp.max(s, axis=-1, keepdims=True))
            p = jnp.exp(s - m_new)
            alpha = jnp.exp(m - m_new)
            l = alpha * l + jnp.sum(p, axis=-1, keepdims=True)
            acc = alpha * acc + jnp.dot(p.astype(BF16), vblk, preferred_element_type=F32)
            new.append((m_new, l, acc))
        return tuple(new)

    init1 = (jnp.full((tq, 1), -jnp.inf, F32), jnp.zeros((tq, 1), F32),
             jnp.zeros((tq, 2 * HD_A), F32))
    carry = lax.fori_loop(0, i, lambda j, c: block(j, c, False), (init1, init1))
    (m1, l1, a1), (m2, l2, a2) = block(i, carry, True)

    lam_p = lam_ref[...]
    lam = (jnp.exp(jnp.sum(lam_p[0:1] * lam_p[1:2], axis=-1, keepdims=True))
           - jnp.exp(jnp.sum(lam_p[2:3] * lam_p[3:4], axis=-1, keepdims=True)) + lam_init)
    o = a1 / l1 - lam * (a2 / l2)
    ms = jnp.mean(o * o, axis=-1, keepdims=True)
    o_ref[...] = ((o * lax.rsqrt(ms + EPS) * subg_ref[...]) * (1.0 - lam_init)).astype(o_ref.dtype)


def _diff_attention(proj_bf, lam_p, subln_g, lam_init, tq):
    bsz, s_len, _ = proj_bf.shape
    kern = functools.partial(_diffattn_kernel, tq=tq, lam_init=lam_init)
    return pl.pallas_call(
        kern,
        grid=(bsz, N_HEADS_A, s_len // tq),
        in_specs=[
            pl.BlockSpec((None, tq, 2 * HD_A), lambda b, h, i: (b, i, h)),
            pl.BlockSpec((None, s_len, 2 * HD_A), lambda b, h, i: (b, 0, N_HEADS_A + h)),
            pl.BlockSpec((None, s_len, 2 * HD_A), lambda b, h, i: (b, 0, 2 * N_HEADS_A + h)),
            pl.BlockSpec((4, HD_A), lambda b, h, i: (0, 0)),
            pl.BlockSpec((1, 2 * HD_A), lambda b, h, i: (0, 0)),
        ],
        out_specs=pl.BlockSpec((None, tq, 2 * HD_A), lambda b, h, i: (b, i, h)),
        out_shape=jax.ShapeDtypeStruct((bsz, s_len, WIDTH_A), BF16),
        compiler_params=_cparams(("parallel", "parallel", "arbitrary")),
        name="diff_attention",
    )(proj_bf, proj_bf, proj_bf, lam_p, subln_g)


def _dilated_kernel(q_ref, k_ref, kp_ref, v_ref, vp_ref, o_ref, lse_ref, *, tq, blk):
    n = pl.program_id(2)
    scale = 1.0 / math.sqrt(HD_B)
    nt = (((1,), (1,)), ((), ()))
    rr = lax.broadcasted_iota(jnp.int32, (blk, blk), 0)
    cc = lax.broadcasted_iota(jnp.int32, (blk, blk), 1)
    cur_ok = cc <= rr
    prev_ok = cc >= rr
    first_bias = jnp.where(n > 0, 0.0, -jnp.inf).astype(F32)
    for h in range(N_HEADS_B):
        cols = slice(h * HD_B, (h + 1) * HD_B)
        for c in range(tq // blk):
            rows = slice(c * blk, (c + 1) * blk)
            qh = q_ref[rows, cols]
            kc = k_ref[rows, cols]
            vc = v_ref[rows, cols]
            if c == 0:
                kpv, vpv = kp_ref[:, cols], vp_ref[:, cols]
            else:
                prows = slice((c - 1) * blk, c * blk)
                kpv, vpv = k_ref[prows, cols], v_ref[prows, cols]
            s_c = lax.dot_general(qh, kc, nt, preferred_element_type=F32) * scale
            s_p = lax.dot_general(qh, kpv, nt, preferred_element_type=F32) * scale
            s_c = jnp.where(cur_ok, s_c, -jnp.inf)
            s_p = jnp.where(prev_ok, s_p, -jnp.inf)
            if c == 0:
                s_p = s_p + first_bias
            m = jnp.maximum(jnp.max(s_c, axis=-1, keepdims=True), jnp.max(s_p, axis=-1, keepdims=True))
            p_c = jnp.exp(s_c - m)
            p_p = jnp.exp(s_p - m)
            den = jnp.sum(p_c, axis=-1, keepdims=True) + jnp.sum(p_p, axis=-1, keepdims=True)
            pv = (jnp.dot(p_c.astype(BF16), vc, preferred_element_type=F32)
                  + jnp.dot(p_p.astype(BF16), vpv, preferred_element_type=F32))
            o_ref[rows, cols] = pv / den
            lse_ref[rows, cols] = jnp.broadcast_to(m + jnp.log(den), (blk, HD_B))


def _dilated_attention(proj_bf, group, tq):
    bsz, s_len, _ = proj_bf.shape
    window, dil = DIL_PAIRS[group]
    blk = window // dil
    rows = s_len // dil
    tq = min(tq, rows)
    q_tile, k_tile, v_tile = TILE_BQ + group, TILE_BK + group, TILE_BV + group
    if dil > 1:
        take = lambda t: proj_bf[:, :, t * WIDTH_B:(t + 1) * WIDTH_B]
        proj_bf = jnp.concatenate([take(q_tile), take(k_tile), take(v_tile)], axis=-1)
        q_tile, k_tile, v_tile = 0, 1, 2
    per_res = proj_bf.shape[-1] // WIDTH_B
    view = proj_bf.reshape(bsz, rows, dil * per_res * WIDTH_B)
    sub = tq // blk
    qcol = lambda r: r * per_res + q_tile
    kcol = lambda r: r * per_res + k_tile
    vcol = lambda r: r * per_res + v_tile
    prev = lambda n: jnp.maximum(n * sub - 1, 0)
    kern = functools.partial(_dilated_kernel, tq=tq, blk=blk)
    o, lse = pl.pallas_call(
        kern,
        grid=(bsz, dil, rows // tq),
        in_specs=[
            pl.BlockSpec((None, tq, WIDTH_B), lambda b, r, n: (b, n, qcol(r))),
            pl.BlockSpec((None, tq, WIDTH_B), lambda b, r, n: (b, n, kcol(r))),
            pl.BlockSpec((None, blk, WIDTH_B), lambda b, r, n: (b, prev(n), kcol(r))),
            pl.BlockSpec((None, tq, WIDTH_B), lambda b, r, n: (b, n, vcol(r))),
            pl.BlockSpec((None, blk, WIDTH_B), lambda b, r, n: (b, prev(n), vcol(r))),
        ],
        out_specs=[pl.BlockSpec((None, tq, WIDTH_B), lambda b, r, n: (b, n, r))] * 2,
        out_shape=[jax.ShapeDtypeStruct((bsz, rows, dil * WIDTH_B), F32)] * 2,
        compiler_params=_cparams(("parallel", "parallel", "arbitrary")),
        name=f"dilated_attention_g{group}",
    )(view, view, view, view, view)
    return o.reshape(bsz * s_len, WIDTH_B), lse.reshape(bsz * s_len, WIDTH_B)


def _ssm_kernel(u_ref, bd_ref, cd_ref, pw_ref, dskip_ref, wglu_ref, bglu_ref, o_ref,
                x_scr, carry_scr, *, t_chunk):
    @pl.when(pl.program_id(1) == 0)
    def _():
        carry_scr[...] = jnp.zeros_like(carry_scr)

    u = u_ref[...]
    x_scr[...] = jnp.dot(u.astype(BF16), bd_ref[...], preferred_element_type=F32)
    re = slice(0, N_STATE)
    im = slice(N_STATE, 2 * N_STATE)

    def tile(t, carry):
        c_re, c_im = carry
        r0 = pl.multiple_of(t * SUBLANES, SUBLANES)
        x_re = x_scr[pl.ds(r0, SUBLANES), re]
        x_im = x_scr[pl.ds(r0, SUBLANES), im]
        for si, shift in enumerate((1, 2, 4)):
            a = pw_ref[si, :, re]
            b = pw_ref[si, :, im]
            s_re = pltpu.roll(x_re, shift, 0)
            s_im = pltpu.roll(x_im, shift, 0)
            x_re, x_im = x_re + (a * s_re - b * s_im), x_im + (a * s_im + b * s_re)
        a = pw_ref[3, :, re]
        b = pw_ref[3, :, im]
        x_re, x_im = x_re + (a * c_re - b * c_im), x_im + (a * c_im + b * c_re)
        x_scr[pl.ds(r0, SUBLANES), re] = x_re
        x_scr[pl.ds(r0, SUBLANES), im] = x_im
        last = SUBLANES - 1
        return (jnp.broadcast_to(x_re[last:, :], x_re.shape), jnp.broadcast_to(x_im[last:, :], x_im.shape))

    c_re, c_im = lax.fori_loop(0, t_chunk // SUBLANES, tile, (carry_scr[:, re], carry_scr[:, im]))
    carry_scr[:, re] = c_re
    carry_scr[:, im] = c_im

    y = jnp.dot(x_scr[...].astype(BF16), cd_ref[...], preferred_element_type=F32)
    y = y + dskip_ref[...] * u
    cdf = 0.5 * (1.0 + jnp.tanh(math.sqrt(2.0 / math.pi) * (y + 0.044715 * (y * y * y))))
    z = y * cdf
    gate = jnp.dot(z.astype(BF16), wglu_ref[...], preferred_element_type=F32) + bglu_ref[...]
    o_ref[...] = (z * jax.nn.sigmoid(gate)).astype(o_ref.dtype)


def _ssm_operands(a_re, a_im, log_dt, b_re, b_im, c_re, c_im):
    a_re, a_im = a_re.astype(F32), a_im.astype(F32)
    dt = jnp.exp(log_dt.astype(F32))[:, None]
    mag = jnp.exp(a_re * dt)
    lb_re, lb_im = mag * jnp.cos(a_im * dt), mag * jnp.sin(a_im * dt)
    n_re, n_im = lb_re - 1.0, lb_im
    den = a_re * a_re + a_im * a_im
    f_re = (n_re * a_re + n_im * a_im) / den
    f_im = (n_im * a_re - n_re * a_im) / den
    b_re, b_im = b_re.astype(F32), b_im.astype(F32)
    bb_re = f_re[..., None] * b_re - f_im[..., None] * b_im
    bb_im = f_re[..., None] * b_im + f_im[..., None] * b_re
    eye = jnp.eye(N_GROUPS_C, dtype=F32)
    blockdiag_in = lambda t: jnp.einsum('gpc,gh->gchp', t, eye).reshape(WIDTH_C, N_STATE)
    bd = jnp.concatenate([blockdiag_in(bb_re), blockdiag_in(bb_im)], axis=1).astype(BF16)
    blockdiag_out = lambda t: jnp.einsum('gcp,gh->gphc', t, eye).reshape(N_STATE, WIDTH_C)
    cd = jnp.concatenate([blockdiag_out(c_re.astype(F32)), -blockdiag_out(c_im.astype(F32))],
                         axis=0).astype(BF16)
    rows = jnp.arange(SUBLANES, dtype=F32)[:, None]
    flat = lambda t: t.reshape(1, N_STATE)

    def power(k):
        return (flat(mag) ** k) * jnp.cos(flat(a_im * dt) * k), (flat(mag) ** k) * jnp.sin(flat(a_im * dt) * k)

    tabs = []
    for shift in (1, 2, 4):
        p_re, p_im = power(float(shift))
        keep = rows >= shift
        tabs.append(jnp.concatenate([jnp.where(keep, p_re, 0.0), jnp.where(keep, p_im, 0.0)], axis=1))
    p_re, p_im = power(rows + 1.0)
    tabs.append(jnp.concatenate([p_re, p_im], axis=1))
    return bd, cd, jnp.stack(tabs, axis=0)


def _ssm_branch(proj_f32, bsz, s_len, bd, cd, pw, d_skip, w_glu_bf, layer, b_glu, t_chunk):
    n_chunks = s_len // t_chunk
    const2 = lambda b, c: (0, 0)
    kern = functools.partial(_ssm_kernel, t_chunk=t_chunk)
    return pl.pallas_call(
        kern,
        grid=(bsz, n_chunks),
        in_specs=[
            pl.BlockSpec((t_chunk, WIDTH_C), lambda b, c: (b * n_chunks + c, 0)),
            pl.BlockSpec((WIDTH_C, 2 * N_STATE), const2),
            pl.BlockSpec((2 * N_STATE, WIDTH_C), const2),
            pl.BlockSpec((4, SUBLANES, 2 * N_STATE), lambda b, c: (0, 0, 0)),
            pl.BlockSpec((1, WIDTH_C), const2),
            pl.BlockSpec((None, WIDTH_C, WIDTH_C), lambda b, c: (layer, 0, 0)),
            pl.BlockSpec((1, WIDTH_C), const2),
        ],
        out_specs=pl.BlockSpec((t_chunk, WIDTH_C), lambda b, c: (b * n_chunks + c, 0)),
        out_shape=jax.ShapeDtypeStruct((bsz * s_len, WIDTH_C), BF16),
        scratch_shapes=[pltpu.VMEM((t_chunk, 2 * N_STATE), F32),
                        pltpu.VMEM((SUBLANES, 2 * N_STATE), F32)],
        compiler_params=_cparams(("parallel", "arbitrary")),
        name="s5_scan_glu",
    )(proj_f32, bd, cd, pw, d_skip, w_glu_bf, b_glu)


def _merge_kernel(x_ref, oa_ref, o0_ref, o1_ref, o2_ref, l0_ref, l1_ref, l2_ref, oc_ref,
                  g0_ref, g1_ref, g2_ref, wa_ref, wb_ref, wc_ref, wo_ref, out_ref):
    l0, l1, l2 = l0_ref[...], l1_ref[...], l2_ref[...]
    m = jnp.maximum(jnp.maximum(l0, l1), l2)
    e0, e1, e2 = jnp.exp(l0 - m), jnp.exp(l1 - m), jnp.exp(l2 - m)
    tot = e0 + e1 + e2
    ob = (e0 / tot) * o0_ref[...] + (e1 / tot) * o1_ref[...] + (e2 / tot) * o2_ref[...]
    ya = jnp.dot(oa_ref[...], wa_ref[...], preferred_element_type=F32)
    yb = jnp.dot(ob.astype(BF16), wb_ref[...], preferred_element_type=F32)
    yc = jnp.dot(oc_ref[...], wc_ref[...], preferred_element_type=F32)
    merged = g0_ref[...] * ya + g1_ref[...] * yb + g2_ref[...] * yc
    out_ref[...] = x_ref[...] + jnp.dot(merged.astype(BF16), wo_ref[...], preferred_element_type=F32)


def _merge(x2d, oa, ob_parts, lse_parts, oc, gates, wa, wb, wc, wo, layer, tm):
    n = x2d.shape[0]
    row = lambda i: (i, 0)
    half = pl.BlockSpec((tm, WIDTH_C), row)
    full = pl.BlockSpec((tm, D_MODEL), row)
    wspec_half = pl.BlockSpec((None, WIDTH_C, D_MODEL), lambda i: (layer, 0, 0))
    gspec = lambda k: pl.BlockSpec((tm, D_MODEL), lambda i: (i, k))
    return pl.pallas_call(
        _merge_kernel,
        grid=(n // tm,),
        in_specs=[full, half, half, half, half, half, half, half, half,
                  gspec(0), gspec(1), gspec(2),
                  wspec_half, wspec_half, wspec_half,
                  pl.BlockSpec((None, D_MODEL, D_MODEL), lambda i: (layer, 0, 0))],
        out_specs=full,
        out_shape=jax.ShapeDtypeStruct((n, D_MODEL), F32),
        compiler_params=_cparams(("parallel",)),
        name="gated_merge",
    )(x2d, oa, *ob_parts, *lse_parts, oc, gates, gates, gates, wa, wb, wc, wo)


def _ffn_kernel(x_ref, g_ref, wup_ref, cw_ref, cb_ref, wdown_ref, o_ref, a_scr, *, tm, tiles_per_seq):
    halo = SUBLANES
    x = x_ref[...]
    ms = jnp.mean(x * x, axis=-1, keepdims=True)
    h = (x * lax.rsqrt(ms + EPS) * g_ref[...]).astype(BF16)
    up = jnp.dot(h, wup_ref[...], preferred_element_type=F32)

    @pl.when(pl.program_id(0) % tiles_per_seq == 0)
    def _():
        a_scr[0:halo, :] = jnp.zeros((halo, D_FF), F32)

    @pl.when(pl.program_id(0) % tiles_per_seq != 0)
    def _():
        a_scr[0:halo, :] = a_scr[tm:tm + halo, :]

    a_scr[halo:halo + tm, :] = up[:, :D_FF]
    cw = cw_ref[...]
    conv = (cb_ref[...] + cw[0:1] * a_scr[halo - 2:halo - 2 + tm, :]
            + cw[1:2] * a_scr[halo - 1:halo - 1 + tm, :] + cw[2:3] * a_scr[halo:halo + tm, :])
    act = (conv * jax.nn.sigmoid(conv)) * up[:, D_FF:]
    o_ref[...] = x + jnp.dot(act.astype(BF16), wdown_ref[...], preferred_element_type=F32)


def _conv_ffn(x2d, norm_g, wup_bf, conv_w, conv_b, wdown_bf, layer, s_len, tm):
    n = x2d.shape[0]
    row = lambda i: (i, 0)
    const = lambda i: (0, 0)
    kern = functools.partial(_ffn_kernel, tm=tm, tiles_per_seq=s_len // tm)
    return pl.pallas_call(
        kern,
        grid=(n // tm,),
        in_specs=[
            pl.BlockSpec((tm, D_MODEL), row),
            pl.BlockSpec((1, D_MODEL), const),
            pl.BlockSpec((None, D_MODEL, 2 * D_FF), lambda i: (layer, 0, 0), pipeline_mode=pl.Buffered(1)),
            pl.BlockSpec((CONV_WIDTH, D_FF), const),
            pl.BlockSpec((1, D_FF), const),
            pl.BlockSpec((None, D_FF, D_MODEL), lambda i: (layer, 0, 0), pipeline_mode=pl.Buffered(1)),
        ],
        out_specs=pl.BlockSpec((tm, D_MODEL), row),
        out_shape=jax.ShapeDtypeStruct((n, D_MODEL), F32),
        scratch_shapes=[pltpu.VMEM((tm + SUBLANES, D_FF), F32)],
        compiler_params=_cparams(("arbitrary",)),
        name="conv_ffn",
    )(x2d, norm_g, wup_bf, conv_w, conv_b, wdown_bf)


def _rope_tables(positions, head_dim):
    rot = head_dim // ROPE_FRACTION
    half = rot // 2
    inv = ROPE_THETA ** (-jnp.arange(0, rot, 2, dtype=F32) / rot)
    ang = positions.reshape(-1).astype(F32)[:, None] * inv
    cos, sin = jnp.cos(ang), jnp.sin(ang)
    n = ang.shape[0]
    rest = head_dim - rot
    c = jnp.concatenate([cos, cos, jnp.ones((n, rest), F32)], axis=1)
    s_up = jnp.concatenate([jnp.zeros((n, half), F32), sin, jnp.zeros((n, rest), F32)], axis=1)
    s_dn = jnp.concatenate([-sin, jnp.zeros((n, half + rest), F32)], axis=1)
    reps = LANES // head_dim
    return tuple(jnp.tile(t, (1, reps)) for t in (c, s_up, s_dn))


def kernel(x, positions, attn_norm_g, w_in, b_gate, qn_a, kn_a, lam_q1, lam_k1, lam_q2, lam_k2, subln_g, w_br_a, qn_b, kn_b, w_br_b, ssm_a_re, ssm_a_im, ssm_log_dt, ssm_b_re, ssm_b_im, ssm_c_re, ssm_c_im, ssm_d, w_glu, b_glu, w_br_c, w_out, ffn_norm_g, w_up, conv_w, conv_b, w_down):
    bsz, s_len, d_model = x.shape
    depth = w_in.shape[0]
    assert d_model == D_MODEL and w_in.shape[2] == IN_COLS
    assert s_len % DIL_PAIRS[-1][0] == 0, "sequence must be a multiple of the largest dilated window"
    n = bsz * s_len

    tm_proj = min(1024, n)
    tq_a = min(512, s_len)
    tq_b = 512
    t_ssm = min(512, s_len)
    tm_merge = min(256, n)
    tm_ffn = min(256, s_len)

    rope = _rope_tables(positions, HD_A) + _rope_tables(positions, HD_B)
    w_in_bf = w_in.astype(BF16)
    wa_bf, wb_bf, wc_bf, wo_bf = (w.astype(BF16) for w in (w_br_a, w_br_b, w_br_c, w_out))
    wglu_bf, wup_bf, wdown_bf = w_glu.astype(BF16), w_up.astype(BF16), w_down.astype(BF16)

    x2d = x.reshape(n, D_MODEL)
    for l in range(depth):
        lam_init = 0.8 - 0.6 * math.exp(-0.3 * l)
        ones = lambda w: jnp.ones((w,), F32)
        col_gain = jnp.concatenate([
            jnp.tile(qn_a[l].astype(F32) * (1.0 / math.sqrt(HD_A)), A_Q_COLS // HD_A),
            jnp.tile(kn_a[l].astype(F32), A_Q_COLS // HD_A), ones(WIDTH_A),
            jnp.tile(qn_b[l].astype(F32), B_COLS // HD_B), jnp.tile(kn_b[l].astype(F32), B_COLS // HD_B),
            ones(B_COLS + WIDTH_C + N_BRANCHES * D_MODEL)]).reshape(1, IN_COLS)
        col_bias = jnp.concatenate([jnp.zeros((IN_COLS - N_BRANCHES * D_MODEL,), F32),
                                    b_gate[l].astype(F32)]).reshape(1, IN_COLS)
        proj_bf, c_u, gates = _in_projection(x2d, attn_norm_g[l].reshape(1, D_MODEL).astype(F32), w_in_bf, l,
                                             col_gain, col_bias, rope, tm_proj)
        proj_bf3 = proj_bf.reshape(bsz, s_len, BF_COLS)

        lam_p = jnp.stack([lam_q1[l], lam_k1[l], lam_q2[l], lam_k2[l]]).astype(F32)
        oa = _diff_attention(proj_bf3, lam_p, subln_g[l].reshape(1, 2 * HD_A).astype(F32), lam_init, tq_a)
        oa = oa.reshape(n, WIDTH_A)

        ob_parts, lse_parts = zip(*[_dilated_attention(proj_bf3, g, tq_b) for g in range(N_DIL)])

        bd, cd, pw = _ssm_operands(ssm_a_re[l], ssm_a_im[l], ssm_log_dt[l], ssm_b_re[l], ssm_b_im[l],
                                   ssm_c_re[l], ssm_c_im[l])
        oc = _ssm_branch(c_u, bsz, s_len, bd, cd, pw, ssm_d[l].reshape(1, WIDTH_C).astype(F32),
                         wglu_bf, l, b_glu[l].reshape(1, WIDTH_C).astype(F32), t_ssm)

        x2d = _merge(x2d, oa, ob_parts, lse_parts, oc, gates, wa_bf, wb_bf, wc_bf, wo_bf, l, tm_merge)
        x2d = _conv_ffn(x2d, ffn_norm_g[l].reshape(1, D_MODEL).astype(F32), wup_bf,
                        conv_w[l].astype(F32), conv_b[l].reshape(1, D_FF).astype(F32), wdown_bf, l,
                        s_len, tm_ffn)
    return x2d.reshape(bsz, s_len, D_MODEL)
```

```python
import functools
import math

import jax
import jax.numpy as jnp
from jax import lax
from jax.experimental import pallas as pl
from jax.experimental.pallas import tpu as pltpu

F32 = jnp.float32
BF16 = jnp.bfloat16

LANES = 128
SUBLANES = 8

D_MODEL = 1024
N_HEADS_A = 4
HD_A = 64
N_DIL = 3
DIL_PAIRS = ((128, 1), (512, 4), (2048, 16))
N_HEADS_B = 4
HD_B = 128
WIDTH_B = N_HEADS_B * HD_B
SSM_GROUP = 16
SSM_STATE = 64
WIDTH_C = 512
N_GROUPS_C = WIDTH_C // SSM_GROUP
N_STATE = N_GROUPS_C * SSM_STATE
N_BRANCHES = 3
D_FF = 2816
CONV_WIDTH = 3
ROPE_THETA = 500000.0
ROPE_FRACTION = 4
EPS = 1e-6

COL_TILE = 512
A_Q_COLS = 2 * N_HEADS_A * HD_A
WIDTH_A = N_HEADS_A * 2 * HD_A
B_COLS = N_DIL * N_HEADS_B * HD_B
IN_COLS = 2 * A_Q_COLS + WIDTH_A + 3 * B_COLS + WIDTH_C + N_BRANCHES * D_MODEL
N_COL_TILES = IN_COLS // COL_TILE
BF_TILES = (2 * A_Q_COLS + WIDTH_A + 3 * B_COLS) // COL_TILE
BF_COLS = BF_TILES * COL_TILE
TILE_AQ, TILE_AK, TILE_AV = 0, 1, 2
TILE_BQ, TILE_BK, TILE_BV = 3, 6, 9
TILE_CU = 12

VMEM_LIMIT = 56 * 1024 * 1024


def _cparams(sem):
    return pltpu.CompilerParams(dimension_semantics=sem, vmem_limit_bytes=VMEM_LIMIT)


def _norm_rope_tile(acc, gain, seg, half, c_ref, s1_ref, s2_ref):
    cos = c_ref[...]
    s_up = s1_ref[...]
    s_dn = s2_ref[...]
    lane = lax.broadcasted_iota(jnp.int32, (1, LANES), 1)
    outs = []
    for gi in range(acc.shape[1] // LANES):
        y = acc[:, gi * LANES:(gi + 1) * LANES]
        ysq = y * y
        tot = jnp.sum(ysq, axis=-1, keepdims=True)
        if seg == LANES:
            ssum = tot
        else:
            low = jnp.sum(jnp.where(lane < seg, ysq, 0.0), axis=-1, keepdims=True)
            ssum = jnp.where(lane < seg, low, tot - low)
        yn = y * lax.rsqrt(ssum * (1.0 / seg) + EPS) * gain[:, gi * LANES:(gi + 1) * LANES]
        rot = (yn * cos + pltpu.roll(yn, half, 1) * s_up
               + pltpu.roll(yn, LANES - half, 1) * s_dn)
        outs.append(rot)
    return jnp.concatenate(outs, axis=1)


def _inproj_kernel(x_ref, g_ref, w_ref, gain_ref, bias_ref,
                   ca_ref, sa1_ref, sa2_ref, cb_ref, sb1_ref, sb2_ref,
                   qk_ref, vt_ref, b0_ref, b1_ref, b2_ref, cu_ref, gate_ref, dil_scr, *, tm):
    x = x_ref[...]
    ms = jnp.mean(x * x, axis=-1, keepdims=True)
    h = (x * lax.rsqrt(ms + EPS) * g_ref[...]).astype(BF16)
    dil_refs = (b0_ref, b1_ref, b2_ref)
    for j in range(N_COL_TILES):
        cols = slice(j * COL_TILE, (j + 1) * COL_TILE)
        acc = jnp.dot(h, w_ref[:, cols], preferred_element_type=F32)
        if j in (TILE_AQ, TILE_AK):
            qk_ref[:, cols] = _norm_rope_tile(acc, gain_ref[:, cols], HD_A, HD_A // ROPE_FRACTION // 2,
                                              ca_ref, sa1_ref, sa2_ref).astype(BF16)
        elif j == TILE_AV:
            vt_ref[...] = acc.T.astype(BF16)
        elif j < TILE_CU:
            part, group = divmod(j - TILE_BQ, N_DIL)
            if part < 2:
                acc = _norm_rope_tile(acc, gain_ref[:, cols], HD_B, HD_B // ROPE_FRACTION // 2,
                                      cb_ref, sb1_ref, sb2_ref)
            dil = DIL_PAIRS[group][1]
            out_ref = dil_refs[group]
            if dil == 1:
                out_ref[:, part * WIDTH_B:(part + 1) * WIDTH_B] = acc.astype(BF16)
            else:
                slot = (group - 1) * 3 + part
                for gi in range(WIDTH_B // LANES):
                    dil_scr[slot, gi] = acc[:, gi * LANES:(gi + 1) * LANES]
                for r in range(dil):
                    for gi in range(WIDTH_B // LANES):
                        c0 = (r * 3 + part) * WIDTH_B + gi * LANES
                        out_ref[:, c0:c0 + LANES] = (
                            dil_scr[slot, gi, pl.ds(r, tm // dil, stride=dil), :].astype(BF16))
        elif j == TILE_CU:
            cu_ref[...] = acc
        else:
            g0 = (j - TILE_CU - 1) * COL_TILE
            gate_ref[:, g0:g0 + COL_TILE] = jax.nn.sigmoid(acc + bias_ref[:, cols])


def _in_projection(x2d, norm_g, w_bf, layer, col_gain, col_bias, rope, bsz, s_len, tm):
    n = x2d.shape[0]
    tiles_per_seq = s_len // tm
    row = lambda i: (i, 0)
    const = lambda i: (0, 0)
    d1, d2 = DIL_PAIRS[1][1], DIL_PAIRS[2][1]
    kern = functools.partial(_inproj_kernel, tm=tm)
    return pl.pallas_call(
        kern,
        grid=(n // tm,),
        in_specs=[
            pl.BlockSpec((tm, D_MODEL), row),
            pl.BlockSpec((1, D_MODEL), const),
            pl.BlockSpec((None, D_MODEL, IN_COLS), lambda i: (layer, 0, 0), pipeline_mode=pl.Buffered(1)),
            pl.BlockSpec((1, IN_COLS), const),
            pl.BlockSpec((1, IN_COLS), const),
        ] + [pl.BlockSpec((tm, LANES), row)] * 6,
        out_specs=[
            pl.BlockSpec((tm, 2 * A_Q_COLS), row),
            pl.BlockSpec((None, WIDTH_A, tm), lambda i: (i // tiles_per_seq, 0, i % tiles_per_seq)),
            pl.BlockSpec((tm, 3 * WIDTH_B), row),
            pl.BlockSpec((tm // d1, d1 * 3 * WIDTH_B), row),
            pl.BlockSpec((tm // d2, d2 * 3 * WIDTH_B), row),
            pl.BlockSpec((tm, WIDTH_C), row),
            pl.BlockSpec((tm, N_BRANCHES * D_MODEL), row),
        ],
        out_shape=[jax.ShapeDtypeStruct((n, 2 * A_Q_COLS), BF16),
                   jax.ShapeDtypeStruct((bsz, WIDTH_A, s_len), BF16),
                   jax.ShapeDtypeStruct((n, 3 * WIDTH_B), BF16),
                   jax.ShapeDtypeStruct((n // d1, d1 * 3 * WIDTH_B), BF16),
                   jax.ShapeDtypeStruct((n // d2, d2 * 3 * WIDTH_B), BF16),
                   jax.ShapeDtypeStruct((n, WIDTH_C), F32),
                   jax.ShapeDtypeStruct((n, N_BRANCHES * D_MODEL), F32)],
        scratch_shapes=[pltpu.VMEM((6, WIDTH_B // LANES, tm, LANES), F32)],
        compiler_params=_cparams(("parallel",)),
        name="in_projection",
    )(x2d, norm_g, w_bf, col_gain, col_bias, *rope)


def _diffattn_kernel(q_ref, k_ref, vt_ref, lam_ref, subg_ref, o_ref, acc_scr, s_scr, *, tq, lam_init):
    i = pl.program_id(2)
    q = q_ref[...]
    lane = lax.broadcasted_iota(jnp.int32, (1, 2 * HD_A), 1)
    zero = jnp.zeros_like(q)
    qm = (jnp.where(lane < HD_A, q, zero), jnp.where(lane >= HD_A, q, zero))
    nt = (((1,), (1,)), ((), ()))
    acc_scr[...] = jnp.zeros_like(acc_scr)

    def scores(j, slot):
        r0 = pl.multiple_of(j * tq, tq)
        kblk = k_ref[pl.ds(r0, tq), :]
        for mi in range(2):
            s_scr[slot, mi] = lax.dot_general(kblk, qm[mi], nt, preferred_element_type=F32)

    def consume(j, slot, carry, diagonal):
        r0 = pl.multiple_of(j * tq, tq)
        vtblk = vt_ref[:, pl.ds(r0, tq)]
        new = []
        for mi in range(2):
            m, l = carry[mi]
            st = s_scr[slot, mi]
            if diagonal:
                key = lax.broadcasted_iota(jnp.int32, (tq, tq), 0)
                qry = lax.broadcasted_iota(jnp.int32, (tq, tq), 1)
                st = jnp.where(key <= qry, st, -jnp.inf)
            m_new = jnp.maximum(m, jnp.max(st, axis=0, keepdims=True))
            p = jnp.exp2(st - m_new)
            alpha = jnp.exp2(m - m_new)
            l = alpha * l + jnp.sum(p, axis=0, keepdims=True)
            acc_scr[mi] = alpha * acc_scr[mi] + jnp.dot(vtblk, p.astype(BF16), preferred_element_type=F32)
            new.append((m_new, l))
        return tuple(new)

    def stage(j, src, carry):
        scores(j + 1, 1 - src)
        return consume(j, src, carry, False)

    init1 = (jnp.full((1, tq), -jnp.inf, F32), jnp.zeros((1, tq), F32))
    scores(0, 0)
    carry = lax.fori_loop(0, i // 2, lambda t, c: stage(2 * t + 1, 1, stage(2 * t, 0, c)), (init1, init1))
    odd = lax.rem(i, 2) == 1
    (_, l1), (_, l2) = lax.cond(
        odd,
        lambda c: consume(i, 1, stage(i - 1, 0, c), True),
        lambda c: consume(i, 0, c, True),
        carry)

    lam_p = lam_ref[...]
    lam = (jnp.exp(jnp.sum(lam_p[0:1] * lam_p[1:2], axis=-1, keepdims=True))
           - jnp.exp(jnp.sum(lam_p[2:3] * lam_p[3:4], axis=-1, keepdims=True)) + lam_init)
    o_t = acc_scr[0] / l1 - lam * (acc_scr[1] / l2)
    ms = jnp.mean(o_t * o_t, axis=0, keepdims=True)
    o = (o_t * lax.rsqrt(ms + EPS)).T
    o_ref[...] = ((o * subg_ref[...]) * (1.0 - lam_init)).astype(o_ref.dtype)


def _diff_attention(qk, v_t, lam_p, subln_g, lam_init, tq):
    bsz, s_len, _ = qk.shape
    proj_bf = qk
    kern = functools.partial(_diffattn_kernel, tq=tq, lam_init=lam_init)
    return pl.pallas_call(
        kern,
        grid=(bsz, N_HEADS_A, s_len // tq),
        in_specs=[
            pl.BlockSpec((None, tq, 2 * HD_A), lambda b, h, i: (b, i, h)),
            pl.BlockSpec((None, s_len, 2 * HD_A), lambda b, h, i: (b, 0, N_HEADS_A + h)),
            pl.BlockSpec((None, 2 * HD_A, s_len), lambda b, h, i: (b, h, 0)),
            pl.BlockSpec((4, HD_A), lambda b, h, i: (0, 0)),
            pl.BlockSpec((1, 2 * HD_A), lambda b, h, i: (0, 0)),
        ],
        out_specs=pl.BlockSpec((None, tq, 2 * HD_A), lambda b, h, i: (b, i, h)),
        out_shape=jax.ShapeDtypeStruct((bsz, s_len, WIDTH_A), BF16),
        scratch_shapes=[pltpu.VMEM((2, 2 * HD_A, tq), F32), pltpu.VMEM((2, 2, tq, tq), F32)],
        compiler_params=_cparams(("parallel", "parallel", "arbitrary")),
        name="diff_attention",
    )(proj_bf, proj_bf, v_t, lam_p, subln_g)


def _dilated_kernel(q_ref, k_ref, kp_ref, v_ref, vp_ref, o_ref, lse_ref, *, tq, blk):
    n = pl.program_id(2)
    scale = 1.0 / math.sqrt(HD_B)
    nt = (((1,), (1,)), ((), ()))
    rr = lax.broadcasted_iota(jnp.int32, (blk, blk), 0)
    cc = lax.broadcasted_iota(jnp.int32, (blk, blk), 1)
    cur_ok = cc <= rr
    prev_ok = cc >= rr
    first_bias = jnp.where(n > 0, 0.0, -jnp.inf).astype(F32)
    for h in range(N_HEADS_B):
        cols = slice(h * HD_B, (h + 1) * HD_B)
        for c in range(tq // blk):
            rows = slice(c * blk, (c + 1) * blk)
            qh = q_ref[rows, cols]
            kc = k_ref[rows, cols]
            vc = v_ref[rows, cols]
            if c == 0:
                kpv, vpv = kp_ref[:, cols], vp_ref[:, cols]
            else:
                prows = slice((c - 1) * blk, c * blk)
                kpv, vpv = k_ref[prows, cols], v_ref[prows, cols]
            s_c = lax.dot_general(qh, kc, nt, preferred_element_type=F32) * scale
            s_p = lax.dot_general(qh, kpv, nt, preferred_element_type=F32) * scale
            s_c = jnp.where(cur_ok, s_c, -jnp.inf)
            s_p = jnp.where(prev_ok, s_p, -jnp.inf)
            if c == 0:
                s_p = s_p + first_bias
            m = jnp.maximum(jnp.max(s_c, axis=-1, keepdims=True), jnp.max(s_p, axis=-1, keepdims=True))
            p_c = jnp.exp(s_c - m)
            p_p = jnp.exp(s_p - m)
            den = jnp.sum(p_c, axis=-1, keepdims=True) + jnp.sum(p_p, axis=-1, keepdims=True)
            pv = (jnp.dot(p_c.astype(BF16), vc, preferred_element_type=F32)
                  + jnp.dot(p_p.astype(BF16), vpv, preferred_element_type=F32))
            o_ref[rows, cols] = pv / den
            lse_ref[rows, cols] = jnp.broadcast_to(m + jnp.log(den), (blk, HD_B))


def _dilated_attention(qkv, bsz, s_len, group, tq):
    window, dil = DIL_PAIRS[group]
    blk = window // dil
    rows = s_len // dil
    tq = min(tq, rows)
    per_res = 3
    view = qkv.reshape(bsz, rows, dil * per_res * WIDTH_B)
    sub = tq // blk
    qcol = lambda r: r * per_res
    kcol = lambda r: r * per_res + 1
    vcol = lambda r: r * per_res + 2
    prev = lambda n: jnp.maximum(n * sub - 1, 0)
    kern = functools.partial(_dilated_kernel, tq=tq, blk=blk)
    o, lse = pl.pallas_call(
        kern,
        grid=(bsz, dil, rows // tq),
        in_specs=[
            pl.BlockSpec((None, tq, WIDTH_B), lambda b, r, n: (b, n, qcol(r))),
            pl.BlockSpec((None, tq, WIDTH_B), lambda b, r, n: (b, n, kcol(r))),
            pl.BlockSpec((None, blk, WIDTH_B), lambda b, r, n: (b, prev(n), kcol(r))),
            pl.BlockSpec((None, tq, WIDTH_B), lambda b, r, n: (b, n, vcol(r))),
            pl.BlockSpec((None, blk, WIDTH_B), lambda b, r, n: (b, prev(n), vcol(r))),
        ],
        out_specs=[pl.BlockSpec((None, tq, WIDTH_B), lambda b, r, n: (b, n, r))] * 2,
        out_shape=[jax.ShapeDtypeStruct((bsz, rows, dil * WIDTH_B), F32)] * 2,
        compiler_params=_cparams(("parallel", "parallel", "arbitrary")),
        name=f"dilated_attention_g{group}",
    )(view, view, view, view, view)
    return o.reshape(bsz * rows, dil * WIDTH_B), lse.reshape(bsz * rows, dil * WIDTH_B)


def _ssm_kernel(u_ref, bd_ref, cd_ref, pw_ref, dskip_ref, wglu_ref, bglu_ref, o_ref,
                x_scr, carry_scr, *, t_chunk):
    @pl.when(pl.program_id(1) == 0)
    def _():
        carry_scr[...] = jnp.zeros_like(carry_scr)

    u = u_ref[...]
    x_scr[...] = jnp.dot(u.astype(BF16), bd_ref[...], preferred_element_type=F32)
    re = slice(0, N_STATE)
    im = slice(N_STATE, 2 * N_STATE)

    def tile(t, carry):
        c_re, c_im = carry
        r0 = pl.multiple_of(t * SUBLANES, SUBLANES)
        x_re = x_scr[pl.ds(r0, SUBLANES), re]
        x_im = x_scr[pl.ds(r0, SUBLANES), im]
        for si, shift in enumerate((1, 2, 4)):
            a = pw_ref[si, :, re]
            b = pw_ref[si, :, im]
            s_re = pltpu.roll(x_re, shift, 0)
            s_im = pltpu.roll(x_im, shift, 0)
            x_re, x_im = x_re + (a * s_re - b * s_im), x_im + (a * s_im + b * s_re)
        a = pw_ref[3, :, re]
        b = pw_ref[3, :, im]
        x_re, x_im = x_re + (a * c_re - b * c_im), x_im + (a * c_im + b * c_re)
        x_scr[pl.ds(r0, SUBLANES), re] = x_re
        x_scr[pl.ds(r0, SUBLANES), im] = x_im
        last = SUBLANES - 1
        return (jnp.broadcast_to(x_re[last:, :], x_re.shape), jnp.broadcast_to(x_im[last:, :], x_im.shape))

    c_re, c_im = lax.fori_loop(0, t_chunk // SUBLANES, tile, (carry_scr[:, re], carry_scr[:, im]))
    carry_scr[:, re] = c_re
    carry_scr[:, im] = c_im

    y = jnp.dot(x_scr[...].astype(BF16), cd_ref[...], preferred_element_type=F32)
    y = y + dskip_ref[...] * u
    cdf = 0.5 * (1.0 + jnp.tanh(math.sqrt(2.0 / math.pi) * (y + 0.044715 * (y * y * y))))
    z = y * cdf
    gate = jnp.dot(z.astype(BF16), wglu_ref[...], preferred_element_type=F32) + bglu_ref[...]
    o_ref[...] = (z * jax.nn.sigmoid(gate)).astype(o_ref.dtype)


def _ssm_operands(a_re, a_im, log_dt, b_re, b_im, c_re, c_im):
    a_re, a_im = a_re.astype(F32), a_im.astype(F32)
    dt = jnp.exp(log_dt.astype(F32))[:, None]
    mag = jnp.exp(a_re * dt)
    lb_re, lb_im = mag * jnp.cos(a_im * dt), mag * jnp.sin(a_im * dt)
    n_re, n_im = lb_re - 1.0, lb_im
    den = a_re * a_re + a_im * a_im
    f_re = (n_re * a_re + n_im * a_im) / den
    f_im = (n_im * a_re - n_re * a_im) / den
    b_re, b_im = b_re.astype(F32), b_im.astype(F32)
    bb_re = f_re[..., None] * b_re - f_im[..., None] * b_im
    bb_im = f_re[..., None] * b_im + f_im[..., None] * b_re
    eye = jnp.eye(N_GROUPS_C, dtype=F32)
    blockdiag_in = lambda t: jnp.einsum('gpc,gh->gchp', t, eye).reshape(WIDTH_C, N_STATE)
    bd = jnp.concatenate([blockdiag_in(bb_re), blockdiag_in(bb_im)], axis=1).astype(BF16)
    blockdiag_out = lambda t: jnp.einsum('gcp,gh->gphc', t, eye).reshape(N_STATE, WIDTH_C)
    cd = jnp.concatenate([blockdiag_out(c_re.astype(F32)), -blockdiag_out(c_im.astype(F32))],
                         axis=0).astype(BF16)
    rows = jnp.arange(SUBLANES, dtype=F32)[:, None]
    flat = lambda t: t.reshape(1, N_STATE)

    def power(k):
        return (flat(mag) ** k) * jnp.cos(flat(a_im * dt) * k), (flat(mag) ** k) * jnp.sin(flat(a_im * dt) * k)

    tabs = []
    for shift in (1, 2, 4):
        p_re, p_im = power(float(shift))
        keep = rows >= shift
        tabs.append(jnp.concatenate([jnp.where(keep, p_re, 0.0), jnp.where(keep, p_im, 0.0)], axis=1))
    p_re, p_im = power(rows + 1.0)
    tabs.append(jnp.concatenate([p_re, p_im], axis=1))
    return bd, cd, jnp.stack(tabs, axis=0)


def _ssm_branch(proj_f32, bsz, s_len, bd, cd, pw, d_skip, w_glu_bf, layer, b_glu, t_chunk):
    n_chunks = s_len // t_chunk
    const2 = lambda b, c: (0, 0)
    kern = functools.partial(_ssm_kernel, t_chunk=t_chunk)
    return pl.pallas_call(
        kern,
        grid=(bsz, n_chunks),
        in_specs=[
            pl.BlockSpec((t_chunk, WIDTH_C), lambda b, c: (b * n_chunks + c, 0)),
            pl.BlockSpec((WIDTH_C, 2 * N_STATE), const2),
            pl.BlockSpec((2 * N_STATE, WIDTH_C), const2),
            pl.BlockSpec((4, SUBLANES, 2 * N_STATE), lambda b, c: (0, 0, 0)),
            pl.BlockSpec((1, WIDTH_C), const2),
            pl.BlockSpec((None, WIDTH_C, WIDTH_C), lambda b, c: (layer, 0, 0)),
            pl.BlockSpec((1, WIDTH_C), const2),
        ],
        out_specs=pl.BlockSpec((t_chunk, WIDTH_C), lambda b, c: (b * n_chunks + c, 0)),
        out_shape=jax.ShapeDtypeStruct((bsz * s_len, WIDTH_C), BF16),
        scratch_shapes=[pltpu.VMEM((t_chunk, 2 * N_STATE), F32),
                        pltpu.VMEM((SUBLANES, 2 * N_STATE), F32)],
        compiler_params=_cparams(("parallel", "arbitrary")),
        name="s5_scan_glu",
    )(proj_f32, bd, cd, pw, d_skip, w_glu_bf, b_glu)


def _merge_kernel(x_ref, oa_ref, o0_ref, o1_ref, o2_ref, l0_ref, l1_ref, l2_ref, oc_ref,
                  g0_ref, g1_ref, g2_ref, wa_ref, wb_ref, wc_ref, wo_ref, out_ref, tok_scr, *, tm):
    for gi, (o_ref, l_ref) in enumerate(((o1_ref, l1_ref), (o2_ref, l2_ref))):
        dil = DIL_PAIRS[gi + 1][1]
        for r in range(dil):
            for li in range(WIDTH_B // LANES):
                cols = slice(r * WIDTH_B + li * LANES, r * WIDTH_B + (li + 1) * LANES)
                tok_scr[2 * gi, li, pl.ds(r, tm // dil, stride=dil), :] = o_ref[:, cols]
                tok_scr[2 * gi + 1, li, pl.ds(r, tm // dil, stride=dil), :] = l_ref[:, cols]
    tok = lambda k: jnp.concatenate([tok_scr[k, li] for li in range(WIDTH_B // LANES)], axis=1)
    l0, l1, l2 = l0_ref[...], tok(1), tok(3)
    m = jnp.maximum(jnp.maximum(l0, l1), l2)
    e0, e1, e2 = jnp.exp(l0 - m), jnp.exp(l1 - m), jnp.exp(l2 - m)
    tot = e0 + e1 + e2
    ob = (e0 / tot) * o0_ref[...] + (e1 / tot) * tok(0) + (e2 / tot) * tok(2)
    ya = jnp.dot(oa_ref[...], wa_ref[...], preferred_element_type=F32)
    yb = jnp.dot(ob.astype(BF16), wb_ref[...], preferred_element_type=F32)
    yc = jnp.dot(oc_ref[...], wc_ref[...], preferred_element_type=F32)
    merged = g0_ref[...] * ya + g1_ref[...] * yb + g2_ref[...] * yc
    out_ref[...] = x_ref[...] + jnp.dot(merged.astype(BF16), wo_ref[...], preferred_element_type=F32)


def _merge(x2d, oa, ob_parts, lse_parts, oc, gates, wa, wb, wc, wo, layer, tm):
    n = x2d.shape[0]
    row = lambda i: (i, 0)
    half = pl.BlockSpec((tm, WIDTH_C), row)
    full = pl.BlockSpec((tm, D_MODEL), row)
    wspec_half = pl.BlockSpec((None, WIDTH_C, D_MODEL), lambda i: (layer, 0, 0))
    gspec = lambda k: pl.BlockSpec((tm, D_MODEL), lambda i: (i, k))
    dilated = lambda g: pl.BlockSpec((tm // DIL_PAIRS[g][1], DIL_PAIRS[g][1] * WIDTH_B), row)
    kern = functools.partial(_merge_kernel, tm=tm)
    return pl.pallas_call(
        kern,
        grid=(n // tm,),
        in_specs=[full, half, half, dilated(1), dilated(2), half, dilated(1), dilated(2), half,
                  gspec(0), gspec(1), gspec(2),
                  wspec_half, wspec_half, wspec_half,
                  pl.BlockSpec((None, D_MODEL, D_MODEL), lambda i: (layer, 0, 0))],
        out_specs=full,
        out_shape=jax.ShapeDtypeStruct((n, D_MODEL), F32),
        scratch_shapes=[pltpu.VMEM((4, WIDTH_B // LANES, tm, LANES), F32)],
        compiler_params=_cparams(("parallel",)),
        name="gated_merge",
    )(x2d, oa, *ob_parts, *lse_parts, oc, gates, gates, gates, wa, wb, wc, wo)


def _ffn_kernel(x_ref, g_ref, wup_ref, cw_ref, cb_ref, wdown_ref, o_ref, a_scr, *, tm, tiles_per_seq):
    halo = SUBLANES
    x = x_ref[...]
    ms = jnp.mean(x * x, axis=-1, keepdims=True)
    h = (x * lax.rsqrt(ms + EPS) * g_ref[...]).astype(BF16)
    up = jnp.dot(h, wup_ref[...], preferred_element_type=F32)

    @pl.when(pl.program_id(0) % tiles_per_seq == 0)
    def _():
        a_scr[0:halo, :] = jnp.zeros((halo, D_FF), F32)

    @pl.when(pl.program_id(0) % tiles_per_seq != 0)
    def _():
        a_scr[0:halo, :] = a_scr[tm:tm + halo, :]

    a_scr[halo:halo + tm, :] = up[:, :D_FF]
    cw = cw_ref[...]
    conv = (cb_ref[...] + cw[0:1] * a_scr[halo - 2:halo - 2 + tm, :]
            + cw[1:2] * a_scr[halo - 1:halo - 1 + tm, :] + cw[2:3] * a_scr[halo:halo + tm, :])
    act = (conv * jax.nn.sigmoid(conv)) * up[:, D_FF:]
    o_ref[...] = x + jnp.dot(act.astype(BF16), wdown_ref[...], preferred_element_type=F32)


def _conv_ffn(x2d, norm_g, wup_bf, conv_w, conv_b, wdown_bf, layer, s_len, tm):
    n = x2d.shape[0]
    row = lambda i: (i, 0)
    const = lambda i: (0, 0)
    kern = functools.partial(_ffn_kernel, tm=tm, tiles_per_seq=s_len // tm)
    return pl.pallas_call(
        kern,
        grid=(n // tm,),
        in_specs=[
            pl.BlockSpec((tm, D_MODEL), row),
            pl.BlockSpec((1, D_MODEL), const),
            pl.BlockSpec((None, D_MODEL, 2 * D_FF), lambda i: (layer, 0, 0), pipeline_mode=pl.Buffered(1)),
            pl.BlockSpec((CONV_WIDTH, D_FF), const),
            pl.BlockSpec((1, D_FF), const),
            pl.BlockSpec((None, D_FF, D_MODEL), lambda i: (layer, 0, 0), pipeline_mode=pl.Buffered(1)),
        ],
        out_specs=pl.BlockSpec((tm, D_MODEL), row),
        out_shape=jax.ShapeDtypeStruct((n, D_MODEL), F32),
        scratch_shapes=[pltpu.VMEM((tm + SUBLANES, D_FF), F32)],
        compiler_params=_cparams(("arbitrary",)),
        name="conv_ffn",
    )(x2d, norm_g, wup_bf, conv_w, conv_b, wdown_bf)


def _rope_tables(positions, head_dim):
    rot = head_dim // ROPE_FRACTION
    half = rot // 2
    inv = ROPE_THETA ** (-jnp.arange(0, rot, 2, dtype=F32) / rot)
    ang = positions.reshape(-1).astype(F32)[:, None] * inv
    cos, sin = jnp.cos(ang), jnp.sin(ang)
    n = ang.shape[0]
    rest = head_dim - rot
    c = jnp.concatenate([cos, cos, jnp.ones((n, rest), F32)], axis=1)
    s_up = jnp.concatenate([jnp.zeros((n, half), F32), sin, jnp.zeros((n, rest), F32)], axis=1)
    s_dn = jnp.concatenate([-sin, jnp.zeros((n, half + rest), F32)], axis=1)
    reps = LANES // head_dim
    return tuple(jnp.tile(t, (1, reps)) for t in (c, s_up, s_dn))


def kernel(x, positions, attn_norm_g, w_in, b_gate, qn_a, kn_a, lam_q1, lam_k1, lam_q2, lam_k2, subln_g, w_br_a, qn_b, kn_b, w_br_b, ssm_a_re, ssm_a_im, ssm_log_dt, ssm_b_re, ssm_b_im, ssm_c_re, ssm_c_im, ssm_d, w_glu, b_glu, w_br_c, w_out, ffn_norm_g, w_up, conv_w, conv_b, w_down):
    bsz, s_len, d_model = x.shape
    depth = w_in.shape[0]
    assert d_model == D_MODEL and w_in.shape[2] == IN_COLS
    assert s_len % DIL_PAIRS[-1][0] == 0, "sequence must be a multiple of the largest dilated window"
    n = bsz * s_len

    tm_proj = 256
    tq_a = min(512, s_len)
    tq_b = 512
    t_ssm = min(512, s_len)
    tm_merge = min(256, n)
    tm_ffn = min(256, s_len)

    rope = _rope_tables(positions, HD_A) + _rope_tables(positions, HD_B)
    w_in_bf = w_in.astype(BF16)
    wa_bf, wb_bf, wc_bf, wo_bf = (w.astype(BF16) for w in (w_br_a, w_br_b, w_br_c, w_out))
    wglu_bf, wup_bf, wdown_bf = w_glu.astype(BF16), w_up.astype(BF16), w_down.astype(BF16)

    x2d = x.reshape(n, D_MODEL)
    for l in range(depth):
        lam_init = 0.8 - 0.6 * math.exp(-0.3 * l)
        ones = lambda w: jnp.ones((w,), F32)
        col_gain = jnp.concatenate([
            jnp.tile(qn_a[l].astype(F32) * (math.log2(math.e) / math.sqrt(HD_A)), A_Q_COLS // HD_A),
            jnp.tile(kn_a[l].astype(F32), A_Q_COLS // HD_A), ones(WIDTH_A),
            jnp.tile(qn_b[l].astype(F32), B_COLS // HD_B), jnp.tile(kn_b[l].astype(F32), B_COLS // HD_B),
            ones(B_COLS + WIDTH_C + N_BRANCHES * D_MODEL)]).reshape(1, IN_COLS)
        col_bias = jnp.concatenate([jnp.zeros((IN_COLS - N_BRANCHES * D_MODEL,), F32),
                                    b_gate[l].astype(F32)]).reshape(1, IN_COLS)
        qk_a, v_t, *qkv_b, c_u, gates = _in_projection(
            x2d, attn_norm_g[l].reshape(1, D_MODEL).astype(F32), w_in_bf, l, col_gain, col_bias, rope,
            bsz, s_len, tm_proj)

        lam_p = jnp.stack([lam_q1[l], lam_k1[l], lam_q2[l], lam_k2[l]]).astype(F32)
        oa = _diff_attention(qk_a.reshape(bsz, s_len, 2 * A_Q_COLS), v_t, lam_p,
                             subln_g[l].reshape(1, 2 * HD_A).astype(F32), lam_init, tq_a)
        oa = oa.reshape(n, WIDTH_A)

        ob_parts, lse_parts = zip(*[_dilated_attention(qkv_b[g], bsz, s_len, g, tq_b) for g in range(N_DIL)])

        bd, cd, pw = _ssm_operands(ssm_a_re[l], ssm_a_im[l], ssm_log_dt[l], ssm_b_re[l], ssm_b_im[l],
                                   ssm_c_re[l], ssm_c_im[l])
        oc = _ssm_branch(c_u, bsz, s_len, bd, cd, pw, ssm_d[l].reshape(1, WIDTH_C).astype(F32),
                         wglu_bf, l, b_glu[l].reshape(1, WIDTH_C).astype(F32), t_ssm)

        x2d = _merge(x2d, oa, ob_parts, lse_parts, oc, gates, wa_bf, wb_bf, wc_bf, wo_bf, l, tm_merge)
        x2d = _conv_ffn(x2d, ffn_norm_g[l].reshape(1, D_MODEL).astype(F32), wup_bf,
                        conv_w[l].astype(F32), conv_b[l].reshape(1, D_FF).astype(F32), wdown_bf, l,
                        s_len, tm_ffn)
    return x2d.reshape(bsz, s_len, D_MODEL)
```

```python
import functools
import math

import jax
import jax.numpy as jnp
from jax import lax
from jax.experimental import pallas as pl
from jax.experimental.pallas import tpu as pltpu

F32 = jnp.float32
BF16 = jnp.bfloat16

LANES = 128
SUBLANES = 8

D_MODEL = 1024
N_HEADS_A = 4
HD_A = 64
N_DIL = 3
DIL_PAIRS = ((128, 1), (512, 4), (2048, 16))
N_HEADS_B = 4
HD_B = 128
WIDTH_B = N_HEADS_B * HD_B
SSM_GROUP = 16
SSM_STATE = 64
WIDTH_C = 512
N_GROUPS_C = WIDTH_C // SSM_GROUP
N_STATE = N_GROUPS_C * SSM_STATE
N_BRANCHES = 3
D_FF = 2816
CONV_WIDTH = 3
ROPE_THETA = 500000.0
ROPE_FRACTION = 4
EPS = 1e-6

COL_TILE = 512
A_Q_COLS = 2 * N_HEADS_A * HD_A
WIDTH_A = N_HEADS_A * 2 * HD_A
B_COLS = N_DIL * N_HEADS_B * HD_B
IN_COLS = 2 * A_Q_COLS + WIDTH_A + 3 * B_COLS + WIDTH_C + N_BRANCHES * D_MODEL
N_COL_TILES = IN_COLS // COL_TILE
BF_TILES = (2 * A_Q_COLS + WIDTH_A + 3 * B_COLS) // COL_TILE
BF_COLS = BF_TILES * COL_TILE
TILE_AQ, TILE_AK, TILE_AV = 0, 1, 2
TILE_BQ, TILE_BK, TILE_BV = 3, 6, 9
TILE_CU = 12

LSE_LANES = LANES // N_HEADS_B
SSM_SLAB = 128
FFN_TILE = 256

VMEM_LIMIT = 56 * 1024 * 1024


def _cparams(sem):
    return pltpu.CompilerParams(dimension_semantics=sem, vmem_limit_bytes=VMEM_LIMIT)


def _norm_rope_tile(acc, gain, seg, half, c_ref, s1_ref, s2_ref):
    cos = c_ref[...]
    s_up = s1_ref[...]
    s_dn = s2_ref[...]
    lane = lax.broadcasted_iota(jnp.int32, (1, LANES), 1)
    outs = []
    for gi in range(acc.shape[1] // LANES):
        y = acc[:, gi * LANES:(gi + 1) * LANES]
        ysq = y * y
        tot = jnp.sum(ysq, axis=-1, keepdims=True)
        if seg == LANES:
            ssum = tot
        else:
            low = jnp.sum(jnp.where(lane < seg, ysq, 0.0), axis=-1, keepdims=True)
            ssum = jnp.where(lane < seg, low, tot - low)
        yn = y * lax.rsqrt(ssum * (1.0 / seg) + EPS) * gain[:, gi * LANES:(gi + 1) * LANES]
        rot = (yn * cos + pltpu.roll(yn, half, 1) * s_up
               + pltpu.roll(yn, LANES - half, 1) * s_dn)
        outs.append(rot)
    return jnp.concatenate(outs, axis=1)


def _inproj_kernel(x_ref, g_ref, w_ref, gain_ref, bias_ref,
                   ca_ref, sa1_ref, sa2_ref, cb_ref, sb1_ref, sb2_ref,
                   qk_ref, vt_ref, b0_ref, b1_ref, b2_ref, cu_ref, gate_ref, dil_scr, *, tm):
    x = x_ref[...]
    ms = jnp.mean(x * x, axis=-1, keepdims=True)
    h = (x * lax.rsqrt(ms + EPS) * g_ref[...]).astype(BF16)
    dil_refs = (b0_ref, b1_ref, b2_ref)
    for j in range(N_COL_TILES):
        cols = slice(j * COL_TILE, (j + 1) * COL_TILE)
        acc = jnp.dot(h, w_ref[:, cols], preferred_element_type=F32)
        if j in (TILE_AQ, TILE_AK):
            qk_ref[:, cols] = _norm_rope_tile(acc, gain_ref[:, cols], HD_A, HD_A // ROPE_FRACTION // 2,
                                              ca_ref, sa1_ref, sa2_ref).astype(BF16)
        elif j == TILE_AV:
            vt_ref[...] = acc.T.astype(BF16)
        elif j < TILE_CU:
            part, group = divmod(j - TILE_BQ, N_DIL)
            if part < 2:
                acc = _norm_rope_tile(acc, gain_ref[:, cols], HD_B, HD_B // ROPE_FRACTION // 2,
                                      cb_ref, sb1_ref, sb2_ref)
            dil = DIL_PAIRS[group][1]
            out_ref = dil_refs[group]
            if dil == 1:
                out_ref[:, part * WIDTH_B:(part + 1) * WIDTH_B] = acc.astype(BF16)
            else:
                slot = (group - 1) * 3 + part
                for gi in range(WIDTH_B // LANES):
                    dil_scr[slot, gi] = acc[:, gi * LANES:(gi + 1) * LANES]
                for r in range(dil):
                    for gi in range(WIDTH_B // LANES):
                        c0 = (r * 3 + part) * WIDTH_B + gi * LANES
                        out_ref[:, c0:c0 + LANES] = (
                            dil_scr[slot, gi, pl.ds(r, tm // dil, stride=dil), :].astype(BF16))
        elif j == TILE_CU:
            cu_ref[...] = acc
        else:
            g0 = (j - TILE_CU - 1) * COL_TILE
            gate_ref[:, g0:g0 + COL_TILE] = jax.nn.sigmoid(acc + bias_ref[:, cols]).astype(gate_ref.dtype)


def _in_projection(x2d, norm_g, w_bf, layer, col_gain, col_bias, rope, bsz, s_len, tm):
    n = x2d.shape[0]
    tiles_per_seq = s_len // tm
    row = lambda i: (i, 0)
    const = lambda i: (0, 0)
    d1, d2 = DIL_PAIRS[1][1], DIL_PAIRS[2][1]
    kern = functools.partial(_inproj_kernel, tm=tm)
    return pl.pallas_call(
        kern,
        grid=(n // tm,),
        in_specs=[
            pl.BlockSpec((tm, D_MODEL), row),
            pl.BlockSpec((1, D_MODEL), const),
            pl.BlockSpec((None, D_MODEL, IN_COLS), lambda i: (layer, 0, 0), pipeline_mode=pl.Buffered(1)),
            pl.BlockSpec((1, IN_COLS), const),
            pl.BlockSpec((1, IN_COLS), const),
        ] + [pl.BlockSpec((tm, LANES), row)] * 6,
        out_specs=[
            pl.BlockSpec((tm, 2 * A_Q_COLS), row),
            pl.BlockSpec((None, WIDTH_A, tm), lambda i: (i // tiles_per_seq, 0, i % tiles_per_seq)),
            pl.BlockSpec((tm, 3 * WIDTH_B), row),
            pl.BlockSpec((tm // d1, d1 * 3 * WIDTH_B), row),
            pl.BlockSpec((tm // d2, d2 * 3 * WIDTH_B), row),
            pl.BlockSpec((tm, WIDTH_C), row),
            pl.BlockSpec((tm, N_BRANCHES * D_MODEL), row),
        ],
        out_shape=[jax.ShapeDtypeStruct((n, 2 * A_Q_COLS), BF16),
                   jax.ShapeDtypeStruct((bsz, WIDTH_A, s_len), BF16),
                   jax.ShapeDtypeStruct((n, 3 * WIDTH_B), BF16),
                   jax.ShapeDtypeStruct((n // d1, d1 * 3 * WIDTH_B), BF16),
                   jax.ShapeDtypeStruct((n // d2, d2 * 3 * WIDTH_B), BF16),
                   jax.ShapeDtypeStruct((n, WIDTH_C), F32),
                   jax.ShapeDtypeStruct((n, N_BRANCHES * D_MODEL), BF16)],
        scratch_shapes=[pltpu.VMEM((6, WIDTH_B // LANES, tm, LANES), F32)],
        compiler_params=_cparams(("parallel",)),
        name="in_projection",
    )(x2d, norm_g, w_bf, col_gain, col_bias, *rope)


def _diffattn_kernel(q_ref, k_ref, vt_ref, lam_ref, subg_ref, o_ref, acc_scr, s_scr, *, tq, lam_init):
    i = pl.program_id(2)
    q = q_ref[...]
    lane = lax.broadcasted_iota(jnp.int32, (1, 2 * HD_A), 1)
    zero = jnp.zeros_like(q)
    qm = (jnp.where(lane < HD_A, q, zero), jnp.where(lane >= HD_A, q, zero))
    nt = (((1,), (1,)), ((), ()))
    acc_scr[...] = jnp.zeros_like(acc_scr)

    def scores(j, slot):
        r0 = pl.multiple_of(j * tq, tq)
        kblk = k_ref[pl.ds(r0, tq), :]
        for mi in range(2):
            s_scr[slot, mi] = lax.dot_general(kblk, qm[mi], nt, preferred_element_type=F32)

    def consume(j, slot, carry, diagonal):
        r0 = pl.multiple_of(j * tq, tq)
        vtblk = vt_ref[:, pl.ds(r0, tq)]
        new = []
        for mi in range(2):
            m, l = carry[mi]
            st = s_scr[slot, mi]
            if diagonal:
                key = lax.broadcasted_iota(jnp.int32, (tq, tq), 0)
                qry = lax.broadcasted_iota(jnp.int32, (tq, tq), 1)
                st = jnp.where(key <= qry, st, -jnp.inf)
            m_new = jnp.maximum(m, jnp.max(st, axis=0, keepdims=True))
            p = jnp.exp2(st - m_new)
            alpha = jnp.exp2(m - m_new)
            l = alpha * l + jnp.sum(p, axis=0, keepdims=True)
            acc_scr[mi] = alpha * acc_scr[mi] + jnp.dot(vtblk, p.astype(BF16), preferred_element_type=F32)
            new.append((m_new, l))
        return tuple(new)

    def stage(j, src, carry):
        scores(j + 1, 1 - src)
        return consume(j, src, carry, False)

    init1 = (jnp.full((1, tq), -jnp.inf, F32), jnp.zeros((1, tq), F32))
    scores(0, 0)
    carry = lax.fori_loop(0, i // 2, lambda t, c: stage(2 * t + 1, 1, stage(2 * t, 0, c)), (init1, init1))
    odd = lax.rem(i, 2) == 1
    (_, l1), (_, l2) = lax.cond(
        odd,
        lambda c: consume(i, 1, stage(i - 1, 0, c), True),
        lambda c: consume(i, 0, c, True),
        carry)

    lam_p = lam_ref[...]
    lam = (jnp.exp(jnp.sum(lam_p[0:1] * lam_p[1:2], axis=-1, keepdims=True))
           - jnp.exp(jnp.sum(lam_p[2:3] * lam_p[3:4], axis=-1, keepdims=True)) + lam_init)
    o_t = acc_scr[0] / l1 - lam * (acc_scr[1] / l2)
    ms = jnp.mean(o_t * o_t, axis=0, keepdims=True)
    o = (o_t * lax.rsqrt(ms + EPS)).T
    o_ref[...] = ((o * subg_ref[...]) * (1.0 - lam_init)).astype(o_ref.dtype)


def _diff_attention(qk, v_t, lam_p, subln_g, lam_init, tq):
    bsz, s_len, _ = qk.shape
    proj_bf = qk
    kern = functools.partial(_diffattn_kernel, tq=tq, lam_init=lam_init)
    return pl.pallas_call(
        kern,
        grid=(bsz, N_HEADS_A, s_len // tq),
        in_specs=[
            pl.BlockSpec((None, tq, 2 * HD_A), lambda b, h, i: (b, i, h)),
            pl.BlockSpec((None, s_len, 2 * HD_A), lambda b, h, i: (b, 0, N_HEADS_A + h)),
            pl.BlockSpec((None, 2 * HD_A, s_len), lambda b, h, i: (b, h, 0)),
            pl.BlockSpec((4, HD_A), lambda b, h, i: (0, 0)),
            pl.BlockSpec((1, 2 * HD_A), lambda b, h, i: (0, 0)),
        ],
        out_specs=pl.BlockSpec((None, tq, 2 * HD_A), lambda b, h, i: (b, i, h)),
        out_shape=jax.ShapeDtypeStruct((bsz, s_len, WIDTH_A), BF16),
        scratch_shapes=[pltpu.VMEM((2, 2 * HD_A, tq), F32), pltpu.VMEM((2, 2, tq, tq), F32)],
        compiler_params=_cparams(("parallel", "parallel", "arbitrary")),
        name="diff_attention",
    )(proj_bf, proj_bf, v_t, lam_p, subln_g)


def _dilated_kernel(q_ref, k_ref, kp_ref, v_ref, vp_ref, o_ref, lse_ref, *, tq, blk):
    n = pl.program_id(2)
    scale = 1.0 / math.sqrt(HD_B)
    nt = (((1,), (1,)), ((), ()))
    rr = lax.broadcasted_iota(jnp.int32, (blk, 2 * blk), 0)
    cc = lax.broadcasted_iota(jnp.int32, (blk, 2 * blk), 1)
    band = jnp.logical_and(cc >= rr, cc <= rr + blk)
    lane = lax.broadcasted_iota(jnp.int32, (1, 2 * blk), 1)
    first_bias = jnp.where(lane >= blk, 0.0, jnp.where(n > 0, 0.0, -jnp.inf).astype(F32))

    def window(ref, pref, c, cols):
        if c == 0:
            return jnp.concatenate([pref[:, cols], ref[0:blk, cols]], axis=0)
        return ref[(c - 1) * blk:(c + 1) * blk, cols]

    def scores(unit):
        h, c = unit
        cols = slice(h * HD_B, (h + 1) * HD_B)
        qh = q_ref[c * blk:(c + 1) * blk, cols]
        s = lax.dot_general(qh, window(k_ref, kp_ref, c, cols), nt, preferred_element_type=F32) * scale
        s = jnp.where(band, s, -jnp.inf)
        return s + first_bias if c == 0 else s

    head_lane = lax.broadcasted_iota(jnp.int32, (1, LANES), 1) // LSE_LANES

    def finish(unit, s, lse_acc):
        h, c = unit
        cols = slice(h * HD_B, (h + 1) * HD_B)
        rows = slice(c * blk, (c + 1) * blk)
        m = jnp.max(s, axis=-1, keepdims=True)
        p = jnp.exp(s - m)
        den = jnp.sum(p, axis=-1, keepdims=True)
        pv = jnp.dot(p.astype(BF16), window(v_ref, vp_ref, c, cols), preferred_element_type=F32)
        o_ref[rows, cols] = (pv / den).astype(o_ref.dtype)
        lse = jnp.broadcast_to(m + jnp.log(den), (blk, LANES))
        lse_acc = lse if h == 0 else jnp.where(head_lane == h, lse, lse_acc)
        if h == N_HEADS_B - 1:
            lse_ref[rows, :] = lse_acc
        return lse_acc

    units = [(h, c) for c in range(tq // blk) for h in range(N_HEADS_B)]
    s = scores(units[0])
    lse_acc = None
    for idx, unit in enumerate(units):
        s_next = scores(units[idx + 1]) if idx + 1 < len(units) else None
        lse_acc = finish(unit, s, lse_acc)
        s = s_next


def _dilated_attention(qkv, bsz, s_len, group, tq):
    window, dil = DIL_PAIRS[group]
    blk = window // dil
    rows = s_len // dil
    tq = min(tq, rows)
    per_res = 3
    view = qkv.reshape(bsz, rows, dil * per_res * WIDTH_B)
    sub = tq // blk
    qcol = lambda r: r * per_res
    kcol = lambda r: r * per_res + 1
    vcol = lambda r: r * per_res + 2
    prev = lambda n: jnp.maximum(n * sub - 1, 0)
    kern = functools.partial(_dilated_kernel, tq=tq, blk=blk)
    o, lse = pl.pallas_call(
        kern,
        grid=(bsz, dil, rows // tq),
        in_specs=[
            pl.BlockSpec((None, tq, WIDTH_B), lambda b, r, n: (b, n, qcol(r))),
            pl.BlockSpec((None, tq, WIDTH_B), lambda b, r, n: (b, n, kcol(r))),
            pl.BlockSpec((None, blk, WIDTH_B), lambda b, r, n: (b, prev(n), kcol(r))),
            pl.BlockSpec((None, tq, WIDTH_B), lambda b, r, n: (b, n, vcol(r))),
            pl.BlockSpec((None, blk, WIDTH_B), lambda b, r, n: (b, prev(n), vcol(r))),
        ],
        out_specs=[pl.BlockSpec((None, tq, WIDTH_B), lambda b, r, n: (b, n, r)),
                   pl.BlockSpec((None, tq, LANES), lambda b, r, n: (b, n, r))],
        out_shape=[jax.ShapeDtypeStruct((bsz, rows, dil * WIDTH_B), BF16),
                   jax.ShapeDtypeStruct((bsz, rows, dil * LANES), F32)],
        compiler_params=_cparams(("parallel", "parallel", "arbitrary")),
        name=f"dilated_attention_g{group}",
    )(view, view, view, view, view)
    return o.reshape(bsz * rows, dil * WIDTH_B), lse.reshape(bsz * rows, dil * LANES)


def _ssm_kernel(u_ref, bd_ref, cd_ref, pw_ref, dskip_ref, wglu_ref, bglu_ref, o_ref,
                x_scr, carry_scr, *, t_chunk):
    @pl.when(pl.program_id(1) == 0)
    def _():
        carry_scr[...] = jnp.zeros_like(carry_scr)

    u = u_ref[...]
    ub = u.astype(BF16)
    re = slice(0, N_STATE)
    im = slice(N_STATE, 2 * N_STATE)
    ch = WIDTH_C // 2
    sh = N_STATE // 2
    c_re, c_im = carry_scr[:, re], carry_scr[:, im]
    n_slabs = t_chunk // SSM_SLAB
    slab_rows = lambda s: slice(s * SSM_SLAB, (s + 1) * SSM_SLAB)

    def drive(s):
        for hh in range(2):
            for part in range(2):
                c0 = part * N_STATE + hh * sh
                x_scr[slab_rows(s), c0:c0 + sh] = jnp.dot(ub[slab_rows(s), hh * ch:(hh + 1) * ch],
                                                          bd_ref[hh * ch:(hh + 1) * ch, c0:c0 + sh],
                                                          preferred_element_type=F32)

    drive(0)
    for s in range(n_slabs):
        rows = slab_rows(s)
        if s + 1 < n_slabs:
            drive(s + 1)
        for t in range(SSM_SLAB // SUBLANES):
            r8 = slice(s * SSM_SLAB + t * SUBLANES, s * SSM_SLAB + (t + 1) * SUBLANES)
            x_re = x_scr[r8, re]
            x_im = x_scr[r8, im]
            for si, shift in enumerate((1, 2, 4)):
                a = pw_ref[si, :, re]
                b = pw_ref[si, :, im]
                s_re = pltpu.roll(x_re, shift, 0)
                s_im = pltpu.roll(x_im, shift, 0)
                x_re, x_im = x_re + (a * s_re - b * s_im), x_im + (a * s_im + b * s_re)
            a = pw_ref[3, :, re]
            b = pw_ref[3, :, im]
            x_re, x_im = x_re + (a * c_re - b * c_im), x_im + (a * c_im + b * c_re)
            x_scr[r8, re] = x_re
            x_scr[r8, im] = x_im
            last = SUBLANES - 1
            c_re = jnp.broadcast_to(x_re[last:, :], x_re.shape)
            c_im = jnp.broadcast_to(x_im[last:, :], x_im.shape)
        halves = []
        for hh in range(2):
            acc = None
            for part in range(2):
                c0 = part * N_STATE + hh * sh
                d = jnp.dot(x_scr[rows, c0:c0 + sh].astype(BF16), cd_ref[c0:c0 + sh, hh * ch:(hh + 1) * ch],
                            preferred_element_type=F32)
                acc = d if acc is None else acc + d
            halves.append(acc)
        y = jnp.concatenate(halves, axis=1) + dskip_ref[...] * u[rows]
        cdf = 0.5 * (1.0 + jnp.tanh(math.sqrt(2.0 / math.pi) * (y + 0.044715 * (y * y * y))))
        z = y * cdf
        gate = jnp.dot(z.astype(BF16), wglu_ref[...], preferred_element_type=F32) + bglu_ref[...]
        o_ref[rows, :] = (z * jax.nn.sigmoid(gate)).astype(o_ref.dtype)
    carry_scr[:, re] = c_re
    carry_scr[:, im] = c_im


def _ssm_operands(a_re, a_im, log_dt, b_re, b_im, c_re, c_im):
    a_re, a_im = a_re.astype(F32), a_im.astype(F32)
    dt = jnp.exp(log_dt.astype(F32))[:, None]
    mag = jnp.exp(a_re * dt)
    lb_re, lb_im = mag * jnp.cos(a_im * dt), mag * jnp.sin(a_im * dt)
    n_re, n_im = lb_re - 1.0, lb_im
    den = a_re * a_re + a_im * a_im
    f_re = (n_re * a_re + n_im * a_im) / den
    f_im = (n_im * a_re - n_re * a_im) / den
    b_re, b_im = b_re.astype(F32), b_im.astype(F32)
    bb_re = f_re[..., None] * b_re - f_im[..., None] * b_im
    bb_im = f_re[..., None] * b_im + f_im[..., None] * b_re
    eye = jnp.eye(N_GROUPS_C, dtype=F32)
    blockdiag_in = lambda t: jnp.einsum('gpc,gh->gchp', t, eye).reshape(WIDTH_C, N_STATE)
    bd = jnp.concatenate([blockdiag_in(bb_re), blockdiag_in(bb_im)], axis=1).astype(BF16)
    blockdiag_out = lambda t: jnp.einsum('gcp,gh->gphc', t, eye).reshape(N_STATE, WIDTH_C)
    cd = jnp.concatenate([blockdiag_out(c_re.astype(F32)), -blockdiag_out(c_im.astype(F32))],
                         axis=0).astype(BF16)
    rows = jnp.arange(SUBLANES, dtype=F32)[:, None]
    flat = lambda t: t.reshape(1, N_STATE)

    def power(k):
        return (flat(mag) ** k) * jnp.cos(flat(a_im * dt) * k), (flat(mag) ** k) * jnp.sin(flat(a_im * dt) * k)

    tabs = []
    for shift in (1, 2, 4):
        p_re, p_im = power(float(shift))
        keep = rows >= shift
        tabs.append(jnp.concatenate([jnp.where(keep, p_re, 0.0), jnp.where(keep, p_im, 0.0)], axis=1))
    p_re, p_im = power(rows + 1.0)
    tabs.append(jnp.concatenate([p_re, p_im], axis=1))
    return bd, cd, jnp.stack(tabs, axis=0)


def _ssm_branch(proj_f32, bsz, s_len, bd, cd, pw, d_skip, w_glu_bf, layer, b_glu, t_chunk):
    n_chunks = s_len // t_chunk
    const2 = lambda b, c: (0, 0)
    kern = functools.partial(_ssm_kernel, t_chunk=t_chunk)
    return pl.pallas_call(
        kern,
        grid=(bsz, n_chunks),
        in_specs=[
            pl.BlockSpec((t_chunk, WIDTH_C), lambda b, c: (b * n_chunks + c, 0)),
            pl.BlockSpec((WIDTH_C, 2 * N_STATE), const2),
            pl.BlockSpec((2 * N_STATE, WIDTH_C), const2),
            pl.BlockSpec((4, SUBLANES, 2 * N_STATE), lambda b, c: (0, 0, 0)),
            pl.BlockSpec((1, WIDTH_C), const2),
            pl.BlockSpec((None, WIDTH_C, WIDTH_C), lambda b, c: (layer, 0, 0)),
            pl.BlockSpec((1, WIDTH_C), const2),
        ],
        out_specs=pl.BlockSpec((t_chunk, WIDTH_C), lambda b, c: (b * n_chunks + c, 0)),
        out_shape=jax.ShapeDtypeStruct((bsz * s_len, WIDTH_C), BF16),
        scratch_shapes=[pltpu.VMEM((t_chunk, 2 * N_STATE), F32),
                        pltpu.VMEM((SUBLANES, 2 * N_STATE), F32)],
        compiler_params=_cparams(("parallel", "arbitrary")),
        name="s5_scan_glu",
    )(proj_f32, bd, cd, pw, d_skip, w_glu_bf, b_glu)


def _merge_kernel(x_ref, oa_ref, o0_ref, o1_ref, o2_ref, l0_ref, l1_ref, l2_ref, oc_ref,
                  g0_ref, g1_ref, g2_ref, wa_ref, wb_ref, wc_ref, wo_ref, out_ref, tok_scr, lse_scr, *, tm):
    for gi, (o_ref, l_ref) in enumerate(((o1_ref, l1_ref), (o2_ref, l2_ref))):
        dil = DIL_PAIRS[gi + 1][1]
        for r in range(dil):
            lse_scr[gi, pl.ds(r, tm // dil, stride=dil), :] = l_ref[:, r * LANES:(r + 1) * LANES]
            for li in range(WIDTH_B // LANES):
                cols = slice(r * WIDTH_B + li * LANES, r * WIDTH_B + (li + 1) * LANES)
                tok_scr[gi, li, pl.ds(r, tm // dil, stride=dil), :] = o_ref[:, cols].astype(F32)
    tok = lambda k: jnp.concatenate([tok_scr[k, li] for li in range(WIDTH_B // LANES)], axis=1)
    l0, l1, l2 = l0_ref[...], lse_scr[0], lse_scr[1]
    m = jnp.maximum(jnp.maximum(l0, l1), l2)
    e0, e1, e2 = jnp.exp(l0 - m), jnp.exp(l1 - m), jnp.exp(l2 - m)
    tot = e0 + e1 + e2
    sel_r = lax.broadcasted_iota(jnp.int32, (LANES, WIDTH_B), 0)
    sel_c = lax.broadcasted_iota(jnp.int32, (LANES, WIDTH_B), 1)
    select = jnp.where(sel_r == LSE_LANES * (sel_c // HD_B), 1.0, 0.0).astype(BF16)

    def spread(w):
        hi = w.astype(BF16)
        lo = (w - hi.astype(F32)).astype(BF16)
        return (jnp.dot(hi, select, preferred_element_type=F32)
                + jnp.dot(lo, select, preferred_element_type=F32))

    ob = (spread(e0 / tot) * o0_ref[...].astype(F32) + spread(e1 / tot) * tok(0)
          + spread(e2 / tot) * tok(1))
    ya = jnp.dot(oa_ref[...], wa_ref[...], preferred_element_type=F32)
    yb = jnp.dot(ob.astype(BF16), wb_ref[...], preferred_element_type=F32)
    yc = jnp.dot(oc_ref[...], wc_ref[...], preferred_element_type=F32)
    merged = (g0_ref[...].astype(F32) * ya + g1_ref[...].astype(F32) * yb
              + g2_ref[...].astype(F32) * yc)
    out_ref[...] = x_ref[...] + jnp.dot(merged.astype(BF16), wo_ref[...], preferred_element_type=F32)


def _merge(x2d, oa, ob_parts, lse_parts, oc, gates, wa, wb, wc, wo, layer, tm):
    n = x2d.shape[0]
    row = lambda i: (i, 0)
    half = pl.BlockSpec((tm, WIDTH_C), row)
    full = pl.BlockSpec((tm, D_MODEL), row)
    wspec_half = pl.BlockSpec((None, WIDTH_C, D_MODEL), lambda i: (layer, 0, 0))
    gspec = lambda k: pl.BlockSpec((tm, D_MODEL), lambda i: (i, k))
    dilated = lambda g, width: pl.BlockSpec((tm // DIL_PAIRS[g][1], DIL_PAIRS[g][1] * width), row)
    kern = functools.partial(_merge_kernel, tm=tm)
    return pl.pallas_call(
        kern,
        grid=(n // tm,),
        in_specs=[full, half, half, dilated(1, WIDTH_B), dilated(2, WIDTH_B),
                  pl.BlockSpec((tm, LANES), row), dilated(1, LANES), dilated(2, LANES), half,
                  gspec(0), gspec(1), gspec(2),
                  wspec_half, wspec_half, wspec_half,
                  pl.BlockSpec((None, D_MODEL, D_MODEL), lambda i: (layer, 0, 0))],
        out_specs=full,
        out_shape=jax.ShapeDtypeStruct((n, D_MODEL), F32),
        scratch_shapes=[pltpu.VMEM((2, WIDTH_B // LANES, tm, LANES), F32),
                        pltpu.VMEM((2, tm, LANES), F32)],
        compiler_params=_cparams(("parallel",)),
        name="gated_merge",
    )(x2d, oa, *ob_parts, *lse_parts, oc, gates, gates, gates, wa, wb, wc, wo)


def _ffn_kernel(x_ref, g_ref, wup_ref, cw_ref, cb_ref, wdown_ref, o_ref, a_scr, *, tm, tiles_per_seq):
    halo = SUBLANES
    x = x_ref[...]
    ms = jnp.mean(x * x, axis=-1, keepdims=True)
    h = (x * lax.rsqrt(ms + EPS) * g_ref[...]).astype(BF16)

    @pl.when(pl.program_id(0) % tiles_per_seq == 0)
    def _():
        a_scr[0:halo, :] = jnp.zeros((halo, D_FF), F32)

    @pl.when(pl.program_id(0) % tiles_per_seq != 0)
    def _():
        a_scr[0:halo, :] = a_scr[tm:tm + halo, :]

    n_tiles = D_FF // FFN_TILE
    tile_cols = lambda f: slice(f * FFN_TILE, (f + 1) * FFN_TILE)

    def up(f):
        a_scr[halo:halo + tm, tile_cols(f)] = jnp.dot(h, wup_ref[:, tile_cols(f)], preferred_element_type=F32)
        return jnp.dot(h, wup_ref[:, D_FF + f * FFN_TILE:D_FF + (f + 1) * FFN_TILE],
                       preferred_element_type=F32)

    y = x
    gate = up(0)
    for f in range(n_tiles):
        cols = tile_cols(f)
        next_gate = up(f + 1) if f + 1 < n_tiles else None
        conv = (cb_ref[:, cols] + cw_ref[0:1, cols] * a_scr[halo - 2:halo - 2 + tm, cols]
                + cw_ref[1:2, cols] * a_scr[halo - 1:halo - 1 + tm, cols]
                + cw_ref[2:3, cols] * a_scr[halo:halo + tm, cols])
        act = (conv * jax.nn.sigmoid(conv)) * gate
        y = y + jnp.dot(act.astype(BF16), wdown_ref[cols, :], preferred_element_type=F32)
        gate = next_gate
    o_ref[...] = y


def _conv_ffn(x2d, norm_g, wup_bf, conv_w, conv_b, wdown_bf, layer, s_len, tm):
    n = x2d.shape[0]
    row = lambda i: (i, 0)
    const = lambda i: (0, 0)
    kern = functools.partial(_ffn_kernel, tm=tm, tiles_per_seq=s_len // tm)
    return pl.pallas_call(
        kern,
        grid=(n // tm,),
        in_specs=[
            pl.BlockSpec((tm, D_MODEL), row),
            pl.BlockSpec((1, D_MODEL), const),
            pl.BlockSpec((None, D_MODEL, 2 * D_FF), lambda i: (layer, 0, 0), pipeline_mode=pl.Buffered(1)),
            pl.BlockSpec((CONV_WIDTH, D_FF), const),
            pl.BlockSpec((1, D_FF), const),
            pl.BlockSpec((None, D_FF, D_MODEL), lambda i: (layer, 0, 0), pipeline_mode=pl.Buffered(1)),
        ],
        out_specs=pl.BlockSpec((tm, D_MODEL), row),
        out_shape=jax.ShapeDtypeStruct((n, D_MODEL), F32),
        scratch_shapes=[pltpu.VMEM((tm + SUBLANES, D_FF), F32)],
        compiler_params=_cparams(("arbitrary",)),
        name="conv_ffn",
    )(x2d, norm_g, wup_bf, conv_w, conv_b, wdown_bf)


def _rope_tables(positions, head_dim):
    rot = head_dim // ROPE_FRACTION
    half = rot // 2
    inv = ROPE_THETA ** (-jnp.arange(0, rot, 2, dtype=F32) / rot)
    ang = positions.reshape(-1).astype(F32)[:, None] * inv
    cos, sin = jnp.cos(ang), jnp.sin(ang)
    n = ang.shape[0]
    rest = head_dim - rot
    c = jnp.concatenate([cos, cos, jnp.ones((n, rest), F32)], axis=1)
    s_up = jnp.concatenate([jnp.zeros((n, half), F32), sin, jnp.zeros((n, rest), F32)], axis=1)
    s_dn = jnp.concatenate([-sin, jnp.zeros((n, half + rest), F32)], axis=1)
    reps = LANES // head_dim
    return tuple(jnp.tile(t, (1, reps)) for t in (c, s_up, s_dn))


def kernel(x, positions, attn_norm_g, w_in, b_gate, qn_a, kn_a, lam_q1, lam_k1, lam_q2, lam_k2, subln_g, w_br_a, qn_b, kn_b, w_br_b, ssm_a_re, ssm_a_im, ssm_log_dt, ssm_b_re, ssm_b_im, ssm_c_re, ssm_c_im, ssm_d, w_glu, b_glu, w_br_c, w_out, ffn_norm_g, w_up, conv_w, conv_b, w_down):
    bsz, s_len, d_model = x.shape
    depth = w_in.shape[0]
    assert d_model == D_MODEL and w_in.shape[2] == IN_COLS
    assert s_len % DIL_PAIRS[-1][0] == 0, "sequence must be a multiple of the largest dilated window"
    n = bsz * s_len

    tm_proj = 256
    tq_a = min(512, s_len)
    tq_b = 512
    t_ssm = min(512, s_len)
    tm_merge = min(256, n)
    tm_ffn = min(256, s_len)

    rope = _rope_tables(positions, HD_A) + _rope_tables(positions, HD_B)
    w_in_bf = w_in.astype(BF16)
    wa_bf, wb_bf, wc_bf, wo_bf = (w.astype(BF16) for w in (w_br_a, w_br_b, w_br_c, w_out))
    wglu_bf, wup_bf, wdown_bf = w_glu.astype(BF16), w_up.astype(BF16), w_down.astype(BF16)

    x2d = x.reshape(n, D_MODEL)
    for l in range(depth):
        lam_init = 0.8 - 0.6 * math.exp(-0.3 * l)
        ones = lambda w: jnp.ones((w,), F32)
        col_gain = jnp.concatenate([
            jnp.tile(qn_a[l].astype(F32) * (math.log2(math.e) / math.sqrt(HD_A)), A_Q_COLS // HD_A),
            jnp.tile(kn_a[l].astype(F32), A_Q_COLS // HD_A), ones(WIDTH_A),
            jnp.tile(qn_b[l].astype(F32), B_COLS // HD_B), jnp.tile(kn_b[l].astype(F32), B_COLS // HD_B),
            ones(B_COLS + WIDTH_C + N_BRANCHES * D_MODEL)]).reshape(1, IN_COLS)
        col_bias = jnp.concatenate([jnp.zeros((IN_COLS - N_BRANCHES * D_MODEL,), F32),
                                    b_gate[l].astype(F32)]).reshape(1, IN_COLS)
        qk_a, v_t, *qkv_b, c_u, gates = _in_projection(
            x2d, attn_norm_g[l].reshape(1, D_MODEL).astype(F32), w_in_bf, l, col_gain, col_bias, rope,
            bsz, s_len, tm_proj)

        lam_p = jnp.stack([lam_q1[l], lam_k1[l], lam_q2[l], lam_k2[l]]).astype(F32)
        oa = _diff_attention(qk_a.reshape(bsz, s_len, 2 * A_Q_COLS), v_t, lam_p,
                             subln_g[l].reshape(1, 2 * HD_A).astype(F32), lam_init, tq_a)
        oa = oa.reshape(n, WIDTH_A)

        ob_parts, lse_parts = zip(*[_dilated_attention(qkv_b[g], bsz, s_len, g, tq_b) for g in range(N_DIL)])

        bd, cd, pw = _ssm_operands(ssm_a_re[l], ssm_a_im[l], ssm_log_dt[l], ssm_b_re[l], ssm_b_im[l],
                                   ssm_c_re[l], ssm_c_im[l])
        oc = _ssm_branch(c_u, bsz, s_len, bd, cd, pw, ssm_d[l].reshape(1, WIDTH_C).astype(F32),
                         wglu_bf, l, b_glu[l].reshape(1, WIDTH_C).astype(F32), t_ssm)

        x2d = _merge(x2d, oa, ob_parts, lse_parts, oc, gates, wa_bf, wb_bf, wc_bf, wo_bf, l, tm_merge)
        x2d = _conv_ffn(x2d, ffn_norm_g[l].reshape(1, D_MODEL).astype(F32), wup_bf,
                        conv_w[l].astype(F32), conv_b[l].reshape(1, D_FF).astype(F32), wdown_bf, l,
                        s_len, tm_ffn)
    return x2d.reshape(bsz, s_len, D_MODEL)
```

```python
import functools
import math

import jax
import jax.numpy as jnp
from jax import lax
from jax.experimental import pallas as pl
from jax.experimental.pallas import tpu as pltpu

F32 = jnp.float32
BF16 = jnp.bfloat16

LANES = 128
SUBLANES = 8

D_MODEL = 1024
N_HEADS_A = 4
HD_A = 64
N_DIL = 3
DIL_PAIRS = ((128, 1), (512, 4), (2048, 16))
N_HEADS_B = 4
HD_B = 128
WIDTH_B = N_HEADS_B * HD_B
SSM_GROUP = 16
SSM_STATE = 64
WIDTH_C = 512
N_GROUPS_C = WIDTH_C // SSM_GROUP
N_STATE = N_GROUPS_C * SSM_STATE
N_BRANCHES = 3
D_FF = 2816
CONV_WIDTH = 3
ROPE_THETA = 500000.0
ROPE_FRACTION = 4
EPS = 1e-6

COL_TILE = 512
A_Q_COLS = 2 * N_HEADS_A * HD_A
WIDTH_A = N_HEADS_A * 2 * HD_A
B_COLS = N_DIL * N_HEADS_B * HD_B
IN_COLS = 2 * A_Q_COLS + WIDTH_A + 3 * B_COLS + WIDTH_C + N_BRANCHES * D_MODEL
N_COL_TILES = IN_COLS // COL_TILE
BF_TILES = (2 * A_Q_COLS + WIDTH_A + 3 * B_COLS) // COL_TILE
BF_COLS = BF_TILES * COL_TILE
TILE_AQ, TILE_AK, TILE_AV = 0, 1, 2
TILE_BQ, TILE_BK, TILE_BV = 3, 6, 9
TILE_CU = 12

HEADS_PER_STEP = 2
VT_PAD = 16
VT_ROWS = 2 * HD_A + VT_PAD
LSE_LANES = LANES // N_HEADS_B
SSM_SLAB = 128
SSM_TAPS = SUBLANES
FFN_TILE = 256

VMEM_LIMIT = 56 * 1024 * 1024


def _cparams(sem):
    return pltpu.CompilerParams(dimension_semantics=sem, vmem_limit_bytes=VMEM_LIMIT)


def _norm_rope_tile(acc, gain, seg, half, c_ref, s1_ref, s2_ref):
    cos = c_ref[...]
    s_up = s1_ref[...]
    s_dn = s2_ref[...]
    lane = lax.broadcasted_iota(jnp.int32, (1, LANES), 1)
    outs = []
    for gi in range(acc.shape[1] // LANES):
        y = acc[:, gi * LANES:(gi + 1) * LANES]
        ysq = y * y
        tot = jnp.sum(ysq, axis=-1, keepdims=True)
        if seg == LANES:
            ssum = tot
        else:
            low = jnp.sum(jnp.where(lane < seg, ysq, 0.0), axis=-1, keepdims=True)
            ssum = jnp.where(lane < seg, low, tot - low)
        yn = y * lax.rsqrt(ssum * (1.0 / seg) + EPS) * gain[:, gi * LANES:(gi + 1) * LANES]
        rot = (yn * cos + pltpu.roll(yn, half, 1) * s_up
               + pltpu.roll(yn, LANES - half, 1) * s_dn)
        outs.append(rot)
    return jnp.concatenate(outs, axis=1)


def _inproj_kernel(x_ref, g_ref, w_ref, gain_ref, bias_ref,
                   ca_ref, sa1_ref, sa2_ref, cb_ref, sb1_ref, sb2_ref,
                   qk_ref, vt_ref, b0_ref, b1_ref, b2_ref, cu_ref, gate_ref, dil_scr, *, tm):
    x = x_ref[...]
    ms = jnp.mean(x * x, axis=-1, keepdims=True)
    h = (x * lax.rsqrt(ms + EPS) * g_ref[...]).astype(BF16)
    dil_refs = (b0_ref, b1_ref, b2_ref)
    heavy = [TILE_AQ, TILE_AK] + list(range(TILE_BQ, TILE_BV))
    light = [TILE_AV] + list(range(TILE_BV, TILE_CU + 1))
    order = heavy + [j for j in range(N_COL_TILES) if j not in heavy + light] + light
    for j in order:
        cols = slice(j * COL_TILE, (j + 1) * COL_TILE)
        acc = jnp.dot(h, w_ref[:, cols], preferred_element_type=F32)
        if j in (TILE_AQ, TILE_AK):
            qk_ref[:, cols] = _norm_rope_tile(acc, gain_ref[:, cols], HD_A, HD_A // ROPE_FRACTION // 2,
                                              ca_ref, sa1_ref, sa2_ref).astype(BF16)
        elif j == TILE_AV:
            acc_t = acc.T.astype(BF16)
            for hd in range(N_HEADS_A):
                vt_ref[hd * VT_ROWS:hd * VT_ROWS + 2 * HD_A, :] = acc_t[hd * 2 * HD_A:(hd + 1) * 2 * HD_A, :]
                vt_ref[hd * VT_ROWS + 2 * HD_A:(hd + 1) * VT_ROWS, :] = jnp.ones((VT_PAD, tm), BF16)
        elif j < TILE_CU:
            part, group = divmod(j - TILE_BQ, N_DIL)
            if part < 2:
                acc = _norm_rope_tile(acc, gain_ref[:, cols], HD_B, HD_B // ROPE_FRACTION // 2,
                                      cb_ref, sb1_ref, sb2_ref)
            dil = DIL_PAIRS[group][1]
            out_ref = dil_refs[group]
            if dil == 1:
                out_ref[:, part * WIDTH_B:(part + 1) * WIDTH_B] = acc.astype(BF16)
            else:
                slot = (group - 1) * 3 + part
                for gi in range(WIDTH_B // LANES):
                    dil_scr[slot, gi] = acc[:, gi * LANES:(gi + 1) * LANES]
                for r in range(dil):
                    for gi in range(WIDTH_B // LANES):
                        c0 = (r * 3 + part) * WIDTH_B + gi * LANES
                        out_ref[:, c0:c0 + LANES] = (
                            dil_scr[slot, gi, pl.ds(r, tm // dil, stride=dil), :].astype(BF16))
        elif j == TILE_CU:
            cu_ref[...] = acc
        else:
            g0 = (j - TILE_CU - 1) * COL_TILE
            gate_ref[:, g0:g0 + COL_TILE] = jax.nn.sigmoid(acc + bias_ref[:, cols]).astype(gate_ref.dtype)


def _in_projection(x2d, norm_g, w_bf, layer, col_gain, col_bias, rope, bsz, s_len, tm):
    n = x2d.shape[0]
    tiles_per_seq = s_len // tm
    row = lambda i: (i, 0)
    const = lambda i: (0, 0)
    d1, d2 = DIL_PAIRS[1][1], DIL_PAIRS[2][1]
    kern = functools.partial(_inproj_kernel, tm=tm)
    return pl.pallas_call(
        kern,
        grid=(n // tm,),
        in_specs=[
            pl.BlockSpec((tm, D_MODEL), row),
            pl.BlockSpec((1, D_MODEL), const),
            pl.BlockSpec((None, D_MODEL, IN_COLS), lambda i: (layer, 0, 0), pipeline_mode=pl.Buffered(1)),
            pl.BlockSpec((1, IN_COLS), const),
            pl.BlockSpec((1, IN_COLS), const),
        ] + [pl.BlockSpec((tm, LANES), row)] * 6,
        out_specs=[
            pl.BlockSpec((tm, 2 * A_Q_COLS), row),
            pl.BlockSpec((None, N_HEADS_A * VT_ROWS, tm), lambda i: (i // tiles_per_seq, 0, i % tiles_per_seq)),
            pl.BlockSpec((tm, 3 * WIDTH_B), row),
            pl.BlockSpec((tm // d1, d1 * 3 * WIDTH_B), row),
            pl.BlockSpec((tm // d2, d2 * 3 * WIDTH_B), row),
            pl.BlockSpec((tm, WIDTH_C), row),
            pl.BlockSpec((tm, N_BRANCHES * D_MODEL), row),
        ],
        out_shape=[jax.ShapeDtypeStruct((n, 2 * A_Q_COLS), BF16),
                   jax.ShapeDtypeStruct((bsz, N_HEADS_A * VT_ROWS, s_len), BF16),
                   jax.ShapeDtypeStruct((n, 3 * WIDTH_B), BF16),
                   jax.ShapeDtypeStruct((n // d1, d1 * 3 * WIDTH_B), BF16),
                   jax.ShapeDtypeStruct((n // d2, d2 * 3 * WIDTH_B), BF16),
                   jax.ShapeDtypeStruct((n, WIDTH_C), F32),
                   jax.ShapeDtypeStruct((n, N_BRANCHES * D_MODEL), BF16)],
        scratch_shapes=[pltpu.VMEM((6, WIDTH_B // LANES, tm, LANES), F32)],
        compiler_params=_cparams(("parallel",)),
        name="in_projection",
    )(x2d, norm_g, w_bf, col_gain, col_bias, *rope)


def _diffattn_kernel(q_ref, k_ref, vt_ref, mask_ref, lam_ref, subg_ref, o_ref, acc_scr, s_scr, *, tq, lam_init):
    i = pl.program_id(2)
    q = q_ref[...]
    lane = lax.broadcasted_iota(jnp.int32, (1, 2 * HD_A), 1)
    nt = (((1,), (1,)), ((), ()))
    qm = []
    for hh in range(HEADS_PER_STEP):
        qh = q[:, hh * 2 * HD_A:(hh + 1) * 2 * HD_A]
        zero = jnp.zeros_like(qh)
        qm += [jnp.where(lane < HD_A, qh, zero), jnp.where(lane >= HD_A, qh, zero)]
    n_maps = len(qm)
    acc_scr[...] = jnp.zeros_like(acc_scr)

    def scores(j, slot):
        r0 = pl.multiple_of(j * tq, tq)
        for mi in range(n_maps):
            hh = mi // 2
            kblk = k_ref[pl.ds(r0, tq), hh * 2 * HD_A:(hh + 1) * 2 * HD_A]
            s_scr[slot, mi] = lax.dot_general(kblk, qm[mi], nt, preferred_element_type=F32)

    def consume(j, slot, ms, diagonal):
        r0 = pl.multiple_of(j * tq, tq)
        new = []
        for mi in range(n_maps):
            hh = mi // 2
            vtblk = vt_ref[hh * VT_ROWS:(hh + 1) * VT_ROWS, pl.ds(r0, tq)]
            st = s_scr[slot, mi]
            if diagonal:
                st = st + mask_ref[...]
            m_new = jnp.maximum(ms[mi], jnp.max(st, axis=0, keepdims=True))
            p = jnp.exp2(st - m_new)
            alpha = jnp.exp2(ms[mi] - m_new)
            acc_scr[mi] = alpha * acc_scr[mi] + jnp.dot(vtblk, p.astype(BF16), preferred_element_type=F32)
            new.append(m_new)
        return tuple(new)

    def stage(j, src, ms):
        scores(j + 1, 1 - src)
        return consume(j, src, ms, False)

    init = tuple(jnp.full((1, tq), -jnp.inf, F32) for _ in range(n_maps))
    scores(0, 0)
    ms = lax.fori_loop(0, i // 2, lambda t, c: stage(2 * t + 1, 1, stage(2 * t, 0, c)), init)
    odd = lax.rem(i, 2) == 1

    @pl.when(odd)
    def _():
        consume(i, 1, stage(i - 1, 0, ms), True)

    @pl.when(jnp.logical_not(odd))
    def _():
        consume(i, 0, ms, True)

    lam_p = lam_ref[...]
    lam = (jnp.exp(jnp.sum(lam_p[0:1] * lam_p[1:2], axis=-1, keepdims=True))
           - jnp.exp(jnp.sum(lam_p[2:3] * lam_p[3:4], axis=-1, keepdims=True)) + lam_init)
    vals = slice(0, 2 * HD_A)
    den = slice(2 * HD_A, 2 * HD_A + 1)
    for hh in range(HEADS_PER_STEP):
        a1, a2 = acc_scr[2 * hh], acc_scr[2 * hh + 1]
        o_t = a1[vals] / a1[den] - lam * (a2[vals] / a2[den])
        msq = jnp.mean(o_t * o_t, axis=0, keepdims=True)
        o = (o_t * lax.rsqrt(msq + EPS)).T
        o_ref[:, hh * 2 * HD_A:(hh + 1) * 2 * HD_A] = ((o * subg_ref[...]) * (1.0 - lam_init)).astype(o_ref.dtype)


def _diff_attention(qk, v_t, lam_p, subln_g, lam_init, tq):
    bsz, s_len, _ = qk.shape
    width = HEADS_PER_STEP * 2 * HD_A
    kern = functools.partial(_diffattn_kernel, tq=tq, lam_init=lam_init)
    key = lax.broadcasted_iota(jnp.int32, (tq, tq), 0)
    qry = lax.broadcasted_iota(jnp.int32, (tq, tq), 1)
    causal = jnp.where(key <= qry, 0.0, -jnp.inf).astype(F32)
    return pl.pallas_call(
        kern,
        grid=(bsz, N_HEADS_A // HEADS_PER_STEP, s_len // tq),
        in_specs=[
            pl.BlockSpec((None, tq, width), lambda b, h, i: (b, i, h)),
            pl.BlockSpec((None, s_len, width), lambda b, h, i: (b, 0, N_HEADS_A // HEADS_PER_STEP + h)),
            pl.BlockSpec((None, HEADS_PER_STEP * VT_ROWS, s_len), lambda b, h, i: (b, h, 0)),
            pl.BlockSpec((tq, tq), lambda b, h, i: (0, 0)),
            pl.BlockSpec((4, HD_A), lambda b, h, i: (0, 0)),
            pl.BlockSpec((1, 2 * HD_A), lambda b, h, i: (0, 0)),
        ],
        out_specs=pl.BlockSpec((None, tq, width), lambda b, h, i: (b, i, h)),
        out_shape=jax.ShapeDtypeStruct((bsz, s_len, WIDTH_A), BF16),
        scratch_shapes=[pltpu.VMEM((2 * HEADS_PER_STEP, VT_ROWS, tq), F32),
                        pltpu.VMEM((2, 2 * HEADS_PER_STEP, tq, tq), F32)],
        compiler_params=_cparams(("parallel", "parallel", "arbitrary")),
        name="diff_attention",
    )(qk, qk, v_t, causal, lam_p, subln_g)


def _dilated_kernel(q_ref, k_ref, kp_ref, v_ref, vp_ref, o_ref, lse_ref, *, tq, blk):
    n = pl.program_id(2)
    scale = 1.0 / math.sqrt(HD_B)
    nt = (((1,), (1,)), ((), ()))
    rr = lax.broadcasted_iota(jnp.int32, (blk, 2 * blk), 0)
    cc = lax.broadcasted_iota(jnp.int32, (blk, 2 * blk), 1)
    band = jnp.logical_and(cc >= rr, cc <= rr + blk)
    lane = lax.broadcasted_iota(jnp.int32, (1, 2 * blk), 1)
    first_bias = jnp.where(lane >= blk, 0.0, jnp.where(n > 0, 0.0, -jnp.inf).astype(F32))

    def window(ref, pref, c, cols):
        if c == 0:
            return jnp.concatenate([pref[:, cols], ref[0:blk, cols]], axis=0)
        return ref[(c - 1) * blk:(c + 1) * blk, cols]

    def scores(unit):
        h, c = unit
        cols = slice(h * HD_B, (h + 1) * HD_B)
        qh = q_ref[c * blk:(c + 1) * blk, cols]
        s = lax.dot_general(qh, window(k_ref, kp_ref, c, cols), nt, preferred_element_type=F32) * scale
        s = jnp.where(band, s, -jnp.inf)
        return s + first_bias if c == 0 else s

    head_lane = lax.broadcasted_iota(jnp.int32, (1, LANES), 1) // LSE_LANES

    def finish(unit, s, lse_acc):
        h, c = unit
        cols = slice(h * HD_B, (h + 1) * HD_B)
        rows = slice(c * blk, (c + 1) * blk)
        m = jnp.max(s, axis=-1, keepdims=True)
        p = jnp.exp(s - m)
        den = jnp.sum(p, axis=-1, keepdims=True)
        pv = jnp.dot(p.astype(BF16), window(v_ref, vp_ref, c, cols), preferred_element_type=F32)
        o_ref[rows, cols] = (pv / den).astype(o_ref.dtype)
        lse = jnp.broadcast_to(m + jnp.log(den), (blk, LANES))
        lse_acc = lse if h == 0 else jnp.where(head_lane == h, lse, lse_acc)
        if h == N_HEADS_B - 1:
            lse_ref[rows, :] = lse_acc
        return lse_acc

    units = [(h, c) for c in range(tq // blk) for h in range(N_HEADS_B)]
    s = scores(units[0])
    lse_acc = None
    for idx, unit in enumerate(units):
        s_next = scores(units[idx + 1]) if idx + 1 < len(units) else None
        lse_acc = finish(unit, s, lse_acc)
        s = s_next


def _dilated_attention(qkv, bsz, s_len, group, tq):
    window, dil = DIL_PAIRS[group]
    blk = window // dil
    rows = s_len // dil
    tq = min(tq, rows)
    per_res = 3
    view = qkv.reshape(bsz, rows, dil * per_res * WIDTH_B)
    sub = tq // blk
    qcol = lambda r: r * per_res
    kcol = lambda r: r * per_res + 1
    vcol = lambda r: r * per_res + 2
    prev = lambda n: jnp.maximum(n * sub - 1, 0)
    kern = functools.partial(_dilated_kernel, tq=tq, blk=blk)
    o, lse = pl.pallas_call(
        kern,
        grid=(bsz, dil, rows // tq),
        in_specs=[
            pl.BlockSpec((None, tq, WIDTH_B), lambda b, r, n: (b, n, qcol(r))),
            pl.BlockSpec((None, tq, WIDTH_B), lambda b, r, n: (b, n, kcol(r))),
            pl.BlockSpec((None, blk, WIDTH_B), lambda b, r, n: (b, prev(n), kcol(r))),
            pl.BlockSpec((None, tq, WIDTH_B), lambda b, r, n: (b, n, vcol(r))),
            pl.BlockSpec((None, blk, WIDTH_B), lambda b, r, n: (b, prev(n), vcol(r))),
        ],
        out_specs=[pl.BlockSpec((None, tq, WIDTH_B), lambda b, r, n: (b, n, r)),
                   pl.BlockSpec((None, tq, LANES), lambda b, r, n: (b, n, r))],
        out_shape=[jax.ShapeDtypeStruct((bsz, rows, dil * WIDTH_B), BF16),
                   jax.ShapeDtypeStruct((bsz, rows, dil * LANES), F32)],
        compiler_params=_cparams(("parallel", "parallel", "arbitrary")),
        name=f"dilated_attention_g{group}",
    )(view, view, view, view, view)
    return o.reshape(bsz * rows, dil * WIDTH_B), lse.reshape(bsz * rows, dil * LANES)


def _ssm_kernel(u_ref, w8_ref, cd_ref, l8_ref, dskip_ref, wglu_ref, bglu_ref, o_ref,
                u_scr, x_scr, carry_scr, *, t_chunk):
    halo = SUBLANES

    @pl.when(pl.program_id(1) == 0)
    def _():
        carry_scr[...] = jnp.zeros_like(carry_scr)
        u_scr[0:halo, :] = jnp.zeros((halo, WIDTH_C), F32)

    @pl.when(pl.program_id(1) != 0)
    def _():
        u_scr[0:halo, :] = u_scr[t_chunk:t_chunk + halo, :]

    u = u_ref[...]
    u_scr[halo:halo + t_chunk, :] = u
    re = slice(0, N_STATE)
    im = slice(N_STATE, 2 * N_STATE)
    ch = WIDTH_C // 2
    sh = N_STATE // 2
    blk_ch = LANES // 2
    blk_st = blk_ch * SSM_STATE // SSM_GROUP
    low = lax.broadcasted_iota(jnp.int32, (1, LANES), 1) < blk_ch
    c_re, c_im = carry_scr[:, re], carry_scr[:, im]
    a8, b8 = l8_ref[:, re], l8_ref[:, im]
    n_slabs = t_chunk // SSM_SLAB
    slab_rows = lambda s: slice(s * SSM_SLAB, (s + 1) * SSM_SLAB)

    def drive(s):
        r0 = halo + s * SSM_SLAB
        for pair in range(WIDTH_C // LANES):
            lanes = slice(pair * LANES, (pair + 1) * LANES)
            taps = [u_scr[r0 - t:r0 - t + SSM_SLAB, lanes] for t in range(SSM_TAPS)]
            for half in range(2):
                cb = 2 * pair + half
                pieces = []
                for a in range(SSM_TAPS // 2):
                    even, odd = taps[2 * a], taps[2 * a + 1]
                    if half == 0:
                        v = jnp.where(low, even, pltpu.roll(odd, blk_ch, 1))
                    else:
                        v = jnp.where(low, pltpu.roll(even, blk_ch, 1), odd)
                    pieces.append(v.astype(BF16))
                w = jnp.dot(jnp.concatenate(pieces, axis=1), w8_ref[cb], preferred_element_type=F32)
                x_scr[slab_rows(s), cb * blk_st:(cb + 1) * blk_st] = w[:, :blk_st]
                x_scr[slab_rows(s), N_STATE + cb * blk_st:N_STATE + (cb + 1) * blk_st] = w[:, blk_st:]

    drive(0)
    for s in range(n_slabs):
        rows = slab_rows(s)
        if s + 1 < n_slabs:
            drive(s + 1)
        for t in range(SSM_SLAB // SUBLANES):
            r8 = slice(s * SSM_SLAB + t * SUBLANES, s * SSM_SLAB + (t + 1) * SUBLANES)
            x_re = x_scr[r8, re] + (a8 * c_re - b8 * c_im)
            x_im = x_scr[r8, im] + (a8 * c_im + b8 * c_re)
            x_scr[r8, re] = x_re
            x_scr[r8, im] = x_im
            c_re, c_im = x_re, x_im
        halves = []
        for hh in range(2):
            acc = None
            for part in range(2):
                c0 = part * N_STATE + hh * sh
                d = jnp.dot(x_scr[rows, c0:c0 + sh].astype(BF16), cd_ref[c0:c0 + sh, hh * ch:(hh + 1) * ch],
                            preferred_element_type=F32)
                acc = d if acc is None else acc + d
            halves.append(acc)
        y = jnp.concatenate(halves, axis=1) + dskip_ref[...] * u[rows]
        cdf = 0.5 * (1.0 + jnp.tanh(math.sqrt(2.0 / math.pi) * (y + 0.044715 * (y * y * y))))
        z = y * cdf
        gate = jnp.dot(z.astype(BF16), wglu_ref[...], preferred_element_type=F32) + bglu_ref[...]
        o_ref[rows, :] = (z * jax.nn.sigmoid(gate)).astype(o_ref.dtype)
    carry_scr[:, re] = c_re
    carry_scr[:, im] = c_im


def _ssm_operands(a_re, a_im, log_dt, b_re, b_im, c_re, c_im):
    a_re, a_im = a_re.astype(F32), a_im.astype(F32)
    dt = jnp.exp(log_dt.astype(F32))[:, None]
    mag = jnp.exp(a_re * dt)
    lb_re, lb_im = mag * jnp.cos(a_im * dt), mag * jnp.sin(a_im * dt)
    n_re, n_im = lb_re - 1.0, lb_im
    den = a_re * a_re + a_im * a_im
    f_re = (n_re * a_re + n_im * a_im) / den
    f_im = (n_im * a_re - n_re * a_im) / den
    b_re, b_im = b_re.astype(F32), b_im.astype(F32)
    bb_re = f_re[..., None] * b_re - f_im[..., None] * b_im
    bb_im = f_re[..., None] * b_im + f_im[..., None] * b_re
    eye = jnp.eye(N_GROUPS_C, dtype=F32)
    blockdiag_out = lambda t: jnp.einsum('gcp,gh->gphc', t, eye).reshape(N_STATE, WIDTH_C)
    cd = jnp.concatenate([blockdiag_out(c_re.astype(F32)), -blockdiag_out(c_im.astype(F32))],
                         axis=0).astype(BF16)

    def power(k):
        return (mag ** k) * jnp.cos(a_im * dt * k), (mag ** k) * jnp.sin(a_im * dt * k)

    taps = []
    for s in range(SSM_TAPS):
        p_re, p_im = power(float(s))
        taps.append(jnp.stack([p_re[..., None] * bb_re - p_im[..., None] * bb_im,
                               p_re[..., None] * bb_im + p_im[..., None] * bb_re], axis=-1))
    grp = (LANES // 2) // SSM_GROUP
    n_blk = N_GROUPS_C // grp
    v = jnp.stack(taps, axis=0).reshape(SSM_TAPS, n_blk, grp, SSM_STATE, SSM_GROUP, 2)
    w8 = jnp.einsum('sbgpcq,gh->bsgcqhp', v, jnp.eye(grp, dtype=F32))
    w8 = w8.reshape(n_blk, SSM_TAPS * grp * SSM_GROUP, 2 * grp * SSM_STATE).astype(BF16)
    p_re, p_im = power(float(SSM_TAPS))
    l8 = jnp.concatenate([p_re.reshape(1, N_STATE), p_im.reshape(1, N_STATE)], axis=1)
    return w8, cd, jnp.broadcast_to(l8, (SUBLANES, 2 * N_STATE))


def _ssm_branch(c_u, bsz, s_len, w8, cd, l8, d_skip, w_glu_bf, layer, b_glu, t_chunk):
    n_chunks = s_len // t_chunk
    const2 = lambda b, c: (0, 0)
    kern = functools.partial(_ssm_kernel, t_chunk=t_chunk)
    return pl.pallas_call(
        kern,
        grid=(bsz, n_chunks),
        in_specs=[
            pl.BlockSpec((t_chunk, WIDTH_C), lambda b, c: (b * n_chunks + c, 0)),
            pl.BlockSpec(w8.shape, lambda b, c: (0, 0, 0)),
            pl.BlockSpec((2 * N_STATE, WIDTH_C), const2),
            pl.BlockSpec((SUBLANES, 2 * N_STATE), const2),
            pl.BlockSpec((1, WIDTH_C), const2),
            pl.BlockSpec((None, WIDTH_C, WIDTH_C), lambda b, c: (layer, 0, 0)),
            pl.BlockSpec((1, WIDTH_C), const2),
        ],
        out_specs=pl.BlockSpec((t_chunk, WIDTH_C), lambda b, c: (b * n_chunks + c, 0)),
        out_shape=jax.ShapeDtypeStruct((bsz * s_len, WIDTH_C), BF16),
        scratch_shapes=[pltpu.VMEM((t_chunk + SUBLANES, WIDTH_C), F32),
                        pltpu.VMEM((t_chunk, 2 * N_STATE), F32),
                        pltpu.VMEM((SUBLANES, 2 * N_STATE), F32)],
        compiler_params=_cparams(("parallel", "arbitrary")),
        name="s5_scan_glu",
    )(c_u, w8, cd, l8, d_skip, w_glu_bf, b_glu)


def _merge_kernel(x_ref, oa_ref, o0_ref, o1_ref, o2_ref, l0_ref, l1_ref, l2_ref, oc_ref,
                  g0_ref, g1_ref, g2_ref, wa_ref, wb_ref, wc_ref, wo_ref, out_ref, tok_scr, lse_scr, *, tm):
    for gi, (o_ref, l_ref) in enumerate(((o1_ref, l1_ref), (o2_ref, l2_ref))):
        dil = DIL_PAIRS[gi + 1][1]
        for r in range(dil):
            lse_scr[gi, pl.ds(r, tm // dil, stride=dil), :] = l_ref[:, r * LANES:(r + 1) * LANES]
            for li in range(WIDTH_B // LANES):
                cols = slice(r * WIDTH_B + li * LANES, r * WIDTH_B + (li + 1) * LANES)
                tok_scr[gi, li, pl.ds(r, tm // dil, stride=dil), :] = o_ref[:, cols].astype(F32)
    tok = lambda k: jnp.concatenate([tok_scr[k, li] for li in range(WIDTH_B // LANES)], axis=1)
    l0, l1, l2 = l0_ref[...], lse_scr[0], lse_scr[1]
    m = jnp.maximum(jnp.maximum(l0, l1), l2)
    e0, e1, e2 = jnp.exp(l0 - m), jnp.exp(l1 - m), jnp.exp(l2 - m)
    tot = e0 + e1 + e2
    sel_r = lax.broadcasted_iota(jnp.int32, (LANES, WIDTH_B), 0)
    sel_c = lax.broadcasted_iota(jnp.int32, (LANES, WIDTH_B), 1)
    select = jnp.where(sel_r == LSE_LANES * (sel_c // HD_B), 1.0, 0.0).astype(BF16)

    def spread(w):
        hi = w.astype(BF16)
        lo = (w - hi.astype(F32)).astype(BF16)
        return (jnp.dot(hi, select, preferred_element_type=F32)
                + jnp.dot(lo, select, preferred_element_type=F32))

    ob = (spread(e0 / tot) * o0_ref[...].astype(F32) + spread(e1 / tot) * tok(0)
          + spread(e2 / tot) * tok(1))
    ya = jnp.dot(oa_ref[...], wa_ref[...], preferred_element_type=F32)
    yb = jnp.dot(ob.astype(BF16), wb_ref[...], preferred_element_type=F32)
    yc = jnp.dot(oc_ref[...], wc_ref[...], preferred_element_type=F32)
    merged = (g0_ref[...].astype(F32) * ya + g1_ref[...].astype(F32) * yb
              + g2_ref[...].astype(F32) * yc)
    out_ref[...] = x_ref[...] + jnp.dot(merged.astype(BF16), wo_ref[...], preferred_element_type=F32)


def _merge(x2d, oa, ob_parts, lse_parts, oc, gates, wa, wb, wc, wo, layer, tm):
    n = x2d.shape[0]
    row = lambda i: (i, 0)
    half = pl.BlockSpec((tm, WIDTH_C), row)
    full = pl.BlockSpec((tm, D_MODEL), row)
    wspec_half = pl.BlockSpec((None, WIDTH_C, D_MODEL), lambda i: (layer, 0, 0))
    gspec = lambda k: pl.BlockSpec((tm, D_MODEL), lambda i: (i, k))
    dilated = lambda g, width: pl.BlockSpec((tm // DIL_PAIRS[g][1], DIL_PAIRS[g][1] * width), row)
    kern = functools.partial(_merge_kernel, tm=tm)
    return pl.pallas_call(
        kern,
        grid=(n // tm,),
        in_specs=[full, half, half, dilated(1, WIDTH_B), dilated(2, WIDTH_B),
                  pl.BlockSpec((tm, LANES), row), dilated(1, LANES), dilated(2, LANES), half,
                  gspec(0), gspec(1), gspec(2),
                  wspec_half, wspec_half, wspec_half,
                  pl.BlockSpec((None, D_MODEL, D_MODEL), lambda i: (layer, 0, 0))],
        out_specs=full,
        out_shape=jax.ShapeDtypeStruct((n, D_MODEL), F32),
        scratch_shapes=[pltpu.VMEM((2, WIDTH_B // LANES, tm, LANES), F32),
                        pltpu.VMEM((2, tm, LANES), F32)],
        compiler_params=_cparams(("parallel",)),
        name="gated_merge",
    )(x2d, oa, *ob_parts, *lse_parts, oc, gates, gates, gates, wa, wb, wc, wo)


def _ffn_kernel(x_ref, g_ref, wup_ref, cw_ref, cb_ref, wdown_ref, o_ref, a_scr, *, tm, tiles_per_seq):
    halo = SUBLANES
    x = x_ref[...]
    ms = jnp.mean(x * x, axis=-1, keepdims=True)
    h = (x * lax.rsqrt(ms + EPS) * g_ref[...]).astype(BF16)

    @pl.when(pl.program_id(0) % tiles_per_seq == 0)
    def _():
        a_scr[0:halo, :] = jnp.zeros((halo, D_FF), F32)

    @pl.when(pl.program_id(0) % tiles_per_seq != 0)
    def _():
        a_scr[0:halo, :] = a_scr[tm:tm + halo, :]

    n_tiles = D_FF // FFN_TILE
    tile_cols = lambda f: slice(f * FFN_TILE, (f + 1) * FFN_TILE)

    def up(f):
        a_scr[halo:halo + tm, tile_cols(f)] = jnp.dot(h, wup_ref[:, tile_cols(f)], preferred_element_type=F32)
        return jnp.dot(h, wup_ref[:, D_FF + f * FFN_TILE:D_FF + (f + 1) * FFN_TILE],
                       preferred_element_type=F32)

    y = x
    gate = up(0)
    for f in range(n_tiles):
        cols = tile_cols(f)
        next_gate = up(f + 1) if f + 1 < n_tiles else None
        conv = (cb_ref[:, cols] + cw_ref[0:1, cols] * a_scr[halo - 2:halo - 2 + tm, cols]
                + cw_ref[1:2, cols] * a_scr[halo - 1:halo - 1 + tm, cols]
                + cw_ref[2:3, cols] * a_scr[halo:halo + tm, cols])
        act = (conv * jax.nn.sigmoid(conv)) * gate
        y = y + jnp.dot(act.astype(BF16), wdown_ref[cols, :], preferred_element_type=F32)
        gate = next_gate
    o_ref[...] = y


def _conv_ffn(x2d, norm_g, wup_bf, conv_w, conv_b, wdown_bf, layer, s_len, tm):
    n = x2d.shape[0]
    row = lambda i: (i, 0)
    const = lambda i: (0, 0)
    kern = functools.partial(_ffn_kernel, tm=tm, tiles_per_seq=s_len // tm)
    return pl.pallas_call(
        kern,
        grid=(n // tm,),
        in_specs=[
            pl.BlockSpec((tm, D_MODEL), row),
            pl.BlockSpec((1, D_MODEL), const),
            pl.BlockSpec((None, D_MODEL, 2 * D_FF), lambda i: (layer, 0, 0), pipeline_mode=pl.Buffered(1)),
            pl.BlockSpec((CONV_WIDTH, D_FF), const),
            pl.BlockSpec((1, D_FF), const),
            pl.BlockSpec((None, D_FF, D_MODEL), lambda i: (layer, 0, 0), pipeline_mode=pl.Buffered(1)),
        ],
        out_specs=pl.BlockSpec((tm, D_MODEL), row),
        out_shape=jax.ShapeDtypeStruct((n, D_MODEL), F32),
        scratch_shapes=[pltpu.VMEM((tm + SUBLANES, D_FF), F32)],
        compiler_params=_cparams(("arbitrary",)),
        name="conv_ffn",
    )(x2d, norm_g, wup_bf, conv_w, conv_b, wdown_bf)


def _rope_tables(positions, head_dim):
    rot = head_dim // ROPE_FRACTION
    half = rot // 2
    inv = ROPE_THETA ** (-jnp.arange(0, rot, 2, dtype=F32) / rot)
    ang = positions.reshape(-1).astype(F32)[:, None] * inv
    cos, sin = jnp.cos(ang), jnp.sin(ang)
    n = ang.shape[0]
    rest = head_dim - rot
    c = jnp.concatenate([cos, cos, jnp.ones((n, rest), F32)], axis=1)
    s_up = jnp.concatenate([jnp.zeros((n, half), F32), sin, jnp.zeros((n, rest), F32)], axis=1)
    s_dn = jnp.concatenate([-sin, jnp.zeros((n, half + rest), F32)], axis=1)
    reps = LANES // head_dim
    return tuple(jnp.tile(t, (1, reps)) for t in (c, s_up, s_dn))


def kernel(x, positions, attn_norm_g, w_in, b_gate, qn_a, kn_a, lam_q1, lam_k1, lam_q2, lam_k2, subln_g, w_br_a, qn_b, kn_b, w_br_b, ssm_a_re, ssm_a_im, ssm_log_dt, ssm_b_re, ssm_b_im, ssm_c_re, ssm_c_im, ssm_d, w_glu, b_glu, w_br_c, w_out, ffn_norm_g, w_up, conv_w, conv_b, w_down):
    bsz, s_len, d_model = x.shape
    depth = w_in.shape[0]
    assert d_model == D_MODEL and w_in.shape[2] == IN_COLS
    assert s_len % DIL_PAIRS[-1][0] == 0, "sequence must be a multiple of the largest dilated window"
    n = bsz * s_len

    tm_proj = 256
    tq_a = min(512, s_len)
    tq_b = 512
    t_ssm = min(512, s_len)
    tm_merge = min(256, n)
    tm_ffn = min(256, s_len)

    rope = _rope_tables(positions, HD_A) + _rope_tables(positions, HD_B)
    w_in_bf = w_in.astype(BF16)
    wa_bf, wb_bf, wc_bf, wo_bf = (w.astype(BF16) for w in (w_br_a, w_br_b, w_br_c, w_out))
    wglu_bf, wup_bf, wdown_bf = w_glu.astype(BF16), w_up.astype(BF16), w_down.astype(BF16)

    x2d = x.reshape(n, D_MODEL)
    for l in range(depth):
        lam_init = 0.8 - 0.6 * math.exp(-0.3 * l)
        ones = lambda w: jnp.ones((w,), F32)
        col_gain = jnp.concatenate([
            jnp.tile(qn_a[l].astype(F32) * (math.log2(math.e) / math.sqrt(HD_A)), A_Q_COLS // HD_A),
            jnp.tile(kn_a[l].astype(F32), A_Q_COLS // HD_A), ones(WIDTH_A),
            jnp.tile(qn_b[l].astype(F32), B_COLS // HD_B), jnp.tile(kn_b[l].astype(F32), B_COLS // HD_B),
            ones(B_COLS + WIDTH_C + N_BRANCHES * D_MODEL)]).reshape(1, IN_COLS)
        col_bias = jnp.concatenate([jnp.zeros((IN_COLS - N_BRANCHES * D_MODEL,), F32),
                                    b_gate[l].astype(F32)]).reshape(1, IN_COLS)
        qk_a, v_t, *qkv_b, c_u, gates = _in_projection(
            x2d, attn_norm_g[l].reshape(1, D_MODEL).astype(F32), w_in_bf, l, col_gain, col_bias, rope,
            bsz, s_len, tm_proj)

        lam_p = jnp.stack([lam_q1[l], lam_k1[l], lam_q2[l], lam_k2[l]]).astype(F32)
        oa = _diff_attention(qk_a.reshape(bsz, s_len, 2 * A_Q_COLS), v_t, lam_p,
                             subln_g[l].reshape(1, 2 * HD_A).astype(F32), lam_init, tq_a)
        oa = oa.reshape(n, WIDTH_A)

        ob_parts, lse_parts = zip(*[_dilated_attention(qkv_b[g], bsz, s_len, g, tq_b) for g in range(N_DIL)])

        w8, cd, l8 = _ssm_operands(ssm_a_re[l], ssm_a_im[l], ssm_log_dt[l], ssm_b_re[l], ssm_b_im[l],
                                   ssm_c_re[l], ssm_c_im[l])
        oc = _ssm_branch(c_u, bsz, s_len, w8, cd, l8, ssm_d[l].reshape(1, WIDTH_C).astype(F32),
                         wglu_bf, l, b_glu[l].reshape(1, WIDTH_C).astype(F32), t_ssm)

        x2d = _merge(x2d, oa, ob_parts, lse_parts, oc, gates, wa_bf, wb_bf, wc_bf, wo_bf, l, tm_merge)
        x2d = _conv_ffn(x2d, ffn_norm_g[l].reshape(1, D_MODEL).astype(F32), wup_bf,
                        conv_w[l].astype(F32), conv_b[l].reshape(1, D_FF).astype(F32), wdown_bf, l,
                        s_len, tm_ffn)
    return x2d.reshape(bsz, s_len, D_MODEL)
```

```python
import functools
import math

import jax
import jax.numpy as jnp
from jax import lax
from jax.experimental import pallas as pl
from jax.experimental.pallas import tpu as pltpu

F32 = jnp.float32
BF16 = jnp.bfloat16

LANES = 128
SUBLANES = 8
ROT_SHIFT = LANES // 2

D_MODEL = 1024
N_HEADS_A = 4
HD_A = 64
N_DIL = 3
DIL_PAIRS = ((128, 1), (512, 4), (2048, 16))
N_HEADS_B = 4
HD_B = 128
WIDTH_B = N_HEADS_B * HD_B
SSM_GROUP = 16
SSM_STATE = 64
WIDTH_C = 512
N_GROUPS_C = WIDTH_C // SSM_GROUP
N_STATE = N_GROUPS_C * SSM_STATE
N_BRANCHES = 3
D_FF = 2816
CONV_WIDTH = 3
ROPE_THETA = 500000.0
ROPE_FRACTION = 4
EPS = 1e-6

COL_TILE = 512
A_Q_COLS = 2 * N_HEADS_A * HD_A
WIDTH_A = N_HEADS_A * 2 * HD_A
B_COLS = N_DIL * N_HEADS_B * HD_B
IN_COLS = 2 * A_Q_COLS + WIDTH_A + 3 * B_COLS + WIDTH_C + N_BRANCHES * D_MODEL
N_COL_TILES = IN_COLS // COL_TILE
BF_TILES = (2 * A_Q_COLS + WIDTH_A + 3 * B_COLS) // COL_TILE
BF_COLS = BF_TILES * COL_TILE
TILE_AQ, TILE_AK, TILE_AV = 0, 1, 2
TILE_BQ, TILE_BK, TILE_BV = 3, 6, 9
TILE_CU = 12

HEADS_PER_STEP = 2
VT_PAD = 16
VT_ROWS = 2 * HD_A + VT_PAD
LSE_LANES = LANES // N_HEADS_B
SSM_SLAB = 128
SSM_TAPS = SUBLANES
FFN_TILE = 256

VMEM_LIMIT = 56 * 1024 * 1024


def _cparams(sem):
    return pltpu.CompilerParams(dimension_semantics=sem, vmem_limit_bytes=VMEM_LIMIT)


def _first_head_lanes():
    lane = lax.broadcasted_iota(jnp.int32, (1, LANES), 1)
    return (lane // (ROT_SHIFT // 2)) % 2 == 0


def _norm_rope_tile(acc, gain, seg, c_ref, s_ref):
    cos = c_ref[...]
    sin = s_ref[...]
    first = _first_head_lanes()
    outs = []
    for gi in range(acc.shape[1] // LANES):
        y = acc[:, gi * LANES:(gi + 1) * LANES]
        ysq = y * y
        tot = jnp.sum(ysq, axis=-1, keepdims=True)
        if seg == LANES:
            ssum = tot
        else:
            one = jnp.sum(jnp.where(first, ysq, 0.0), axis=-1, keepdims=True)
            ssum = jnp.where(first, one, tot - one)
        yn = y * lax.rsqrt(ssum * (1.0 / seg) + EPS) * gain[:, gi * LANES:(gi + 1) * LANES]
        outs.append(yn * cos + pltpu.roll(yn, ROT_SHIFT, 1) * sin)
    return jnp.concatenate(outs, axis=1)


def _inproj_kernel(x_ref, g_ref, w_ref, gain_ref, bias_ref,
                   ca_ref, sa_ref, cb_ref, sb_ref,
                   qk_ref, vt_ref, b0_ref, b1_ref, b2_ref, cu_ref, gate_ref, dil_scr, *, tm):
    x = x_ref[...]
    ms = jnp.mean(x * x, axis=-1, keepdims=True)
    h = (x * lax.rsqrt(ms + EPS) * g_ref[...]).astype(BF16)
    dil_refs = (b0_ref, b1_ref, b2_ref)
    heavy = [TILE_AQ, TILE_AK] + list(range(TILE_BQ, TILE_BV))
    light = [TILE_AV] + list(range(TILE_BV, TILE_CU + 1))
    order = heavy + [j for j in range(N_COL_TILES) if j not in heavy + light] + light
    project = lambda j: jnp.dot(h, w_ref[:, j * COL_TILE:(j + 1) * COL_TILE], preferred_element_type=F32)
    nxt = project(order[0])
    for idx, j in enumerate(order):
        cols = slice(j * COL_TILE, (j + 1) * COL_TILE)
        acc = nxt
        if idx + 1 < len(order):
            nxt = project(order[idx + 1])
        if j in (TILE_AQ, TILE_AK):
            qk_ref[:, cols] = _norm_rope_tile(acc, gain_ref[:, cols], HD_A, ca_ref, sa_ref).astype(BF16)
        elif j == TILE_AV:
            acc_t = acc.T.astype(BF16)
            for hd in range(N_HEADS_A):
                vt_ref[hd * VT_ROWS:hd * VT_ROWS + 2 * HD_A, :] = acc_t[hd * 2 * HD_A:(hd + 1) * 2 * HD_A, :]
                vt_ref[hd * VT_ROWS + 2 * HD_A:(hd + 1) * VT_ROWS, :] = jnp.ones((VT_PAD, tm), BF16)
        elif j < TILE_CU:
            part, group = divmod(j - TILE_BQ, N_DIL)
            if part < 2:
                acc = _norm_rope_tile(acc, gain_ref[:, cols], HD_B, cb_ref, sb_ref)
            dil = DIL_PAIRS[group][1]
            out_ref = dil_refs[group]
            if dil == 1:
                out_ref[:, part * WIDTH_B:(part + 1) * WIDTH_B] = acc.astype(BF16)
            else:
                slot = (group - 1) * 3 + part
                for gi in range(WIDTH_B // LANES):
                    dil_scr[slot, gi] = acc[:, gi * LANES:(gi + 1) * LANES]
                for r in range(dil):
                    for gi in range(WIDTH_B // LANES):
                        c0 = (r * 3 + part) * WIDTH_B + gi * LANES
                        out_ref[:, c0:c0 + LANES] = (
                            dil_scr[slot, gi, pl.ds(r, tm // dil, stride=dil), :].astype(BF16))
        elif j == TILE_CU:
            cu_ref[...] = acc
        else:
            g0 = (j - TILE_CU - 1) * COL_TILE
            gate_ref[:, g0:g0 + COL_TILE] = jax.nn.sigmoid(acc + bias_ref[:, cols]).astype(gate_ref.dtype)


def _in_projection(x2d, norm_g, w_bf, layer, col_gain, col_bias, rope, bsz, s_len, tm):
    n = x2d.shape[0]
    tiles_per_seq = s_len // tm
    row = lambda i: (i, 0)
    const = lambda i: (0, 0)
    d1, d2 = DIL_PAIRS[1][1], DIL_PAIRS[2][1]
    kern = functools.partial(_inproj_kernel, tm=tm)
    return pl.pallas_call(
        kern,
        grid=(n // tm,),
        in_specs=[
            pl.BlockSpec((tm, D_MODEL), row),
            pl.BlockSpec((1, D_MODEL), const),
            pl.BlockSpec((None, D_MODEL, IN_COLS), lambda i: (layer, 0, 0), pipeline_mode=pl.Buffered(1)),
            pl.BlockSpec((1, IN_COLS), const),
            pl.BlockSpec((1, IN_COLS), const),
        ] + [pl.BlockSpec((tm, LANES), row)] * len(rope),
        out_specs=[
            pl.BlockSpec((tm, 2 * A_Q_COLS), row),
            pl.BlockSpec((None, N_HEADS_A * VT_ROWS, tm), lambda i: (i // tiles_per_seq, 0, i % tiles_per_seq)),
            pl.BlockSpec((tm, 3 * WIDTH_B), row),
            pl.BlockSpec((tm // d1, d1 * 3 * WIDTH_B), row),
            pl.BlockSpec((tm // d2, d2 * 3 * WIDTH_B), row),
            pl.BlockSpec((tm, WIDTH_C), row),
            pl.BlockSpec((tm, N_BRANCHES * D_MODEL), row),
        ],
        out_shape=[jax.ShapeDtypeStruct((n, 2 * A_Q_COLS), BF16),
                   jax.ShapeDtypeStruct((bsz, N_HEADS_A * VT_ROWS, s_len), BF16),
                   jax.ShapeDtypeStruct((n, 3 * WIDTH_B), BF16),
                   jax.ShapeDtypeStruct((n // d1, d1 * 3 * WIDTH_B), BF16),
                   jax.ShapeDtypeStruct((n // d2, d2 * 3 * WIDTH_B), BF16),
                   jax.ShapeDtypeStruct((n, WIDTH_C), F32),
                   jax.ShapeDtypeStruct((n, N_BRANCHES * D_MODEL), BF16)],
        scratch_shapes=[pltpu.VMEM((6, WIDTH_B // LANES, tm, LANES), F32)],
        compiler_params=_cparams(("parallel",)),
        name="in_projection",
    )(x2d, norm_g, w_bf, col_gain, col_bias, *rope)


def _diffattn_kernel(q_ref, k_ref, vt_ref, mask_ref, lam_ref, subg_ref, o_ref, acc_scr, s_scr, *, tq, lam_init):
    i = pl.program_id(2)
    q = q_ref[...]
    first = _first_head_lanes()
    nt = (((1,), (1,)), ((), ()))
    qm = []
    for hh in range(HEADS_PER_STEP):
        qh = q[:, hh * 2 * HD_A:(hh + 1) * 2 * HD_A]
        zero = jnp.zeros_like(qh)
        qm += [jnp.where(first, qh, zero), jnp.where(first, zero, qh)]
    n_maps = len(qm)
    acc_scr[...] = jnp.zeros_like(acc_scr)

    def scores(j, slot):
        r0 = pl.multiple_of(j * tq, tq)
        for mi in range(n_maps):
            hh = mi // 2
            kblk = k_ref[pl.ds(r0, tq), hh * 2 * HD_A:(hh + 1) * 2 * HD_A]
            s_scr[slot, mi] = lax.dot_general(kblk, qm[mi], nt, preferred_element_type=F32)

    def consume(j, slot, ms, diagonal):
        r0 = pl.multiple_of(j * tq, tq)
        new = []
        for mi in range(n_maps):
            hh = mi // 2
            vtblk = vt_ref[hh * VT_ROWS:(hh + 1) * VT_ROWS, pl.ds(r0, tq)]
            st = s_scr[slot, mi]
            if diagonal:
                st = st + mask_ref[...]
            m_new = jnp.maximum(ms[mi], jnp.max(st, axis=0, keepdims=True))
            p = jnp.exp2(st - m_new)
            alpha = jnp.exp2(ms[mi] - m_new)
            acc_scr[mi] = alpha * acc_scr[mi] + jnp.dot(vtblk, p.astype(BF16), preferred_element_type=F32)
            new.append(m_new)
        return tuple(new)

    def stage(j, src, ms):
        scores(j + 1, 1 - src)
        return consume(j, src, ms, False)

    init = tuple(jnp.full((1, tq), -jnp.inf, F32) for _ in range(n_maps))
    scores(0, 0)
    ms = lax.fori_loop(0, i // 2, lambda t, c: stage(2 * t + 1, 1, stage(2 * t, 0, c)), init)
    odd = lax.rem(i, 2) == 1

    @pl.when(odd)
    def _():
        consume(i, 1, stage(i - 1, 0, ms), True)

    @pl.when(jnp.logical_not(odd))
    def _():
        consume(i, 0, ms, True)

    lam_p = lam_ref[...]
    lam = (jnp.exp(jnp.sum(lam_p[0:1] * lam_p[1:2], axis=-1, keepdims=True))
           - jnp.exp(jnp.sum(lam_p[2:3] * lam_p[3:4], axis=-1, keepdims=True)) + lam_init)
    vals = slice(0, 2 * HD_A)
    den = slice(2 * HD_A, 2 * HD_A + 1)
    for hh in range(HEADS_PER_STEP):
        a1, a2 = acc_scr[2 * hh], acc_scr[2 * hh + 1]
        o_t = a1[vals] / a1[den] - lam * (a2[vals] / a2[den])
        msq = jnp.mean(o_t * o_t, axis=0, keepdims=True)
        o = (o_t * lax.rsqrt(msq + EPS)).T
        o_ref[:, hh * 2 * HD_A:(hh + 1) * 2 * HD_A] = ((o * subg_ref[...]) * (1.0 - lam_init)).astype(o_ref.dtype)


def _diff_attention(qk, v_t, lam_p, subln_g, lam_init, tq):
    bsz, s_len, _ = qk.shape
    width = HEADS_PER_STEP * 2 * HD_A
    kern = functools.partial(_diffattn_kernel, tq=tq, lam_init=lam_init)
    key = lax.broadcasted_iota(jnp.int32, (tq, tq), 0)
    qry = lax.broadcasted_iota(jnp.int32, (tq, tq), 1)
    causal = jnp.where(key <= qry, 0.0, -jnp.inf).astype(F32)
    return pl.pallas_call(
        kern,
        grid=(bsz, N_HEADS_A // HEADS_PER_STEP, s_len // tq),
        in_specs=[
            pl.BlockSpec((None, tq, width), lambda b, h, i: (b, i, h)),
            pl.BlockSpec((None, s_len, width), lambda b, h, i: (b, 0, N_HEADS_A // HEADS_PER_STEP + h)),
            pl.BlockSpec((None, HEADS_PER_STEP * VT_ROWS, s_len), lambda b, h, i: (b, h, 0)),
            pl.BlockSpec((tq, tq), lambda b, h, i: (0, 0)),
            pl.BlockSpec((4, HD_A), lambda b, h, i: (0, 0)),
            pl.BlockSpec((1, 2 * HD_A), lambda b, h, i: (0, 0)),
        ],
        out_specs=pl.BlockSpec((None, tq, width), lambda b, h, i: (b, i, h)),
        out_shape=jax.ShapeDtypeStruct((bsz, s_len, WIDTH_A), BF16),
        scratch_shapes=[pltpu.VMEM((2 * HEADS_PER_STEP, VT_ROWS, tq), F32),
                        pltpu.VMEM((2, 2 * HEADS_PER_STEP, tq, tq), F32)],
        compiler_params=_cparams(("parallel", "parallel", "arbitrary")),
        name="diff_attention",
    )(qk, qk, v_t, causal, lam_p, subln_g)


def _dilated_kernel(q_ref, k_ref, kp_ref, v_ref, vp_ref, o_ref, lse_ref, *, tq, blk):
    n = pl.program_id(2)
    scale = 1.0 / math.sqrt(HD_B)
    nt = (((1,), (1,)), ((), ()))
    rr = lax.broadcasted_iota(jnp.int32, (blk, 2 * blk), 0)
    cc = lax.broadcasted_iota(jnp.int32, (blk, 2 * blk), 1)
    band = jnp.logical_and(cc >= rr, cc <= rr + blk)
    lane = lax.broadcasted_iota(jnp.int32, (1, 2 * blk), 1)
    first_bias = jnp.where(lane >= blk, 0.0, jnp.where(n > 0, 0.0, -jnp.inf).astype(F32))

    def window(ref, pref, c, cols):
        if c == 0:
            return jnp.concatenate([pref[:, cols], ref[0:blk, cols]], axis=0)
        return ref[(c - 1) * blk:(c + 1) * blk, cols]

    def scores(unit):
        h, c = unit
        cols = slice(h * HD_B, (h + 1) * HD_B)
        qh = q_ref[c * blk:(c + 1) * blk, cols]
        s = lax.dot_general(qh, window(k_ref, kp_ref, c, cols), nt, preferred_element_type=F32) * scale
        s = jnp.where(band, s, -jnp.inf)
        return s + first_bias if c == 0 else s

    head_lane = lax.broadcasted_iota(jnp.int32, (1, LANES), 1) // LSE_LANES

    def finish(unit, s, lse_acc):
        h, c = unit
        cols = slice(h * HD_B, (h + 1) * HD_B)
        rows = slice(c * blk, (c + 1) * blk)
        m = jnp.max(s, axis=-1, keepdims=True)
        p = jnp.exp(s - m)
        den = jnp.sum(p, axis=-1, keepdims=True)
        pv = jnp.dot(p.astype(BF16), window(v_ref, vp_ref, c, cols), preferred_element_type=F32)
        o_ref[rows, cols] = (pv / den).astype(o_ref.dtype)
        lse = jnp.broadcast_to(m + jnp.log(den), (blk, LANES))
        lse_acc = lse if h == 0 else jnp.where(head_lane == h, lse, lse_acc)
        if h == N_HEADS_B - 1:
            lse_ref[rows, :] = lse_acc
        return lse_acc

    units = [(h, c) for c in range(tq // blk) for h in range(N_HEADS_B)]
    s = scores(units[0])
    lse_acc = None
    for idx, unit in enumerate(units):
        s_next = scores(units[idx + 1]) if idx + 1 < len(units) else None
        lse_acc = finish(unit, s, lse_acc)
        s = s_next


def _dilated_attention(qkv, bsz, s_len, group, tq):
    window, dil = DIL_PAIRS[group]
    blk = window // dil
    rows = s_len // dil
    tq = min(tq, rows)
    per_res = 3
    view = qkv.reshape(bsz, rows, dil * per_res * WIDTH_B)
    sub = tq // blk
    qcol = lambda r: r * per_res
    kcol = lambda r: r * per_res + 1
    vcol = lambda r: r * per_res + 2
    prev = lambda n: jnp.maximum(n * sub - 1, 0)
    kern = functools.partial(_dilated_kernel, tq=tq, blk=blk)
    o, lse = pl.pallas_call(
        kern,
        grid=(bsz, dil, rows // tq),
        in_specs=[
            pl.BlockSpec((None, tq, WIDTH_B), lambda b, r, n: (b, n, qcol(r))),
            pl.BlockSpec((None, tq, WIDTH_B), lambda b, r, n: (b, n, kcol(r))),
            pl.BlockSpec((None, blk, WIDTH_B), lambda b, r, n: (b, prev(n), kcol(r))),
            pl.BlockSpec((None, tq, WIDTH_B), lambda b, r, n: (b, n, vcol(r))),
            pl.BlockSpec((None, blk, WIDTH_B), lambda b, r, n: (b, prev(n), vcol(r))),
        ],
        out_specs=[pl.BlockSpec((None, tq, WIDTH_B), lambda b, r, n: (b, n, r)),
                   pl.BlockSpec((None, tq, LANES), lambda b, r, n: (b, n, r))],
        out_shape=[jax.ShapeDtypeStruct((bsz, rows, dil * WIDTH_B), BF16),
                   jax.ShapeDtypeStruct((bsz, rows, dil * LANES), F32)],
        compiler_params=_cparams(("parallel", "parallel", "arbitrary")),
        name=f"dilated_attention_g{group}",
    )(view, view, view, view, view)
    return o.reshape(bsz * rows, dil * WIDTH_B), lse.reshape(bsz * rows, dil * LANES)


def _ssm_kernel(u_ref, w8_ref, cd_ref, l8_ref, dskip_ref, wglu_ref, bglu_ref, o_ref,
                u_scr, x_scr, carry_scr, *, t_chunk):
    halo = SUBLANES

    @pl.when(pl.program_id(1) == 0)
    def _():
        carry_scr[...] = jnp.zeros_like(carry_scr)
        u_scr[0:halo, :] = jnp.zeros((halo, WIDTH_C), F32)

    @pl.when(pl.program_id(1) != 0)
    def _():
        u_scr[0:halo, :] = u_scr[t_chunk:t_chunk + halo, :]

    u = u_ref[...]
    u_scr[halo:halo + t_chunk, :] = u
    re = slice(0, N_STATE)
    im = slice(N_STATE, 2 * N_STATE)
    ch = WIDTH_C // 2
    sh = N_STATE // 2
    blk_ch = LANES // 2
    blk_st = blk_ch * SSM_STATE // SSM_GROUP
    low = lax.broadcasted_iota(jnp.int32, (1, LANES), 1) < blk_ch
    c_re, c_im = carry_scr[:, re], carry_scr[:, im]
    a8, b8 = l8_ref[:, re], l8_ref[:, im]
    n_slabs = t_chunk // SSM_SLAB
    slab_rows = lambda s: slice(s * SSM_SLAB, (s + 1) * SSM_SLAB)

    def drive(s):
        r0 = halo + s * SSM_SLAB
        for pair in range(WIDTH_C // LANES):
            lanes = slice(pair * LANES, (pair + 1) * LANES)
            taps = [u_scr[r0 - t:r0 - t + SSM_SLAB, lanes] for t in range(SSM_TAPS)]
            for half in range(2):
                cb = 2 * pair + half
                pieces = []
                for a in range(SSM_TAPS // 2):
                    even, odd = taps[2 * a], taps[2 * a + 1]
                    if half == 0:
                        v = jnp.where(low, even, pltpu.roll(odd, blk_ch, 1))
                    else:
                        v = jnp.where(low, pltpu.roll(even, blk_ch, 1), odd)
                    pieces.append(v.astype(BF16))
                w = jnp.dot(jnp.concatenate(pieces, axis=1), w8_ref[cb], preferred_element_type=F32)
                x_scr[slab_rows(s), cb * blk_st:(cb + 1) * blk_st] = w[:, :blk_st]
                x_scr[slab_rows(s), N_STATE + cb * blk_st:N_STATE + (cb + 1) * blk_st] = w[:, blk_st:]

    drive(0)
    for s in range(n_slabs):
        rows = slab_rows(s)
        if s + 1 < n_slabs:
            drive(s + 1)
        for t in range(SSM_SLAB // SUBLANES):
            r8 = slice(s * SSM_SLAB + t * SUBLANES, s * SSM_SLAB + (t + 1) * SUBLANES)
            x_re = x_scr[r8, re] + (a8 * c_re - b8 * c_im)
            x_im = x_scr[r8, im] + (a8 * c_im + b8 * c_re)
            x_scr[r8, re] = x_re
            x_scr[r8, im] = x_im
            c_re, c_im = x_re, x_im
        halves = []
        for hh in range(2):
            acc = None
            for part in range(2):
                c0 = part * N_STATE + hh * sh
                d = jnp.dot(x_scr[rows, c0:c0 + sh].astype(BF16), cd_ref[c0:c0 + sh, hh * ch:(hh + 1) * ch],
                            preferred_element_type=F32)
                acc = d if acc is None else acc + d
            halves.append(acc)
        y = jnp.concatenate(halves, axis=1) + dskip_ref[...] * u[rows]
        cdf = 0.5 * (1.0 + jnp.tanh(math.sqrt(2.0 / math.pi) * (y + 0.044715 * (y * y * y))))
        z = y * cdf
        gate = jnp.dot(z.astype(BF16), wglu_ref[...], preferred_element_type=F32) + bglu_ref[...]
        o_ref[rows, :] = (z * jax.nn.sigmoid(gate)).astype(o_ref.dtype)
    carry_scr[:, re] = c_re
    carry_scr[:, im] = c_im


def _ssm_operands(a_re, a_im, log_dt, b_re, b_im, c_re, c_im):
    a_re, a_im = a_re.astype(F32), a_im.astype(F32)
    dt = jnp.exp(log_dt.astype(F32))[:, None]
    mag = jnp.exp(a_re * dt)
    lb_re, lb_im = mag * jnp.cos(a_im * dt), mag * jnp.sin(a_im * dt)
    n_re, n_im = lb_re - 1.0, lb_im
    den = a_re * a_re + a_im * a_im
    f_re = (n_re * a_re + n_im * a_im) / den
    f_im = (n_im * a_re - n_re * a_im) / den
    b_re, b_im = b_re.astype(F32), b_im.astype(F32)
    bb_re = f_re[..., None] * b_re - f_im[..., None] * b_im
    bb_im = f_re[..., None] * b_im + f_im[..., None] * b_re
    eye = jnp.eye(N_GROUPS_C, dtype=F32)
    blockdiag_out = lambda t: jnp.einsum('gcp,gh->gphc', t, eye).reshape(N_STATE, WIDTH_C)
    cd = jnp.concatenate([blockdiag_out(c_re.astype(F32)), -blockdiag_out(c_im.astype(F32))],
                         axis=0).astype(BF16)

    def power(k):
        return (mag ** k) * jnp.cos(a_im * dt * k), (mag ** k) * jnp.sin(a_im * dt * k)

    taps = []
    for s in range(SSM_TAPS):
        p_re, p_im = power(float(s))
        taps.append(jnp.stack([p_re[..., None] * bb_re - p_im[..., None] * bb_im,
                               p_re[..., None] * bb_im + p_im[..., None] * bb_re], axis=-1))
    grp = (LANES // 2) // SSM_GROUP
    n_blk = N_GROUPS_C // grp
    v = jnp.stack(taps, axis=0).reshape(SSM_TAPS, n_blk, grp, SSM_STATE, SSM_GROUP, 2)
    w8 = jnp.einsum('sbgpcq,gh->bsgcqhp', v, jnp.eye(grp, dtype=F32))
    w8 = w8.reshape(n_blk, SSM_TAPS * grp * SSM_GROUP, 2 * grp * SSM_STATE).astype(BF16)
    p_re, p_im = power(float(SSM_TAPS))
    l8 = jnp.concatenate([p_re.reshape(1, N_STATE), p_im.reshape(1, N_STATE)], axis=1)
    return w8, cd, jnp.broadcast_to(l8, (SUBLANES, 2 * N_STATE))


def _ssm_branch(c_u, bsz, s_len, w8, cd, l8, d_skip, w_glu_bf, layer, b_glu, t_chunk):
    n_chunks = s_len // t_chunk
    const2 = lambda b, c: (0, 0)
    kern = functools.partial(_ssm_kernel, t_chunk=t_chunk)
    return pl.pallas_call(
        kern,
        grid=(bsz, n_chunks),
        in_specs=[
            pl.BlockSpec((t_chunk, WIDTH_C), lambda b, c: (b * n_chunks + c, 0)),
            pl.BlockSpec((None,) + w8.shape[1:], lambda b, c: (layer, 0, 0, 0)),
            pl.BlockSpec((None, 2 * N_STATE, WIDTH_C), lambda b, c: (layer, 0, 0)),
            pl.BlockSpec((None, SUBLANES, 2 * N_STATE), lambda b, c: (layer, 0, 0)),
            pl.BlockSpec((1, WIDTH_C), const2),
            pl.BlockSpec((None, WIDTH_C, WIDTH_C), lambda b, c: (layer, 0, 0)),
            pl.BlockSpec((1, WIDTH_C), const2),
        ],
        out_specs=pl.BlockSpec((t_chunk, WIDTH_C), lambda b, c: (b * n_chunks + c, 0)),
        out_shape=jax.ShapeDtypeStruct((bsz * s_len, WIDTH_C), BF16),
        scratch_shapes=[pltpu.VMEM((t_chunk + SUBLANES, WIDTH_C), F32),
                        pltpu.VMEM((t_chunk, 2 * N_STATE), F32),
                        pltpu.VMEM((SUBLANES, 2 * N_STATE), F32)],
        compiler_params=_cparams(("parallel", "arbitrary")),
        name="s5_scan_glu",
    )(c_u, w8, cd, l8, d_skip, w_glu_bf, b_glu)


def _merge_body(x_ref, oa_ref, o0_ref, o1_ref, o2_ref, l0_ref, l1_ref, l2_ref, oc_ref,
                g0_ref, g1_ref, g2_ref, wa_ref, wb_ref, wc_ref, wo_ref, tok_scr, lse_scr, tm):
    for gi, (o_ref, l_ref) in enumerate(((o1_ref, l1_ref), (o2_ref, l2_ref))):
        dil = DIL_PAIRS[gi + 1][1]
        for r in range(dil):
            lse_scr[gi, pl.ds(r, tm // dil, stride=dil), :] = l_ref[:, r * LANES:(r + 1) * LANES]
            for li in range(WIDTH_B // LANES):
                cols = slice(r * WIDTH_B + li * LANES, r * WIDTH_B + (li + 1) * LANES)
                tok_scr[gi, li, pl.ds(r, tm // dil, stride=dil), :] = o_ref[:, cols].astype(F32)
    tok = lambda k: jnp.concatenate([tok_scr[k, li] for li in range(WIDTH_B // LANES)], axis=1)
    l0, l1, l2 = l0_ref[...], lse_scr[0], lse_scr[1]
    m = jnp.maximum(jnp.maximum(l0, l1), l2)
    e0, e1, e2 = jnp.exp(l0 - m), jnp.exp(l1 - m), jnp.exp(l2 - m)
    tot = e0 + e1 + e2
    sel_r = lax.broadcasted_iota(jnp.int32, (LANES, WIDTH_B), 0)
    sel_c = lax.broadcasted_iota(jnp.int32, (LANES, WIDTH_B), 1)
    select = jnp.where(sel_r == LSE_LANES * (sel_c // HD_B), 1.0, 0.0).astype(BF16)

    def spread(w):
        hi = w.astype(BF16)
        lo = (w - hi.astype(F32)).astype(BF16)
        return (jnp.dot(hi, select, preferred_element_type=F32)
                + jnp.dot(lo, select, preferred_element_type=F32))

    ob = (spread(e0 / tot) * o0_ref[...].astype(F32) + spread(e1 / tot) * tok(0)
          + spread(e2 / tot) * tok(1))
    ya = jnp.dot(oa_ref[...], wa_ref[...], preferred_element_type=F32)
    yb = jnp.dot(ob.astype(BF16), wb_ref[...], preferred_element_type=F32)
    yc = jnp.dot(oc_ref[...], wc_ref[...], preferred_element_type=F32)
    merged = (g0_ref[...].astype(F32) * ya + g1_ref[...].astype(F32) * yb
              + g2_ref[...].astype(F32) * yc)
    return x_ref[...] + jnp.dot(merged.astype(BF16), wo_ref[...], preferred_element_type=F32)


def _ffn_conv_halo(a_scr, tm, tiles_per_seq):
    halo = SUBLANES

    @pl.when(pl.program_id(0) % tiles_per_seq == 0)
    def _():
        a_scr[0:halo, :] = jnp.zeros((halo, D_FF), F32)

    @pl.when(pl.program_id(0) % tiles_per_seq != 0)
    def _():
        a_scr[0:halo, :] = a_scr[tm:tm + halo, :]


def _ffn_body(x, g_ref, wup_ref, cw_ref, cb_ref, wdown_ref, o_ref, a_scr, tm):
    halo = SUBLANES
    ms = jnp.mean(x * x, axis=-1, keepdims=True)
    h = (x * lax.rsqrt(ms + EPS) * g_ref[...]).astype(BF16)

    n_tiles = D_FF // FFN_TILE
    tile_cols = lambda f: slice(f * FFN_TILE, (f + 1) * FFN_TILE)

    def up(f):
        a_scr[halo:halo + tm, tile_cols(f)] = jnp.dot(h, wup_ref[:, tile_cols(f)], preferred_element_type=F32)
        return jnp.dot(h, wup_ref[:, D_FF + f * FFN_TILE:D_FF + (f + 1) * FFN_TILE],
                       preferred_element_type=F32)

    y = x
    gate = up(0)
    for f in range(n_tiles):
        cols = tile_cols(f)
        next_gate = up(f + 1) if f + 1 < n_tiles else None
        conv = (cb_ref[:, cols] + cw_ref[0:1, cols] * a_scr[halo - 2:halo - 2 + tm, cols]
                + cw_ref[1:2, cols] * a_scr[halo - 1:halo - 1 + tm, cols]
                + cw_ref[2:3, cols] * a_scr[halo:halo + tm, cols])
        act = (conv * jax.nn.sigmoid(conv)) * gate
        y = y + jnp.dot(act.astype(BF16), wdown_ref[cols, :], preferred_element_type=F32)
        gate = next_gate
    o_ref[...] = y


N_MERGE_INPUTS = 16


def _merge_ffn_kernel(*refs, tm, tiles_per_seq):
    merge_in = refs[:N_MERGE_INPUTS]
    g_ref, wup_ref, cw_ref, cb_ref, wdown_ref, o_ref, tok_scr, lse_scr, a_scr = refs[N_MERGE_INPUTS:]
    _ffn_conv_halo(a_scr, tm, tiles_per_seq)
    x_mid = _merge_body(*merge_in, tok_scr, lse_scr, tm)
    _ffn_body(x_mid, g_ref, wup_ref, cw_ref, cb_ref, wdown_ref, o_ref, a_scr, tm)


def _merge_ffn(x2d, oa, ob_parts, lse_parts, oc, gates, wa, wb, wc, wo,
               norm_g, wup_bf, conv_w, conv_b, wdown_bf, layer, s_len, tm):
    n = x2d.shape[0]
    row = lambda i: (i, 0)
    const = lambda i: (0, 0)
    half = pl.BlockSpec((tm, WIDTH_C), row)
    full = pl.BlockSpec((tm, D_MODEL), row)
    single = pl.Buffered(1)
    wspec_half = pl.BlockSpec((None, WIDTH_C, D_MODEL), lambda i: (layer, 0, 0), pipeline_mode=single)
    gspec = lambda k: pl.BlockSpec((tm, D_MODEL), lambda i: (i, k))
    dilated = lambda g, width: pl.BlockSpec((tm // DIL_PAIRS[g][1], DIL_PAIRS[g][1] * width), row)
    merge_specs = [full, half, half, dilated(1, WIDTH_B), dilated(2, WIDTH_B),
                   pl.BlockSpec((tm, LANES), row), dilated(1, LANES), dilated(2, LANES), half,
                   gspec(0), gspec(1), gspec(2),
                   wspec_half, wspec_half, wspec_half,
                   pl.BlockSpec((None, D_MODEL, D_MODEL), lambda i: (layer, 0, 0), pipeline_mode=single)]
    assert len(merge_specs) == N_MERGE_INPUTS
    ffn_specs = [
        pl.BlockSpec((1, D_MODEL), const),
        pl.BlockSpec((None, D_MODEL, 2 * D_FF), lambda i: (layer, 0, 0), pipeline_mode=single),
        pl.BlockSpec((CONV_WIDTH, D_FF), const),
        pl.BlockSpec((1, D_FF), const),
        pl.BlockSpec((None, D_FF, D_MODEL), lambda i: (layer, 0, 0), pipeline_mode=single),
    ]
    kern = functools.partial(_merge_ffn_kernel, tm=tm, tiles_per_seq=s_len // tm)
    return pl.pallas_call(
        kern,
        grid=(n // tm,),
        in_specs=merge_specs + ffn_specs,
        out_specs=full,
        out_shape=jax.ShapeDtypeStruct((n, D_MODEL), F32),
        scratch_shapes=[pltpu.VMEM((2, WIDTH_B // LANES, tm, LANES), F32),
                        pltpu.VMEM((2, tm, LANES), F32),
                        pltpu.VMEM((tm + SUBLANES, D_FF), F32)],
        compiler_params=_cparams(("arbitrary",)),
        name="merge_conv_ffn",
    )(x2d, oa, *ob_parts, *lse_parts, oc, gates, gates, gates, wa, wb, wc, wo,
      norm_g, wup_bf, conv_w, conv_b, wdown_bf)


def _lane_order(head_dim):
    half = head_dim // ROPE_FRACTION // 2
    heads = LANES // head_dim
    quarter = ROT_SHIFT // heads
    first, second = [], []
    for hd in range(heads):
        base = hd * head_dim
        first += list(range(base, base + half)) + list(range(base + 2 * half, base + half + quarter))
        second += list(range(base + half, base + 2 * half)) + list(range(base + half + quarter, base + head_dim))
    return first + second


def _permute_groups(w_cols, head_dim):
    order = _lane_order(head_dim)
    perm = jnp.zeros((LANES, LANES), F32).at[jnp.asarray(order), jnp.arange(LANES)].set(1.0).astype(w_cols.dtype)
    grouped = w_cols.reshape(w_cols.shape[:-1] + (w_cols.shape[-1] // LANES, LANES))
    out = jnp.einsum('...gd,de->...ge', grouped, perm, preferred_element_type=F32)
    return out.astype(w_cols.dtype).reshape(w_cols.shape)


def _rope_tables(positions, head_dim):
    rot = head_dim // ROPE_FRACTION
    half = rot // 2
    heads = LANES // head_dim
    quarter = ROT_SHIFT // heads
    inv = ROPE_THETA ** (-jnp.arange(0, rot, 2, dtype=F32) / rot)
    ang = positions.reshape(-1).astype(F32)[:, None] * inv
    cos, sin = jnp.cos(ang), jnp.sin(ang)
    n = ang.shape[0]
    pad = quarter - half
    c_q = jnp.concatenate([cos, jnp.ones((n, pad), F32)], axis=1)
    s_q = jnp.concatenate([sin, jnp.zeros((n, pad), F32)], axis=1)
    c = jnp.tile(c_q, (1, 2 * heads))
    s = jnp.concatenate([jnp.tile(-s_q, (1, heads)), jnp.tile(s_q, (1, heads))], axis=1)
    return c, s


def kernel(x, positions, attn_norm_g, w_in, b_gate, qn_a, kn_a, lam_q1, lam_k1, lam_q2, lam_k2, subln_g, w_br_a, qn_b, kn_b, w_br_b, ssm_a_re, ssm_a_im, ssm_log_dt, ssm_b_re, ssm_b_im, ssm_c_re, ssm_c_im, ssm_d, w_glu, b_glu, w_br_c, w_out, ffn_norm_g, w_up, conv_w, conv_b, w_down):
    bsz, s_len, d_model = x.shape
    depth = w_in.shape[0]
    assert d_model == D_MODEL and w_in.shape[2] == IN_COLS
    assert s_len % DIL_PAIRS[-1][0] == 0, "sequence must be a multiple of the largest dilated window"
    n = bsz * s_len

    tm_proj = 256
    tq_a = min(512, s_len)
    tq_b = 1024
    t_ssm = min(512, s_len)
    tm_ffn = min(256, s_len)

    rope = _rope_tables(positions, HD_A) + _rope_tables(positions, HD_B)
    w_in_bf = w_in.astype(BF16)
    b_qk = slice(TILE_BQ * COL_TILE, TILE_BV * COL_TILE)
    w_in_bf = jnp.concatenate([
        _permute_groups(w_in_bf[:, :, :2 * A_Q_COLS], HD_A), w_in_bf[:, :, 2 * A_Q_COLS:b_qk.start],
        _permute_groups(w_in_bf[:, :, b_qk], HD_B), w_in_bf[:, :, b_qk.stop:]], axis=-1)
    order_a = jnp.asarray(_lane_order(HD_A))
    order_b = jnp.asarray(_lane_order(HD_B))
    wa_bf, wb_bf, wc_bf, wo_bf = (w.astype(BF16) for w in (w_br_a, w_br_b, w_br_c, w_out))
    wglu_bf, wup_bf, wdown_bf = w_glu.astype(BF16), w_up.astype(BF16), w_down.astype(BF16)

    w8, cd, l8 = jax.vmap(_ssm_operands)(ssm_a_re, ssm_a_im, ssm_log_dt, ssm_b_re, ssm_b_im, ssm_c_re, ssm_c_im)

    x2d = x.reshape(n, D_MODEL)
    for l in range(depth):
        lam_init = 0.8 - 0.6 * math.exp(-0.3 * l)
        ones = lambda w: jnp.ones((w,), F32)
        group_a = lambda g: jnp.tile(jnp.tile(g.astype(F32), LANES // HD_A)[order_a], A_Q_COLS // LANES)
        group_b = lambda g: jnp.tile(g.astype(F32)[order_b], B_COLS // LANES)
        col_gain = jnp.concatenate([
            group_a(qn_a[l]) * (math.log2(math.e) / math.sqrt(HD_A)), group_a(kn_a[l]), ones(WIDTH_A),
            group_b(qn_b[l]), group_b(kn_b[l]),
            ones(B_COLS + WIDTH_C + N_BRANCHES * D_MODEL)]).reshape(1, IN_COLS)
        col_bias = jnp.concatenate([jnp.zeros((IN_COLS - N_BRANCHES * D_MODEL,), F32),
                                    b_gate[l].astype(F32)]).reshape(1, IN_COLS)
        qk_a, v_t, *qkv_b, c_u, gates = _in_projection(
            x2d, attn_norm_g[l].reshape(1, D_MODEL).astype(F32), w_in_bf, l, col_gain, col_bias, rope,
            bsz, s_len, tm_proj)

        lam_p = jnp.stack([lam_q1[l], lam_k1[l], lam_q2[l], lam_k2[l]]).astype(F32)
        oa = _diff_attention(qk_a.reshape(bsz, s_len, 2 * A_Q_COLS), v_t, lam_p,
                             subln_g[l].reshape(1, 2 * HD_A).astype(F32), lam_init, tq_a)
        oa = oa.reshape(n, WIDTH_A)

        ob_parts, lse_parts = zip(*[_dilated_attention(qkv_b[g], bsz, s_len, g, tq_b) for g in range(N_DIL)])

        oc = _ssm_branch(c_u, bsz, s_len, w8, cd, l8, ssm_d[l].reshape(1, WIDTH_C).astype(F32),
                         wglu_bf, l, b_glu[l].reshape(1, WIDTH_C).astype(F32), t_ssm)

        x2d = _merge_ffn(x2d, oa, ob_parts, lse_parts, oc, gates, wa_bf, wb_bf, wc_bf, wo_bf,
                         ffn_norm_g[l].reshape(1, D_MODEL).astype(F32), wup_bf,
                         conv_w[l].astype(F32), conv_b[l].reshape(1, D_FF).astype(F32), wdown_bf, l,
                         s_len, tm_ffn)
    return x2d.reshape(bsz, s_len, D_MODEL)
```

```python
import functools
import math

import jax
import jax.numpy as jnp
from jax import lax
from jax.experimental import pallas as pl
from jax.experimental.pallas import tpu as pltpu

F32 = jnp.float32
BF16 = jnp.bfloat16

LANES = 128
SUBLANES = 8
ROT_SHIFT = LANES // 2

D_MODEL = 1024
N_HEADS_A = 4
HD_A = 64
N_DIL = 3
DIL_PAIRS = ((128, 1), (512, 4), (2048, 16))
N_HEADS_B = 4
HD_B = 128
WIDTH_B = N_HEADS_B * HD_B
SSM_GROUP = 16
SSM_STATE = 64
WIDTH_C = 512
N_GROUPS_C = WIDTH_C // SSM_GROUP
N_STATE = N_GROUPS_C * SSM_STATE
N_BRANCHES = 3
D_FF = 2816
CONV_WIDTH = 3
ROPE_THETA = 500000.0
ROPE_FRACTION = 4
EPS = 1e-6

COL_TILE = 512
QK_CHUNK = 2 * COL_TILE
A_Q_COLS = 2 * N_HEADS_A * HD_A
WIDTH_A = N_HEADS_A * 2 * HD_A
B_COLS = N_DIL * N_HEADS_B * HD_B
IN_COLS = 2 * A_Q_COLS + WIDTH_A + 3 * B_COLS + WIDTH_C + N_BRANCHES * D_MODEL
N_COL_TILES = IN_COLS // COL_TILE
BF_TILES = (2 * A_Q_COLS + WIDTH_A + 3 * B_COLS) // COL_TILE
BF_COLS = BF_TILES * COL_TILE
TILE_AQ, TILE_AK, TILE_AV = 0, 1, 2
TILE_BQ, TILE_BK, TILE_BV = 3, 6, 9
TILE_CU = 12

HEADS_PER_STEP = 2
ATTN_UNROLL = 2
VT_PAD = 16
VT_ROWS = 2 * HD_A + VT_PAD
LSE_LANES = LANES // N_HEADS_B
SSM_SLAB = 256
SSM_TAPS = SUBLANES
FFN_TILE = 256

VMEM_LIMIT = 56 * 1024 * 1024


def _cparams(sem):
    return pltpu.CompilerParams(dimension_semantics=sem, vmem_limit_bytes=VMEM_LIMIT)


def _first_head_lanes():
    lane = lax.broadcasted_iota(jnp.int32, (1, LANES), 1)
    return (lane // (ROT_SHIFT // 2)) % 2 == 0


def _norm_rope_tile(acc, gain, seg, c_ref, s_ref):
    cos = c_ref[...]
    sin = s_ref[...]
    first = _first_head_lanes()
    outs = []
    for gi in range(acc.shape[1] // LANES):
        y = acc[:, gi * LANES:(gi + 1) * LANES]
        ysq = y * y
        tot = jnp.sum(ysq, axis=-1, keepdims=True)
        if seg == LANES:
            ssum = tot
        else:
            one = jnp.sum(jnp.where(first, ysq, 0.0), axis=-1, keepdims=True)
            ssum = jnp.where(first, one, tot - one)
        yn = y * lax.rsqrt(ssum * (1.0 / seg) + EPS) * gain[:, gi * LANES:(gi + 1) * LANES]
        outs.append(yn * cos + pltpu.roll(yn, ROT_SHIFT, 1) * sin)
    return jnp.concatenate(outs, axis=1)


def _inproj_kernel(x_ref, g_ref, w_ref, wqa_ref, wqb0_ref, wqb1_ref, wqb2_ref, gain_ref, bias_ref,
                   ca_ref, sa_ref, cb_ref, sb_ref,
                   qk_ref, vt_ref, b0_ref, b1_ref, b2_ref, cu_ref, gate_ref, dil_scr, *, tm):
    x = x_ref[...]
    ms = jnp.mean(x * x, axis=-1, keepdims=True)
    h = (x * lax.rsqrt(ms + EPS) * g_ref[...]).astype(BF16)
    dil_refs = (b0_ref, b1_ref, b2_ref)
    heavy = [TILE_AQ, TILE_AK] + list(range(TILE_BQ, TILE_BV))
    light = [TILE_AV] + list(range(TILE_BV, TILE_CU + 1))
    order = heavy + [j for j in range(N_COL_TILES) if j not in heavy + light] + light
    wqb_refs = (wqb0_ref, wqb1_ref, wqb2_ref)

    def project(j):
        if j in (TILE_AQ, TILE_AK):
            ref, c0 = wqa_ref, j * COL_TILE
        elif TILE_BQ <= j < TILE_BV:
            off = (j - TILE_BQ) * COL_TILE
            ref, c0 = wqb_refs[off // QK_CHUNK], off % QK_CHUNK
        else:
            ref, c0 = w_ref, j * COL_TILE
        return jnp.dot(h, ref[:, c0:c0 + COL_TILE], preferred_element_type=F32)

    nxt = project(order[0])
    for idx, j in enumerate(order):
        cols = slice(j * COL_TILE, (j + 1) * COL_TILE)
        acc = nxt
        if idx + 1 < len(order):
            nxt = project(order[idx + 1])
        if j in (TILE_AQ, TILE_AK):
            qk_ref[:, cols] = _norm_rope_tile(acc, gain_ref[:, cols], HD_A, ca_ref, sa_ref).astype(BF16)
        elif j == TILE_AV:
            acc_t = acc.T.astype(BF16)
            for hd in range(N_HEADS_A):
                vt_ref[hd * VT_ROWS:hd * VT_ROWS + 2 * HD_A, :] = acc_t[hd * 2 * HD_A:(hd + 1) * 2 * HD_A, :]
                vt_ref[hd * VT_ROWS + 2 * HD_A:(hd + 1) * VT_ROWS, :] = jnp.ones((VT_PAD, tm), BF16)
        elif j < TILE_CU:
            part, group = divmod(j - TILE_BQ, N_DIL)
            if part < 2:
                acc = _norm_rope_tile(acc, gain_ref[:, cols], HD_B, cb_ref, sb_ref)
            dil = DIL_PAIRS[group][1]
            out_ref = dil_refs[group]
            if dil == 1:
                out_ref[:, part * WIDTH_B:(part + 1) * WIDTH_B] = acc.astype(BF16)
            else:
                slot = (group - 1) * 3 + part
                for gi in range(WIDTH_B // LANES):
                    dil_scr[slot, gi] = acc[:, gi * LANES:(gi + 1) * LANES]
                for r in range(dil):
                    for gi in range(WIDTH_B // LANES):
                        c0 = (r * 3 + part) * WIDTH_B + gi * LANES
                        out_ref[:, c0:c0 + LANES] = (
                            dil_scr[slot, gi, pl.ds(r, tm // dil, stride=dil), :].astype(BF16))
        elif j == TILE_CU:
            cu_ref[...] = acc
        else:
            g0 = (j - TILE_CU - 1) * COL_TILE
            gate_ref[:, g0:g0 + COL_TILE] = jax.nn.sigmoid(acc + bias_ref[:, cols]).astype(gate_ref.dtype)


def _in_projection(x2d, norm_g, w_bf, w_qk, layer, col_gain, col_bias, rope, bsz, s_len, tm):
    n = x2d.shape[0]
    tiles_per_seq = s_len // tm
    row = lambda i: (i, 0)
    const = lambda i: (0, 0)
    d1, d2 = DIL_PAIRS[1][1], DIL_PAIRS[2][1]
    kern = functools.partial(_inproj_kernel, tm=tm)
    resident = lambda cols: pl.BlockSpec((None, D_MODEL, cols), lambda i: (layer, 0, 0),
                                         pipeline_mode=pl.Buffered(1))
    return pl.pallas_call(
        kern,
        grid=(n // tm,),
        in_specs=[
            pl.BlockSpec((tm, D_MODEL), row),
            pl.BlockSpec((1, D_MODEL), const),
            resident(IN_COLS), resident(QK_CHUNK), resident(QK_CHUNK), resident(QK_CHUNK), resident(QK_CHUNK),
            pl.BlockSpec((1, IN_COLS), const),
            pl.BlockSpec((1, IN_COLS), const),
        ] + [pl.BlockSpec((tm, LANES), row)] * len(rope),
        out_specs=[
            pl.BlockSpec((tm, 2 * A_Q_COLS), row),
            pl.BlockSpec((None, N_HEADS_A * VT_ROWS, tm), lambda i: (i // tiles_per_seq, 0, i % tiles_per_seq)),
            pl.BlockSpec((tm, 3 * WIDTH_B), row),
            pl.BlockSpec((tm // d1, d1 * 3 * WIDTH_B), row),
            pl.BlockSpec((tm // d2, d2 * 3 * WIDTH_B), row),
            pl.BlockSpec((tm, WIDTH_C), row),
            pl.BlockSpec((tm, N_BRANCHES * D_MODEL), row),
        ],
        out_shape=[jax.ShapeDtypeStruct((n, 2 * A_Q_COLS), BF16),
                   jax.ShapeDtypeStruct((bsz, N_HEADS_A * VT_ROWS, s_len), BF16),
                   jax.ShapeDtypeStruct((n, 3 * WIDTH_B), BF16),
                   jax.ShapeDtypeStruct((n // d1, d1 * 3 * WIDTH_B), BF16),
                   jax.ShapeDtypeStruct((n // d2, d2 * 3 * WIDTH_B), BF16),
                   jax.ShapeDtypeStruct((n, WIDTH_C), F32),
                   jax.ShapeDtypeStruct((n, N_BRANCHES * D_MODEL), BF16)],
        scratch_shapes=[pltpu.VMEM((6, WIDTH_B // LANES, tm, LANES), F32)],
        compiler_params=_cparams(("parallel",)),
        name="in_projection",
    )(x2d, norm_g, w_bf, *w_qk, col_gain, col_bias, *rope)


def _diffattn_kernel(q_ref, k_ref, vt_ref, mask_ref, lam_ref, subg_ref, o_ref, acc_scr, s_scr, *, tq, lam_init):
    i = pl.program_id(2)
    q = q_ref[...]
    first = _first_head_lanes()
    nt = (((1,), (1,)), ((), ()))
    qm = []
    for hh in range(HEADS_PER_STEP):
        qh = q[:, hh * 2 * HD_A:(hh + 1) * 2 * HD_A]
        zero = jnp.zeros_like(qh)
        qm += [jnp.where(first, qh, zero), jnp.where(first, zero, qh)]
    n_maps = len(qm)
    acc_scr[...] = jnp.zeros_like(acc_scr)

    def score(j, slot, mi):
        r0 = pl.multiple_of(j * tq, tq)
        hh = mi // 2
        kblk = k_ref[pl.ds(r0, tq), hh * 2 * HD_A:(hh + 1) * 2 * HD_A]
        s_scr[slot, mi] = lax.dot_general(kblk, qm[mi], nt, preferred_element_type=F32)

    def consume_one(j, slot, mi, m_old, diagonal):
        r0 = pl.multiple_of(j * tq, tq)
        hh = mi // 2
        vtblk = vt_ref[hh * VT_ROWS:(hh + 1) * VT_ROWS, pl.ds(r0, tq)]
        st = s_scr[slot, mi]
        if diagonal:
            st = st + mask_ref[...]
        m_new = jnp.maximum(m_old, jnp.max(st, axis=0, keepdims=True))
        p = jnp.exp2(st - m_new)
        alpha = jnp.exp2(m_old - m_new)
        acc_scr[mi] = alpha * acc_scr[mi] + jnp.dot(vtblk, p.astype(BF16), preferred_element_type=F32)
        return m_new

    def scores(j, slot):
        for mi in range(n_maps):
            score(j, slot, mi)

    def consume(j, slot, ms, diagonal):
        return tuple(consume_one(j, slot, mi, ms[mi], diagonal) for mi in range(n_maps))

    def stage(j, src, ms):
        new = []
        for mi in range(n_maps):
            score(j + 1, 1 - src, mi)
            new.append(consume_one(j, src, mi, ms[mi], False))
        return tuple(new)

    init = tuple(jnp.full((1, tq), -jnp.inf, F32) for _ in range(n_maps))
    def stages(j0, count, ms):
        for u in range(count):
            ms = stage(j0 + u, u % 2, ms)
        return ms

    scores(0, 0)
    ms = lax.fori_loop(0, i // ATTN_UNROLL, lambda t, c: stages(ATTN_UNROLL * t, ATTN_UNROLL, c), init)
    rem = lax.rem(i, ATTN_UNROLL)
    for r in range(ATTN_UNROLL):
        @pl.when(rem == r)
        def _(r=r):
            consume(i, r % 2, stages(i - r, r, ms), True)

    lam_p = lam_ref[...]
    lam = (jnp.exp(jnp.sum(lam_p[0:1] * lam_p[1:2], axis=-1, keepdims=True))
           - jnp.exp(jnp.sum(lam_p[2:3] * lam_p[3:4], axis=-1, keepdims=True)) + lam_init)
    vals = slice(0, 2 * HD_A)
    den = slice(2 * HD_A, 2 * HD_A + 1)
    for hh in range(HEADS_PER_STEP):
        a1, a2 = acc_scr[2 * hh], acc_scr[2 * hh + 1]
        o_t = a1[vals] / a1[den] - lam * (a2[vals] / a2[den])
        msq = jnp.mean(o_t * o_t, axis=0, keepdims=True)
        o = (o_t * lax.rsqrt(msq + EPS)).T
        o_ref[:, hh * 2 * HD_A:(hh + 1) * 2 * HD_A] = ((o * subg_ref[...]) * (1.0 - lam_init)).astype(o_ref.dtype)


def _diff_attention(qk, v_t, lam_p, subln_g, lam_init, tq):
    bsz, s_len, _ = qk.shape
    width = HEADS_PER_STEP * 2 * HD_A
    kern = functools.partial(_diffattn_kernel, tq=tq, lam_init=lam_init)
    key = lax.broadcasted_iota(jnp.int32, (tq, tq), 0)
    qry = lax.broadcasted_iota(jnp.int32, (tq, tq), 1)
    causal = jnp.where(key <= qry, 0.0, -jnp.inf).astype(F32)
    return pl.pallas_call(
        kern,
        grid=(bsz, N_HEADS_A // HEADS_PER_STEP, s_len // tq),
        in_specs=[
            pl.BlockSpec((None, tq, width), lambda b, h, i: (b, i, h)),
            pl.BlockSpec((None, s_len, width), lambda b, h, i: (b, 0, N_HEADS_A // HEADS_PER_STEP + h)),
            pl.BlockSpec((None, HEADS_PER_STEP * VT_ROWS, s_len), lambda b, h, i: (b, h, 0)),
            pl.BlockSpec((tq, tq), lambda b, h, i: (0, 0)),
            pl.BlockSpec((4, HD_A), lambda b, h, i: (0, 0)),
            pl.BlockSpec((1, 2 * HD_A), lambda b, h, i: (0, 0)),
        ],
        out_specs=pl.BlockSpec((None, tq, width), lambda b, h, i: (b, i, h)),
        out_shape=jax.ShapeDtypeStruct((bsz, s_len, WIDTH_A), BF16),
        scratch_shapes=[pltpu.VMEM((2 * HEADS_PER_STEP, VT_ROWS, tq), F32),
                        pltpu.VMEM((2, 2 * HEADS_PER_STEP, tq, tq), F32)],
        compiler_params=_cparams(("parallel", "parallel", "arbitrary")),
        name="diff_attention",
    )(qk, qk, v_t, causal, lam_p, subln_g)


def _dilated_kernel(q_ref, k_ref, kp_ref, v_ref, vp_ref, o_ref, lse_ref, *, tq, blk):
    n = pl.program_id(2)
    scale = 1.0 / math.sqrt(HD_B)
    nt = (((1,), (1,)), ((), ()))
    rr = lax.broadcasted_iota(jnp.int32, (blk, 2 * blk), 0)
    cc = lax.broadcasted_iota(jnp.int32, (blk, 2 * blk), 1)
    band = jnp.logical_and(cc >= rr, cc <= rr + blk)
    lane = lax.broadcasted_iota(jnp.int32, (1, 2 * blk), 1)
    first_bias = jnp.where(lane >= blk, 0.0, jnp.where(n > 0, 0.0, -jnp.inf).astype(F32))

    def window(ref, pref, c, cols):
        if c == 0:
            return jnp.concatenate([pref[:, cols], ref[0:blk, cols]], axis=0)
        return ref[(c - 1) * blk:(c + 1) * blk, cols]

    def scores(unit):
        h, c = unit
        cols = slice(h * HD_B, (h + 1) * HD_B)
        qh = q_ref[c * blk:(c + 1) * blk, cols]
        s = lax.dot_general(qh, window(k_ref, kp_ref, c, cols), nt, preferred_element_type=F32) * scale
        s = jnp.where(band, s, -jnp.inf)
        return s + first_bias if c == 0 else s

    head_lane = lax.broadcasted_iota(jnp.int32, (1, LANES), 1) // LSE_LANES

    def finish(unit, s, lse_acc):
        h, c = unit
        cols = slice(h * HD_B, (h + 1) * HD_B)
        rows = slice(c * blk, (c + 1) * blk)
        m = jnp.max(s, axis=-1, keepdims=True)
        p = jnp.exp(s - m)
        den = jnp.sum(p, axis=-1, keepdims=True)
        pv = jnp.dot(p.astype(BF16), window(v_ref, vp_ref, c, cols), preferred_element_type=F32)
        o_ref[rows, cols] = (pv / den).astype(o_ref.dtype)
        lse = jnp.broadcast_to(m + jnp.log(den), (blk, LANES))
        lse_acc = lse if h == 0 else jnp.where(head_lane == h, lse, lse_acc)
        if h == N_HEADS_B - 1:
            lse_ref[rows, :] = lse_acc
        return lse_acc

    units = [(h, c) for c in range(tq // blk) for h in range(N_HEADS_B)]
    s = scores(units[0])
    lse_acc = None
    for idx, unit in enumerate(units):
        s_next = scores(units[idx + 1]) if idx + 1 < len(units) else None
        lse_acc = finish(unit, s, lse_acc)
        s = s_next


def _dilated_attention(qkv, bsz, s_len, group, tq):
    window, dil = DIL_PAIRS[group]
    blk = window // dil
    rows = s_len // dil
    tq = min(tq, rows)
    per_res = 3
    view = qkv.reshape(bsz, rows, dil * per_res * WIDTH_B)
    sub = tq // blk
    qcol = lambda r: r * per_res
    kcol = lambda r: r * per_res + 1
    vcol = lambda r: r * per_res + 2
    prev = lambda n: jnp.maximum(n * sub - 1, 0)
    kern = functools.partial(_dilated_kernel, tq=tq, blk=blk)
    o, lse = pl.pallas_call(
        kern,
        grid=(bsz, dil, rows // tq),
        in_specs=[
            pl.BlockSpec((None, tq, WIDTH_B), lambda b, r, n: (b, n, qcol(r))),
            pl.BlockSpec((None, tq, WIDTH_B), lambda b, r, n: (b, n, kcol(r))),
            pl.BlockSpec((None, blk, WIDTH_B), lambda b, r, n: (b, prev(n), kcol(r))),
            pl.BlockSpec((None, tq, WIDTH_B), lambda b, r, n: (b, n, vcol(r))),
            pl.BlockSpec((None, blk, WIDTH_B), lambda b, r, n: (b, prev(n), vcol(r))),
        ],
        out_specs=[pl.BlockSpec((None, tq, WIDTH_B), lambda b, r, n: (b, n, r)),
                   pl.BlockSpec((None, tq, LANES), lambda b, r, n: (b, n, r))],
        out_shape=[jax.ShapeDtypeStruct((bsz, rows, dil * WIDTH_B), BF16),
                   jax.ShapeDtypeStruct((bsz, rows, dil * LANES), F32)],
        compiler_params=_cparams(("parallel", "parallel", "arbitrary")),
        name=f"dilated_attention_g{group}",
    )(view, view, view, view, view)
    return o.reshape(bsz * rows, dil * WIDTH_B), lse.reshape(bsz * rows, dil * LANES)


def _ssm_kernel(u_ref, w8_ref, cd_ref, l8_ref, dskip_ref, wglu_ref, bglu_ref, o_ref,
                u_scr, x_scr, carry_scr, *, t_chunk):
    halo = SUBLANES

    @pl.when(pl.program_id(1) == 0)
    def _():
        carry_scr[...] = jnp.zeros_like(carry_scr)
        u_scr[0:halo, :] = jnp.zeros((halo, WIDTH_C), F32)

    @pl.when(pl.program_id(1) != 0)
    def _():
        u_scr[0:halo, :] = u_scr[t_chunk:t_chunk + halo, :]

    u = u_ref[...]
    u_scr[halo:halo + t_chunk, :] = u
    re = slice(0, N_STATE)
    im = slice(N_STATE, 2 * N_STATE)
    ch = WIDTH_C // 2
    sh = N_STATE // 2
    blk_ch = LANES // 2
    blk_st = blk_ch * SSM_STATE // SSM_GROUP
    low = lax.broadcasted_iota(jnp.int32, (1, LANES), 1) < blk_ch
    c_re, c_im = carry_scr[:, re], carry_scr[:, im]
    a8, b8 = l8_ref[:, re], l8_ref[:, im]
    n_slabs = t_chunk // SSM_SLAB
    slab_rows = lambda s: slice(s * SSM_SLAB, (s + 1) * SSM_SLAB)

    def drive(s):
        r0 = halo + s * SSM_SLAB
        for pair in range(WIDTH_C // LANES):
            lanes = slice(pair * LANES, (pair + 1) * LANES)
            taps = [u_scr[r0 - t:r0 - t + SSM_SLAB, lanes] for t in range(SSM_TAPS)]
            for half in range(2):
                cb = 2 * pair + half
                pieces = []
                for a in range(SSM_TAPS // 2):
                    even, odd = taps[2 * a], taps[2 * a + 1]
                    if half == 0:
                        v = jnp.where(low, even, pltpu.roll(odd, blk_ch, 1))
                    else:
                        v = jnp.where(low, pltpu.roll(even, blk_ch, 1), odd)
                    pieces.append(v.astype(BF16))
                w = jnp.dot(jnp.concatenate(pieces, axis=1), w8_ref[cb], preferred_element_type=F32)
                x_scr[slab_rows(s), cb * blk_st:(cb + 1) * blk_st] = w[:, :blk_st]
                x_scr[slab_rows(s), N_STATE + cb * blk_st:N_STATE + (cb + 1) * blk_st] = w[:, blk_st:]

    drive(0)
    if n_slabs > 1:
        drive(1)
    for s in range(n_slabs):
        rows = slab_rows(s)
        if s + 2 < n_slabs:
            drive(s + 2)
        for t in range(SSM_SLAB // SUBLANES):
            r8 = slice(s * SSM_SLAB + t * SUBLANES, s * SSM_SLAB + (t + 1) * SUBLANES)
            x_re = x_scr[r8, re] + (a8 * c_re - b8 * c_im)
            x_im = x_scr[r8, im] + (a8 * c_im + b8 * c_re)
            x_scr[r8, re] = x_re
            x_scr[r8, im] = x_im
            c_re, c_im = x_re, x_im
        halves = []
        for hh in range(2):
            acc = None
            for part in range(2):
                c0 = part * N_STATE + hh * sh
                d = jnp.dot(x_scr[rows, c0:c0 + sh].astype(BF16), cd_ref[c0:c0 + sh, hh * ch:(hh + 1) * ch],
                            preferred_element_type=F32)
                acc = d if acc is None else acc + d
            halves.append(acc)
        y = jnp.concatenate(halves, axis=1) + dskip_ref[...] * u[rows]
        cdf = 0.5 * (1.0 + jnp.tanh(math.sqrt(2.0 / math.pi) * (y + 0.044715 * (y * y * y))))
        z = y * cdf
        gate = jnp.dot(z.astype(BF16), wglu_ref[...], preferred_element_type=F32) + bglu_ref[...]
        o_ref[rows, :] = (z * jax.nn.sigmoid(gate)).astype(o_ref.dtype)
    carry_scr[:, re] = c_re
    carry_scr[:, im] = c_im


def _ssm_operands(a_re, a_im, log_dt, b_re, b_im, c_re, c_im):
    a_re, a_im = a_re.astype(F32), a_im.astype(F32)
    dt = jnp.exp(log_dt.astype(F32))[:, None]
    mag = jnp.exp(a_re * dt)
    lb_re, lb_im = mag * jnp.cos(a_im * dt), mag * jnp.sin(a_im * dt)
    n_re, n_im = lb_re - 1.0, lb_im
    den = a_re * a_re + a_im * a_im
    f_re = (n_re * a_re + n_im * a_im) / den
    f_im = (n_im * a_re - n_re * a_im) / den
    b_re, b_im = b_re.astype(F32), b_im.astype(F32)
    bb_re = f_re[..., None] * b_re - f_im[..., None] * b_im
    bb_im = f_re[..., None] * b_im + f_im[..., None] * b_re
    eye = jnp.eye(N_GROUPS_C, dtype=F32)
    blockdiag_out = lambda t: jnp.einsum('gcp,gh->gphc', t, eye).reshape(N_STATE, WIDTH_C)
    cd = jnp.concatenate([blockdiag_out(c_re.astype(F32)), -blockdiag_out(c_im.astype(F32))],
                         axis=0).astype(BF16)

    def power(k):
        return (mag ** k) * jnp.cos(a_im * dt * k), (mag ** k) * jnp.sin(a_im * dt * k)

    taps = []
    for s in range(SSM_TAPS):
        p_re, p_im = power(float(s))
        taps.append(jnp.stack([p_re[..., None] * bb_re - p_im[..., None] * bb_im,
                               p_re[..., None] * bb_im + p_im[..., None] * bb_re], axis=-1))
    grp = (LANES // 2) // SSM_GROUP
    n_blk = N_GROUPS_C // grp
    v = jnp.stack(taps, axis=0).reshape(SSM_TAPS, n_blk, grp, SSM_STATE, SSM_GROUP, 2)
    w8 = jnp.einsum('sbgpcq,gh->bsgcqhp', v, jnp.eye(grp, dtype=F32))
    w8 = w8.reshape(n_blk, SSM_TAPS * grp * SSM_GROUP, 2 * grp * SSM_STATE).astype(BF16)
    p_re, p_im = power(float(SSM_TAPS))
    l8 = jnp.concatenate([p_re.reshape(1, N_STATE), p_im.reshape(1, N_STATE)], axis=1)
    return w8, cd, jnp.broadcast_to(l8, (SUBLANES, 2 * N_STATE))


def _ssm_branch(c_u, bsz, s_len, w8, cd, l8, d_skip, w_glu_bf, layer, b_glu, t_chunk):
    n_chunks = s_len // t_chunk
    const2 = lambda b, c: (0, 0)
    kern = functools.partial(_ssm_kernel, t_chunk=t_chunk)
    return pl.pallas_call(
        kern,
        grid=(bsz, n_chunks),
        in_specs=[
            pl.BlockSpec((t_chunk, WIDTH_C), lambda b, c: (b * n_chunks + c, 0)),
            pl.BlockSpec((None,) + w8.shape[1:], lambda b, c: (layer, 0, 0, 0)),
            pl.BlockSpec((None, 2 * N_STATE, WIDTH_C), lambda b, c: (layer, 0, 0)),
            pl.BlockSpec((None, SUBLANES, 2 * N_STATE), lambda b, c: (layer, 0, 0)),
            pl.BlockSpec((1, WIDTH_C), const2),
            pl.BlockSpec((None, WIDTH_C, WIDTH_C), lambda b, c: (layer, 0, 0)),
            pl.BlockSpec((1, WIDTH_C), const2),
        ],
        out_specs=pl.BlockSpec((t_chunk, WIDTH_C), lambda b, c: (b * n_chunks + c, 0)),
        out_shape=jax.ShapeDtypeStruct((bsz * s_len, WIDTH_C), BF16),
        scratch_shapes=[pltpu.VMEM((t_chunk + SUBLANES, WIDTH_C), F32),
                        pltpu.VMEM((t_chunk, 2 * N_STATE), F32),
                        pltpu.VMEM((SUBLANES, 2 * N_STATE), F32)],
        compiler_params=_cparams(("parallel", "arbitrary")),
        name="s5_scan_glu",
    )(c_u, w8, cd, l8, d_skip, w_glu_bf, b_glu)


def _merge_body(x_ref, oa_ref, o0_ref, o1_ref, o2_ref, l0_ref, l1_ref, l2_ref, oc_ref,
                g0_ref, g1_ref, g2_ref, wa_ref, wb_ref, wc_ref, wo_ref, tok_scr, lse_scr, tm):
    for gi, (o_ref, l_ref) in enumerate(((o1_ref, l1_ref), (o2_ref, l2_ref))):
        dil = DIL_PAIRS[gi + 1][1]
        for r in range(dil):
            lse_scr[gi, pl.ds(r, tm // dil, stride=dil), :] = l_ref[:, r * LANES:(r + 1) * LANES]
            for li in range(WIDTH_B // LANES):
                cols = slice(r * WIDTH_B + li * LANES, r * WIDTH_B + (li + 1) * LANES)
                tok_scr[gi, li, pl.ds(r, tm // dil, stride=dil), :] = o_ref[:, cols].astype(F32)
    tok = lambda k: jnp.concatenate([tok_scr[k, li] for li in range(WIDTH_B // LANES)], axis=1)
    l0, l1, l2 = l0_ref[...], lse_scr[0], lse_scr[1]
    m = jnp.maximum(jnp.maximum(l0, l1), l2)
    e0, e1, e2 = jnp.exp(l0 - m), jnp.exp(l1 - m), jnp.exp(l2 - m)
    tot = e0 + e1 + e2
    sel_r = lax.broadcasted_iota(jnp.int32, (LANES, WIDTH_B), 0)
    sel_c = lax.broadcasted_iota(jnp.int32, (LANES, WIDTH_B), 1)
    select = jnp.where(sel_r == LSE_LANES * (sel_c // HD_B), 1.0, 0.0).astype(BF16)

    def spread(w):
        hi = w.astype(BF16)
        lo = (w - hi.astype(F32)).astype(BF16)
        return (jnp.dot(hi, select, preferred_element_type=F32)
                + jnp.dot(lo, select, preferred_element_type=F32))

    ob = (spread(e0 / tot) * o0_ref[...].astype(F32) + spread(e1 / tot) * tok(0)
          + spread(e2 / tot) * tok(1))
    ya = jnp.dot(oa_ref[...], wa_ref[...], preferred_element_type=F32)
    yb = jnp.dot(ob.astype(BF16), wb_ref[...], preferred_element_type=F32)
    yc = jnp.dot(oc_ref[...], wc_ref[...], preferred_element_type=F32)
    merged = (g0_ref[...].astype(F32) * ya + g1_ref[...].astype(F32) * yb
              + g2_ref[...].astype(F32) * yc)
    return x_ref[...] + jnp.dot(merged.astype(BF16), wo_ref[...], preferred_element_type=F32)


def _ffn_conv_halo(a_scr, tm, tiles_per_seq):
    halo = SUBLANES

    @pl.when(pl.program_id(0) % tiles_per_seq == 0)
    def _():
        a_scr[0:halo, :] = jnp.zeros((halo, D_FF), F32)

    @pl.when(pl.program_id(0) % tiles_per_seq != 0)
    def _():
        a_scr[0:halo, :] = a_scr[tm:tm + halo, :]


def _ffn_body(x, g_ref, wup_ref, cw_ref, cb_ref, wdown_ref, o_ref, a_scr, tm):
    halo = SUBLANES
    ms = jnp.mean(x * x, axis=-1, keepdims=True)
    h = (x * lax.rsqrt(ms + EPS) * g_ref[...]).astype(BF16)

    n_tiles = D_FF // FFN_TILE
    tile_cols = lambda f: slice(f * FFN_TILE, (f + 1) * FFN_TILE)

    def up(f):
        a_scr[halo:halo + tm, tile_cols(f)] = jnp.dot(h, wup_ref[:, tile_cols(f)], preferred_element_type=F32)
        return jnp.dot(h, wup_ref[:, D_FF + f * FFN_TILE:D_FF + (f + 1) * FFN_TILE],
                       preferred_element_type=F32)

    y = x
    gate = up(0)
    for f in range(n_tiles):
        cols = tile_cols(f)
        next_gate = up(f + 1) if f + 1 < n_tiles else None
        conv = (cb_ref[:, cols] + cw_ref[0:1, cols] * a_scr[halo - 2:halo - 2 + tm, cols]
                + cw_ref[1:2, cols] * a_scr[halo - 1:halo - 1 + tm, cols]
                + cw_ref[2:3, cols] * a_scr[halo:halo + tm, cols])
        act = (conv * jax.nn.sigmoid(conv)) * gate
        y = y + jnp.dot(act.astype(BF16), wdown_ref[cols, :], preferred_element_type=F32)
        gate = next_gate
    o_ref[...] = y


N_MERGE_INPUTS = 16


def _merge_ffn_kernel(*refs, tm, tiles_per_seq):
    merge_in = refs[:N_MERGE_INPUTS]
    g_ref, wup_ref, cw_ref, cb_ref, wdown_ref, o_ref, tok_scr, lse_scr, a_scr = refs[N_MERGE_INPUTS:]
    _ffn_conv_halo(a_scr, tm, tiles_per_seq)
    x_mid = _merge_body(*merge_in, tok_scr, lse_scr, tm)
    _ffn_body(x_mid, g_ref, wup_ref, cw_ref, cb_ref, wdown_ref, o_ref, a_scr, tm)


def _merge_ffn(x2d, oa, ob_parts, lse_parts, oc, gates, wa, wb, wc, wo,
               norm_g, wup_bf, conv_w, conv_b, wdown_bf, layer, s_len, tm):
    n = x2d.shape[0]
    row = lambda i: (i, 0)
    const = lambda i: (0, 0)
    half = pl.BlockSpec((tm, WIDTH_C), row)
    full = pl.BlockSpec((tm, D_MODEL), row)
    single = pl.Buffered(1)
    wspec_half = pl.BlockSpec((None, WIDTH_C, D_MODEL), lambda i: (layer, 0, 0), pipeline_mode=single)
    gspec = lambda k: pl.BlockSpec((tm, D_MODEL), lambda i: (i, k))
    dilated = lambda g, width: pl.BlockSpec((tm // DIL_PAIRS[g][1], DIL_PAIRS[g][1] * width), row)
    merge_specs = [full, half, half, dilated(1, WIDTH_B), dilated(2, WIDTH_B),
                   pl.BlockSpec((tm, LANES), row), dilated(1, LANES), dilated(2, LANES), half,
                   gspec(0), gspec(1), gspec(2),
                   wspec_half, wspec_half, wspec_half,
                   pl.BlockSpec((None, D_MODEL, D_MODEL), lambda i: (layer, 0, 0), pipeline_mode=single)]
    assert len(merge_specs) == N_MERGE_INPUTS
    ffn_specs = [
        pl.BlockSpec((1, D_MODEL), const),
        pl.BlockSpec((None, D_MODEL, 2 * D_FF), lambda i: (layer, 0, 0), pipeline_mode=single),
        pl.BlockSpec((CONV_WIDTH, D_FF), const),
        pl.BlockSpec((1, D_FF), const),
        pl.BlockSpec((None, D_FF, D_MODEL), lambda i: (layer, 0, 0), pipeline_mode=single),
    ]
    kern = functools.partial(_merge_ffn_kernel, tm=tm, tiles_per_seq=s_len // tm)
    return pl.pallas_call(
        kern,
        grid=(n // tm,),
        in_specs=merge_specs + ffn_specs,
        out_specs=full,
        out_shape=jax.ShapeDtypeStruct((n, D_MODEL), F32),
        scratch_shapes=[pltpu.VMEM((2, WIDTH_B // LANES, tm, LANES), F32),
                        pltpu.VMEM((2, tm, LANES), F32),
                        pltpu.VMEM((tm + SUBLANES, D_FF), F32)],
        compiler_params=_cparams(("arbitrary",)),
        name="merge_conv_ffn",
    )(x2d, oa, *ob_parts, *lse_parts, oc, gates, gates, gates, wa, wb, wc, wo,
      norm_g, wup_bf, conv_w, conv_b, wdown_bf)


def _lane_order(head_dim):
    half = head_dim // ROPE_FRACTION // 2
    heads = LANES // head_dim
    quarter = ROT_SHIFT // heads
    first, second = [], []
    for hd in range(heads):
        base = hd * head_dim
        first += list(range(base, base + half)) + list(range(base + 2 * half, base + half + quarter))
        second += list(range(base + half, base + 2 * half)) + list(range(base + half + quarter, base + head_dim))
    return first + second


def _permute_groups(w_cols, head_dim):
    order = _lane_order(head_dim)
    groups = w_cols.shape[-1] // LANES
    src = jnp.asarray([g * LANES + o for g in range(groups) for o in order])
    perm = jnp.zeros((groups * LANES,) * 2, F32).at[src, jnp.arange(groups * LANES)].set(1.0).astype(w_cols.dtype)
    return jnp.dot(w_cols, perm, preferred_element_type=F32).astype(w_cols.dtype)


def _rope_tables(positions, head_dim):
    rot = head_dim // ROPE_FRACTION
    half = rot // 2
    heads = LANES // head_dim
    quarter = ROT_SHIFT // heads
    inv = ROPE_THETA ** (-jnp.arange(0, rot, 2, dtype=F32) / rot)
    ang = positions.reshape(-1).astype(F32)[:, None] * inv
    cos, sin = jnp.cos(ang), jnp.sin(ang)
    n = ang.shape[0]
    pad = quarter - half
    c_q = jnp.concatenate([cos, jnp.ones((n, pad), F32)], axis=1)
    s_q = jnp.concatenate([sin, jnp.zeros((n, pad), F32)], axis=1)
    c = jnp.tile(c_q, (1, 2 * heads))
    s = jnp.concatenate([jnp.tile(-s_q, (1, heads)), jnp.tile(s_q, (1, heads))], axis=1)
    return c, s


def kernel(x, positions, attn_norm_g, w_in, b_gate, qn_a, kn_a, lam_q1, lam_k1, lam_q2, lam_k2, subln_g, w_br_a, qn_b, kn_b, w_br_b, ssm_a_re, ssm_a_im, ssm_log_dt, ssm_b_re, ssm_b_im, ssm_c_re, ssm_c_im, ssm_d, w_glu, b_glu, w_br_c, w_out, ffn_norm_g, w_up, conv_w, conv_b, w_down):
    bsz, s_len, d_model = x.shape
    depth = w_in.shape[0]
    assert d_model == D_MODEL and w_in.shape[2] == IN_COLS
    assert s_len % DIL_PAIRS[-1][0] == 0, "sequence must be a multiple of the largest dilated window"
    n = bsz * s_len

    tm_proj = 256
    tq_a = min(512, s_len)
    tq_b = 1024
    t_ssm = min(1024, s_len)
    tm_ffn = min(256, s_len)

    rope = _rope_tables(positions, HD_A) + _rope_tables(positions, HD_B)
    w_in_bf = w_in.astype(BF16)
    w_rows = w_in_bf.reshape(depth * D_MODEL, IN_COLS)
    qk_chunk = lambda c0, hd: _permute_groups(w_rows[:, c0:c0 + QK_CHUNK], hd).reshape(depth, D_MODEL, QK_CHUNK)
    w_qk = [qk_chunk(0, HD_A)] + [qk_chunk(TILE_BQ * COL_TILE + k * QK_CHUNK, HD_B)
                                  for k in range(2 * B_COLS // QK_CHUNK)]
    order_a = jnp.asarray(_lane_order(HD_A))
    order_b = jnp.asarray(_lane_order(HD_B))
    wa_bf, wb_bf, wc_bf, wo_bf = (w.astype(BF16) for w in (w_br_a, w_br_b, w_br_c, w_out))
    wglu_bf, wup_bf, wdown_bf = w_glu.astype(BF16), w_up.astype(BF16), w_down.astype(BF16)

    w8, cd, l8 = jax.vmap(_ssm_operands)(ssm_a_re, ssm_a_im, ssm_log_dt, ssm_b_re, ssm_b_im, ssm_c_re, ssm_c_im)

    x2d = x.reshape(n, D_MODEL)
    for l in range(depth):
        lam_init = 0.8 - 0.6 * math.exp(-0.3 * l)
        ones = lambda w: jnp.ones((w,), F32)
        group_a = lambda g: jnp.tile(jnp.tile(g.astype(F32), LANES // HD_A)[order_a], A_Q_COLS // LANES)
        group_b = lambda g: jnp.tile(g.astype(F32)[order_b], B_COLS // LANES)
        col_gain = jnp.concatenate([
            group_a(qn_a[l]) * (math.log2(math.e) / math.sqrt(HD_A)), group_a(kn_a[l]), ones(WIDTH_A),
            group_b(qn_b[l]), group_b(kn_b[l]),
            ones(B_COLS + WIDTH_C + N_BRANCHES * D_MODEL)]).reshape(1, IN_COLS)
        col_bias = jnp.concatenate([jnp.zeros((IN_COLS - N_BRANCHES * D_MODEL,), F32),
                                    b_gate[l].astype(F32)]).reshape(1, IN_COLS)
        qk_a, v_t, *qkv_b, c_u, gates = _in_projection(
            x2d, attn_norm_g[l].reshape(1, D_MODEL).astype(F32), w_in_bf, w_qk, l, col_gain, col_bias, rope,
            bsz, s_len, tm_proj)

        lam_p = jnp.stack([lam_q1[l], lam_k1[l], lam_q2[l], lam_k2[l]]).astype(F32)
        oa = _diff_attention(qk_a.reshape(bsz, s_len, 2 * A_Q_COLS), v_t, lam_p,
                             subln_g[l].reshape(1, 2 * HD_A).astype(F32), lam_init, tq_a)
        oa = oa.reshape(n, WIDTH_A)

        ob_parts, lse_parts = zip(*[_dilated_attention(qkv_b[g], bsz, s_len, g, tq_b) for g in range(N_DIL)])

        oc = _ssm_branch(c_u, bsz, s_len, w8, cd, l8, ssm_d[l].reshape(1, WIDTH_C).astype(F32),
                         wglu_bf, l, b_glu[l].reshape(1, WIDTH_C).astype(F32), t_ssm)

        x2d = _merge_ffn(x2d, oa, ob_parts, lse_parts, oc, gates, wa_bf, wb_bf, wc_bf, wo_bf,
                         ffn_norm_g[l].reshape(1, D_MODEL).astype(F32), wup_bf,
                         conv_w[l].astype(F32), conv_b[l].reshape(1, D_FF).astype(F32), wdown_bf, l,
                         s_len, tm_ffn)
    return x2d.reshape(bsz, s_len, D_MODEL)
```

```python
import functools
import math

import jax
import jax.numpy as jnp
from jax import lax
from jax.experimental import pallas as pl
from jax.experimental.pallas import tpu as pltpu

F32 = jnp.float32
BF16 = jnp.bfloat16

LANES = 128
SUBLANES = 8
ROT_SHIFT = LANES // 2

D_MODEL = 1024
N_HEADS_A = 4
HD_A = 64
N_DIL = 3
DIL_PAIRS = ((128, 1), (512, 4), (2048, 16))
N_HEADS_B = 4
HD_B = 128
WIDTH_B = N_HEADS_B * HD_B
SSM_GROUP = 16
SSM_STATE = 64
WIDTH_C = 512
N_GROUPS_C = WIDTH_C // SSM_GROUP
N_STATE = N_GROUPS_C * SSM_STATE
N_BRANCHES = 3
D_FF = 2816
CONV_WIDTH = 3
ROPE_THETA = 500000.0
ROPE_FRACTION = 4
EPS = 1e-6

COL_TILE = 512
QK_CHUNK = 2 * COL_TILE
A_Q_COLS = 2 * N_HEADS_A * HD_A
WIDTH_A = N_HEADS_A * 2 * HD_A
B_COLS = N_DIL * N_HEADS_B * HD_B
IN_COLS = 2 * A_Q_COLS + WIDTH_A + 3 * B_COLS + WIDTH_C + N_BRANCHES * D_MODEL
N_COL_TILES = IN_COLS // COL_TILE
BF_TILES = (2 * A_Q_COLS + WIDTH_A + 3 * B_COLS) // COL_TILE
BF_COLS = BF_TILES * COL_TILE
TILE_AQ, TILE_AK, TILE_AV = 0, 1, 2
TILE_BQ, TILE_BK, TILE_BV = 3, 6, 9
TILE_CU = 12

HEADS_PER_STEP = 2
ATTN_UNROLL = 2
VT_PAD = 16
VT_ROWS = 2 * HD_A + VT_PAD
LSE_LANES = LANES // N_HEADS_B
SSM_SLAB = 256
SSM_TAPS = SUBLANES
FFN_TILE = 256
MERGE_SLOT_B = 2
MERGE_SLOT_OUT = 6

VMEM_LIMIT = 56 * 1024 * 1024


def _cparams(sem):
    return pltpu.CompilerParams(dimension_semantics=sem, vmem_limit_bytes=VMEM_LIMIT)


def _first_head_lanes():
    lane = lax.broadcasted_iota(jnp.int32, (1, LANES), 1)
    return (lane // (ROT_SHIFT // 2)) % 2 == 0


def _norm_rope_tile(acc, gain, seg, c_ref, s_ref):
    cos = c_ref[...]
    sin = s_ref[...]
    first = _first_head_lanes()
    outs = []
    for gi in range(acc.shape[1] // LANES):
        y = acc[:, gi * LANES:(gi + 1) * LANES]
        ysq = y * y
        tot = jnp.sum(ysq, axis=-1, keepdims=True)
        if seg == LANES:
            ssum = tot
        else:
            one = jnp.sum(jnp.where(first, ysq, 0.0), axis=-1, keepdims=True)
            ssum = jnp.where(first, one, tot - one)
        yn = y * lax.rsqrt(ssum * (1.0 / seg) + EPS) * gain[:, gi * LANES:(gi + 1) * LANES]
        outs.append(yn * cos + pltpu.roll(yn, ROT_SHIFT, 1) * sin)
    return jnp.concatenate(outs, axis=1)


def _inproj_kernel(x_ref, g_ref, w_ref, wqa_ref, wqb0_ref, wqb1_ref, wqb2_ref, gain_ref, bias_ref,
                   ca_ref, sa_ref, cb_ref, sb_ref,
                   qk_ref, vt_ref, b0_ref, b1_ref, b2_ref, cu_ref, gate_ref, dil_scr, *, tm):
    x = x_ref[...]
    ms = jnp.mean(x * x, axis=-1, keepdims=True)
    h = (x * lax.rsqrt(ms + EPS) * g_ref[...]).astype(BF16)
    dil_refs = (b0_ref, b1_ref, b2_ref)
    heavy = [TILE_AQ, TILE_AK] + list(range(TILE_BQ, TILE_BV))
    light = [TILE_AV] + list(range(TILE_BV, TILE_CU + 1))
    order = heavy + [j for j in range(N_COL_TILES) if j not in heavy + light] + light
    wqb_refs = (wqb0_ref, wqb1_ref, wqb2_ref)

    def project(j):
        if j in (TILE_AQ, TILE_AK):
            ref, c0 = wqa_ref, j * COL_TILE
        elif TILE_BQ <= j < TILE_BV:
            off = (j - TILE_BQ) * COL_TILE
            ref, c0 = wqb_refs[off // QK_CHUNK], off % QK_CHUNK
        else:
            ref, c0 = w_ref, j * COL_TILE
        return jnp.dot(h, ref[:, c0:c0 + COL_TILE], preferred_element_type=F32)

    nxt = project(order[0])
    for idx, j in enumerate(order):
        cols = slice(j * COL_TILE, (j + 1) * COL_TILE)
        acc = nxt
        if idx + 1 < len(order):
            nxt = project(order[idx + 1])
        if j in (TILE_AQ, TILE_AK):
            qk_ref[:, cols] = _norm_rope_tile(acc, gain_ref[:, cols], HD_A, ca_ref, sa_ref).astype(BF16)
        elif j == TILE_AV:
            acc_t = acc.T.astype(BF16)
            for hd in range(N_HEADS_A):
                vt_ref[hd * VT_ROWS:hd * VT_ROWS + 2 * HD_A, :] = acc_t[hd * 2 * HD_A:(hd + 1) * 2 * HD_A, :]
                vt_ref[hd * VT_ROWS + 2 * HD_A:(hd + 1) * VT_ROWS, :] = jnp.ones((VT_PAD, tm), BF16)
        elif j < TILE_CU:
            part, group = divmod(j - TILE_BQ, N_DIL)
            if part < 2:
                acc = _norm_rope_tile(acc, gain_ref[:, cols], HD_B, cb_ref, sb_ref)
            dil = DIL_PAIRS[group][1]
            out_ref = dil_refs[group]
            if dil == 1:
                out_ref[:, part * WIDTH_B:(part + 1) * WIDTH_B] = acc.astype(BF16)
            else:
                slot = (group - 1) * 3 + part
                for gi in range(WIDTH_B // LANES):
                    dil_scr[slot, gi] = acc[:, gi * LANES:(gi + 1) * LANES]
                for r in range(dil):
                    for gi in range(WIDTH_B // LANES):
                        c0 = (r * 3 + part) * WIDTH_B + gi * LANES
                        out_ref[:, c0:c0 + LANES] = (
                            dil_scr[slot, gi, pl.ds(r, tm // dil, stride=dil), :].astype(BF16))
        elif j == TILE_CU:
            cu_ref[...] = acc
        else:
            g0 = (j - TILE_CU - 1) * COL_TILE
            gate_ref[:, g0:g0 + COL_TILE] = jax.nn.sigmoid(acc + bias_ref[:, cols]).astype(gate_ref.dtype)


def _in_projection(x2d, norm_g, w_bf, w_qk, layer, col_gain, col_bias, rope, bsz, s_len, tm):
    n = x2d.shape[0]
    tiles_per_seq = s_len // tm
    row = lambda i: (i, 0)
    const = lambda i: (0, 0)
    d1, d2 = DIL_PAIRS[1][1], DIL_PAIRS[2][1]
    kern = functools.partial(_inproj_kernel, tm=tm)
    resident = lambda cols: pl.BlockSpec((None, D_MODEL, cols), lambda i: (layer, 0, 0),
                                         pipeline_mode=pl.Buffered(1))
    return pl.pallas_call(
        kern,
        grid=(n // tm,),
        in_specs=[
            pl.BlockSpec((tm, D_MODEL), row),
            pl.BlockSpec((1, D_MODEL), const),
            resident(IN_COLS), resident(QK_CHUNK), resident(QK_CHUNK), resident(QK_CHUNK), resident(QK_CHUNK),
            pl.BlockSpec((1, IN_COLS), const),
            pl.BlockSpec((1, IN_COLS), const),
        ] + [pl.BlockSpec((tm, LANES), row)] * len(rope),
        out_specs=[
            pl.BlockSpec((tm, 2 * A_Q_COLS), row),
            pl.BlockSpec((None, N_HEADS_A * VT_ROWS, tm), lambda i: (i // tiles_per_seq, 0, i % tiles_per_seq)),
            pl.BlockSpec((tm, 3 * WIDTH_B), row),
            pl.BlockSpec((tm // d1, d1 * 3 * WIDTH_B), row),
            pl.BlockSpec((tm // d2, d2 * 3 * WIDTH_B), row),
            pl.BlockSpec((tm, WIDTH_C), row),
            pl.BlockSpec((tm, N_BRANCHES * D_MODEL), row),
        ],
        out_shape=[jax.ShapeDtypeStruct((n, 2 * A_Q_COLS), BF16),
                   jax.ShapeDtypeStruct((bsz, N_HEADS_A * VT_ROWS, s_len), BF16),
                   jax.ShapeDtypeStruct((n, 3 * WIDTH_B), BF16),
                   jax.ShapeDtypeStruct((n // d1, d1 * 3 * WIDTH_B), BF16),
                   jax.ShapeDtypeStruct((n // d2, d2 * 3 * WIDTH_B), BF16),
                   jax.ShapeDtypeStruct((n, WIDTH_C), F32),
                   jax.ShapeDtypeStruct((n, N_BRANCHES * D_MODEL), BF16)],
        scratch_shapes=[pltpu.VMEM((6, WIDTH_B // LANES, tm, LANES), F32)],
        compiler_params=_cparams(("parallel",)),
        name="in_projection",
    )(x2d, norm_g, w_bf, *w_qk, col_gain, col_bias, *rope)


def _diffattn_kernel(q_ref, k_ref, vt_ref, mask_ref, lam_ref, subg_ref, o_ref, acc_scr, s_scr, *, tq, lam_init):
    i = pl.program_id(2)
    q = q_ref[...]
    first = _first_head_lanes()
    nt = (((1,), (1,)), ((), ()))
    qm = []
    for hh in range(HEADS_PER_STEP):
        qh = q[:, hh * 2 * HD_A:(hh + 1) * 2 * HD_A]
        zero = jnp.zeros_like(qh)
        qm += [jnp.where(first, qh, zero), jnp.where(first, zero, qh)]
    n_maps = len(qm)
    acc_scr[...] = jnp.zeros_like(acc_scr)

    def score(j, slot, mi):
        r0 = pl.multiple_of(j * tq, tq)
        hh = mi // 2
        kblk = k_ref[pl.ds(r0, tq), hh * 2 * HD_A:(hh + 1) * 2 * HD_A]
        s_scr[slot, mi] = lax.dot_general(kblk, qm[mi], nt, preferred_element_type=F32)

    def consume_one(j, slot, mi, m_old, diagonal):
        r0 = pl.multiple_of(j * tq, tq)
        hh = mi // 2
        vtblk = vt_ref[hh * VT_ROWS:(hh + 1) * VT_ROWS, pl.ds(r0, tq)]
        st = s_scr[slot, mi]
        if diagonal:
            st = st + mask_ref[...]
        m_new = jnp.maximum(m_old, jnp.max(st, axis=0, keepdims=True))
        p = jnp.exp2(st - m_new)
        alpha = jnp.exp2(m_old - m_new)
        acc_scr[mi] = alpha * acc_scr[mi] + jnp.dot(vtblk, p.astype(BF16), preferred_element_type=F32)
        return m_new

    def scores(j, slot):
        for mi in range(n_maps):
            score(j, slot, mi)

    def consume(j, slot, ms, diagonal):
        return tuple(consume_one(j, slot, mi, ms[mi], diagonal) for mi in range(n_maps))

    def stage(j, src, ms):
        new = []
        for mi in range(n_maps):
            score(j + 1, 1 - src, mi)
            new.append(consume_one(j, src, mi, ms[mi], False))
        return tuple(new)

    init = tuple(jnp.full((1, tq), -jnp.inf, F32) for _ in range(n_maps))
    def stages(j0, count, ms):
        for u in range(count):
            ms = stage(j0 + u, u % 2, ms)
        return ms

    scores(0, 0)
    ms = lax.fori_loop(0, i // ATTN_UNROLL, lambda t, c: stages(ATTN_UNROLL * t, ATTN_UNROLL, c), init)
    rem = lax.rem(i, ATTN_UNROLL)
    for r in range(ATTN_UNROLL):
        @pl.when(rem == r)
        def _(r=r):
            consume(i, r % 2, stages(i - r, r, ms), True)

    lam_p = lam_ref[...]
    lam = (jnp.exp(jnp.sum(lam_p[0:1] * lam_p[1:2], axis=-1, keepdims=True))
           - jnp.exp(jnp.sum(lam_p[2:3] * lam_p[3:4], axis=-1, keepdims=True)) + lam_init)
    vals = slice(0, 2 * HD_A)
    den = slice(2 * HD_A, 2 * HD_A + 1)
    for hh in range(HEADS_PER_STEP):
        a1, a2 = acc_scr[2 * hh], acc_scr[2 * hh + 1]
        o_t = a1[vals] / a1[den] - lam * (a2[vals] / a2[den])
        msq = jnp.mean(o_t * o_t, axis=0, keepdims=True)
        o = (o_t * lax.rsqrt(msq + EPS)).T
        o_ref[:, hh * 2 * HD_A:(hh + 1) * 2 * HD_A] = ((o * subg_ref[...]) * (1.0 - lam_init)).astype(o_ref.dtype)


def _diff_attention(qk, v_t, lam_p, subln_g, lam_init, tq):
    bsz, s_len, _ = qk.shape
    width = HEADS_PER_STEP * 2 * HD_A
    kern = functools.partial(_diffattn_kernel, tq=tq, lam_init=lam_init)
    key = lax.broadcasted_iota(jnp.int32, (tq, tq), 0)
    qry = lax.broadcasted_iota(jnp.int32, (tq, tq), 1)
    causal = jnp.where(key <= qry, 0.0, -jnp.inf).astype(F32)
    return pl.pallas_call(
        kern,
        grid=(bsz, N_HEADS_A // HEADS_PER_STEP, s_len // tq),
        in_specs=[
            pl.BlockSpec((None, tq, width), lambda b, h, i: (b, i, h)),
            pl.BlockSpec((None, s_len, width), lambda b, h, i: (b, 0, N_HEADS_A // HEADS_PER_STEP + h)),
            pl.BlockSpec((None, HEADS_PER_STEP * VT_ROWS, s_len), lambda b, h, i: (b, h, 0)),
            pl.BlockSpec((tq, tq), lambda b, h, i: (0, 0)),
            pl.BlockSpec((4, HD_A), lambda b, h, i: (0, 0)),
            pl.BlockSpec((1, 2 * HD_A), lambda b, h, i: (0, 0)),
        ],
        out_specs=pl.BlockSpec((None, tq, width), lambda b, h, i: (b, i, h)),
        out_shape=jax.ShapeDtypeStruct((bsz, s_len, WIDTH_A), BF16),
        scratch_shapes=[pltpu.VMEM((2 * HEADS_PER_STEP, VT_ROWS, tq), F32),
                        pltpu.VMEM((2, 2 * HEADS_PER_STEP, tq, tq), F32)],
        compiler_params=_cparams(("parallel", "parallel", "arbitrary")),
        name="diff_attention",
    )(qk, qk, v_t, causal, lam_p, subln_g)


def _dilated_kernel(q_ref, k_ref, kp_ref, v_ref, vp_ref, o_ref, lse_ref, *, tq, blk):
    n = pl.program_id(2)
    scale = 1.0 / math.sqrt(HD_B)
    nt = (((1,), (1,)), ((), ()))
    rr = lax.broadcasted_iota(jnp.int32, (blk, 2 * blk), 0)
    cc = lax.broadcasted_iota(jnp.int32, (blk, 2 * blk), 1)
    band = jnp.logical_and(cc >= rr, cc <= rr + blk)
    lane = lax.broadcasted_iota(jnp.int32, (1, 2 * blk), 1)
    first_bias = jnp.where(lane >= blk, 0.0, jnp.where(n > 0, 0.0, -jnp.inf).astype(F32))

    def window(ref, pref, c, cols):
        if c == 0:
            return jnp.concatenate([pref[:, cols], ref[0:blk, cols]], axis=0)
        return ref[(c - 1) * blk:(c + 1) * blk, cols]

    def scores(unit):
        h, c = unit
        cols = slice(h * HD_B, (h + 1) * HD_B)
        qh = q_ref[c * blk:(c + 1) * blk, cols]
        s = lax.dot_general(qh, window(k_ref, kp_ref, c, cols), nt, preferred_element_type=F32) * scale
        s = jnp.where(band, s, -jnp.inf)
        return s + first_bias if c == 0 else s

    head_lane = lax.broadcasted_iota(jnp.int32, (1, LANES), 1) // LSE_LANES

    def finish(unit, s, lse_acc):
        h, c = unit
        cols = slice(h * HD_B, (h + 1) * HD_B)
        rows = slice(c * blk, (c + 1) * blk)
        m = jnp.max(s, axis=-1, keepdims=True)
        p = jnp.exp(s - m)
        den = jnp.sum(p, axis=-1, keepdims=True)
        pv = jnp.dot(p.astype(BF16), window(v_ref, vp_ref, c, cols), preferred_element_type=F32)
        o_ref[rows, cols] = (pv / den).astype(o_ref.dtype)
        lse = jnp.broadcast_to(m + jnp.log(den), (blk, LANES))
        lse_acc = lse if h == 0 else jnp.where(head_lane == h, lse, lse_acc)
        if h == N_HEADS_B - 1:
            lse_ref[rows, :] = lse_acc
        return lse_acc

    units = [(h, c) for c in range(tq // blk) for h in range(N_HEADS_B)]
    s = scores(units[0])
    lse_acc = None
    for idx, unit in enumerate(units):
        s_next = scores(units[idx + 1]) if idx + 1 < len(units) else None
        lse_acc = finish(unit, s, lse_acc)
        s = s_next


def _dilated_attention(qkv, bsz, s_len, group, tq):
    window, dil = DIL_PAIRS[group]
    blk = window // dil
    rows = s_len // dil
    tq = min(tq, rows)
    per_res = 3
    view = qkv.reshape(bsz, rows, dil * per_res * WIDTH_B)
    sub = tq // blk
    qcol = lambda r: r * per_res
    kcol = lambda r: r * per_res + 1
    vcol = lambda r: r * per_res + 2
    prev = lambda n: jnp.maximum(n * sub - 1, 0)
    kern = functools.partial(_dilated_kernel, tq=tq, blk=blk)
    o, lse = pl.pallas_call(
        kern,
        grid=(bsz, dil, rows // tq),
        in_specs=[
            pl.BlockSpec((None, tq, WIDTH_B), lambda b, r, n: (b, n, qcol(r))),
            pl.BlockSpec((None, tq, WIDTH_B), lambda b, r, n: (b, n, kcol(r))),
            pl.BlockSpec((None, blk, WIDTH_B), lambda b, r, n: (b, prev(n), kcol(r))),
            pl.BlockSpec((None, tq, WIDTH_B), lambda b, r, n: (b, n, vcol(r))),
            pl.BlockSpec((None, blk, WIDTH_B), lambda b, r, n: (b, prev(n), vcol(r))),
        ],
        out_specs=[pl.BlockSpec((None, tq, WIDTH_B), lambda b, r, n: (b, n, r)),
                   pl.BlockSpec((None, tq, LANES), lambda b, r, n: (b, n, r))],
        out_shape=[jax.ShapeDtypeStruct((bsz, rows, dil * WIDTH_B), BF16),
                   jax.ShapeDtypeStruct((bsz, rows, dil * LANES), F32)],
        compiler_params=_cparams(("parallel", "parallel", "arbitrary")),
        name=f"dilated_attention_g{group}",
    )(view, view, view, view, view)
    return o.reshape(bsz * rows, dil * WIDTH_B), lse.reshape(bsz * rows, dil * LANES)


def _ssm_kernel(u_ref, w8_ref, cd_ref, l8_ref, dskip_ref, wglu_ref, bglu_ref, o_ref,
                u_scr, x_scr, carry_scr, *, t_chunk):
    halo = SUBLANES

    @pl.when(pl.program_id(1) == 0)
    def _():
        carry_scr[...] = jnp.zeros_like(carry_scr)
        u_scr[0:halo, :] = jnp.zeros((halo, WIDTH_C), F32)

    @pl.when(pl.program_id(1) != 0)
    def _():
        u_scr[0:halo, :] = u_scr[t_chunk:t_chunk + halo, :]

    u = u_ref[...]
    u_scr[halo:halo + t_chunk, :] = u
    re = slice(0, N_STATE)
    im = slice(N_STATE, 2 * N_STATE)
    ch = WIDTH_C // 2
    sh = N_STATE // 2
    blk_ch = LANES // 2
    blk_st = blk_ch * SSM_STATE // SSM_GROUP
    low = lax.broadcasted_iota(jnp.int32, (1, LANES), 1) < blk_ch
    c_re, c_im = carry_scr[:, re], carry_scr[:, im]
    a8, b8 = l8_ref[:, re], l8_ref[:, im]
    n_slabs = t_chunk // SSM_SLAB
    slab_rows = lambda s: slice(s * SSM_SLAB, (s + 1) * SSM_SLAB)

    def drive(s):
        r0 = halo + s * SSM_SLAB
        for pair in range(WIDTH_C // LANES):
            lanes = slice(pair * LANES, (pair + 1) * LANES)
            taps = [u_scr[r0 - t:r0 - t + SSM_SLAB, lanes] for t in range(SSM_TAPS)]
            for half in range(2):
                cb = 2 * pair + half
                pieces = []
                for a in range(SSM_TAPS // 2):
                    even, odd = taps[2 * a], taps[2 * a + 1]
                    if half == 0:
                        v = jnp.where(low, even, pltpu.roll(odd, blk_ch, 1))
                    else:
                        v = jnp.where(low, pltpu.roll(even, blk_ch, 1), odd)
                    pieces.append(v.astype(BF16))
                w = jnp.dot(jnp.concatenate(pieces, axis=1), w8_ref[cb], preferred_element_type=F32)
                x_scr[slab_rows(s), cb * blk_st:(cb + 1) * blk_st] = w[:, :blk_st]
                x_scr[slab_rows(s), N_STATE + cb * blk_st:N_STATE + (cb + 1) * blk_st] = w[:, blk_st:]

    drive(0)
    if n_slabs > 1:
        drive(1)
    for s in range(n_slabs):
        rows = slab_rows(s)
        if s + 2 < n_slabs:
            drive(s + 2)
        for t in range(SSM_SLAB // SUBLANES):
            r8 = slice(s * SSM_SLAB + t * SUBLANES, s * SSM_SLAB + (t + 1) * SUBLANES)
            x_re = x_scr[r8, re] + (a8 * c_re - b8 * c_im)
            x_im = x_scr[r8, im] + (a8 * c_im + b8 * c_re)
            x_scr[r8, re] = x_re
            x_scr[r8, im] = x_im
            c_re, c_im = x_re, x_im
        halves = []
        for hh in range(2):
            acc = None
            for part in range(2):
                c0 = part * N_STATE + hh * sh
                d = jnp.dot(x_scr[rows, c0:c0 + sh].astype(BF16), cd_ref[c0:c0 + sh, hh * ch:(hh + 1) * ch],
                            preferred_element_type=F32)
                acc = d if acc is None else acc + d
            halves.append(acc)
        y = jnp.concatenate(halves, axis=1) + dskip_ref[...] * u[rows]
        cdf = 0.5 * (1.0 + jnp.tanh(math.sqrt(2.0 / math.pi) * (y + 0.044715 * (y * y * y))))
        z = y * cdf
        gate = jnp.dot(z.astype(BF16), wglu_ref[...], preferred_element_type=F32) + bglu_ref[...]
        o_ref[rows, :] = (z * jax.nn.sigmoid(gate)).astype(o_ref.dtype)
    carry_scr[:, re] = c_re
    carry_scr[:, im] = c_im


def _ssm_operands(a_re, a_im, log_dt, b_re, b_im, c_re, c_im):
    a_re, a_im = a_re.astype(F32), a_im.astype(F32)
    dt = jnp.exp(log_dt.astype(F32))[:, None]
    mag = jnp.exp(a_re * dt)
    lb_re, lb_im = mag * jnp.cos(a_im * dt), mag * jnp.sin(a_im * dt)
    n_re, n_im = lb_re - 1.0, lb_im
    den = a_re * a_re + a_im * a_im
    f_re = (n_re * a_re + n_im * a_im) / den
    f_im = (n_im * a_re - n_re * a_im) / den
    b_re, b_im = b_re.astype(F32), b_im.astype(F32)
    bb_re = f_re[..., None] * b_re - f_im[..., None] * b_im
    bb_im = f_re[..., None] * b_im + f_im[..., None] * b_re
    eye = jnp.eye(N_GROUPS_C, dtype=F32)
    blockdiag_out = lambda t: jnp.einsum('gcp,gh->gphc', t, eye).reshape(N_STATE, WIDTH_C)
    cd = jnp.concatenate([blockdiag_out(c_re.astype(F32)), -blockdiag_out(c_im.astype(F32))],
                         axis=0).astype(BF16)

    def power(k):
        return (mag ** k) * jnp.cos(a_im * dt * k), (mag ** k) * jnp.sin(a_im * dt * k)

    taps = []
    for s in range(SSM_TAPS):
        p_re, p_im = power(float(s))
        taps.append(jnp.stack([p_re[..., None] * bb_re - p_im[..., None] * bb_im,
                               p_re[..., None] * bb_im + p_im[..., None] * bb_re], axis=-1))
    grp = (LANES // 2) // SSM_GROUP
    n_blk = N_GROUPS_C // grp
    v = jnp.stack(taps, axis=0).reshape(SSM_TAPS, n_blk, grp, SSM_STATE, SSM_GROUP, 2)
    w8 = jnp.einsum('sbgpcq,gh->bsgcqhp', v, jnp.eye(grp, dtype=F32))
    w8 = w8.reshape(n_blk, SSM_TAPS * grp * SSM_GROUP, 2 * grp * SSM_STATE).astype(BF16)
    p_re, p_im = power(float(SSM_TAPS))
    l8 = jnp.concatenate([p_re.reshape(1, N_STATE), p_im.reshape(1, N_STATE)], axis=1)
    return w8, cd, jnp.broadcast_to(l8, (SUBLANES, 2 * N_STATE))


def _ssm_branch(c_u, bsz, s_len, w8, cd, l8, d_skip, w_glu_bf, layer, b_glu, t_chunk):
    n_chunks = s_len // t_chunk
    const2 = lambda b, c: (0, 0)
    kern = functools.partial(_ssm_kernel, t_chunk=t_chunk)
    return pl.pallas_call(
        kern,
        grid=(bsz, n_chunks),
        in_specs=[
            pl.BlockSpec((t_chunk, WIDTH_C), lambda b, c: (b * n_chunks + c, 0)),
            pl.BlockSpec((None,) + w8.shape[1:], lambda b, c: (layer, 0, 0, 0)),
            pl.BlockSpec((None, 2 * N_STATE, WIDTH_C), lambda b, c: (layer, 0, 0)),
            pl.BlockSpec((None, SUBLANES, 2 * N_STATE), lambda b, c: (layer, 0, 0)),
            pl.BlockSpec((1, WIDTH_C), const2),
            pl.BlockSpec((None, WIDTH_C, WIDTH_C), lambda b, c: (layer, 0, 0)),
            pl.BlockSpec((1, WIDTH_C), const2),
        ],
        out_specs=pl.BlockSpec((t_chunk, WIDTH_C), lambda b, c: (b * n_chunks + c, 0)),
        out_shape=jax.ShapeDtypeStruct((bsz * s_len, WIDTH_C), BF16),
        scratch_shapes=[pltpu.VMEM((t_chunk + SUBLANES, WIDTH_C), F32),
                        pltpu.VMEM((t_chunk, 2 * N_STATE), F32),
                        pltpu.VMEM((SUBLANES, 2 * N_STATE), F32)],
        compiler_params=_cparams(("parallel", "arbitrary")),
        name="s5_scan_glu",
    )(c_u, w8, cd, l8, d_skip, w_glu_bf, b_glu)


def _merge_ffn_kernel(x_ref, oa_ref, o0_ref, o1_ref, o2_ref, l0_ref, l1_ref, l2_ref, oc_ref,
                      g0_ref, g1_ref, g2_ref, wa_ref, wb_ref, wc_ref, wo_ref,
                      g_ref, wup_ref, cw_ref, cb_ref, wdown_ref,
                      o_ref, tok_scr, lse_scr, a_scr, xmid_scr, hmid_scr, *, tm, tiles_per_seq):
    step = pl.program_id(0)
    halo = SUBLANES

    @pl.when(step == 0)
    def _():
        xmid_scr[...] = jnp.zeros_like(xmid_scr)
        hmid_scr[...] = jnp.zeros_like(hmid_scr)
        a_scr[...] = jnp.zeros_like(a_scr)

    seq_start = lax.rem(jnp.maximum(step - 1, 0), tiles_per_seq) == 0

    @pl.when(seq_start)
    def _():
        a_scr[0:halo, :] = jnp.zeros((halo, D_FF), F32)

    @pl.when(jnp.logical_not(seq_start))
    def _():
        a_scr[0:halo, :] = a_scr[tm:tm + halo, :]

    x_prev = xmid_scr[...]
    h = hmid_scr[...]
    n_tiles = D_FF // FFN_TILE
    tile_cols = lambda f: slice(f * FFN_TILE, (f + 1) * FFN_TILE)

    def up(f):
        a_scr[halo:halo + tm, tile_cols(f)] = jnp.dot(h, wup_ref[:, tile_cols(f)], preferred_element_type=F32)
        return jnp.dot(h, wup_ref[:, D_FF + f * FFN_TILE:D_FF + (f + 1) * FFN_TILE],
                       preferred_element_type=F32)

    gate = up(0)

    ya = jnp.dot(oa_ref[...], wa_ref[...], preferred_element_type=F32)
    yc = jnp.dot(oc_ref[...], wc_ref[...], preferred_element_type=F32)
    for gi, (og_ref, lg_ref) in enumerate(((o1_ref, l1_ref), (o2_ref, l2_ref))):
        dil = DIL_PAIRS[gi + 1][1]
        for r in range(dil):
            lse_scr[gi, pl.ds(r, tm // dil, stride=dil), :] = lg_ref[:, r * LANES:(r + 1) * LANES]
            for li in range(WIDTH_B // LANES):
                cols = slice(r * WIDTH_B + li * LANES, r * WIDTH_B + (li + 1) * LANES)
                tok_scr[gi, li, pl.ds(r, tm // dil, stride=dil), :] = og_ref[:, cols].astype(F32)
    tok = lambda k: jnp.concatenate([tok_scr[k, li] for li in range(WIDTH_B // LANES)], axis=1)
    l0, l1, l2 = l0_ref[...], lse_scr[0], lse_scr[1]
    m = jnp.maximum(jnp.maximum(l0, l1), l2)
    e0, e1, e2 = jnp.exp(l0 - m), jnp.exp(l1 - m), jnp.exp(l2 - m)
    tot = e0 + e1 + e2
    sel_r = lax.broadcasted_iota(jnp.int32, (2 * LANES, WIDTH_B), 0) % LANES
    sel_c = lax.broadcasted_iota(jnp.int32, (2 * LANES, WIDTH_B), 1)
    select = jnp.where(sel_r == LSE_LANES * (sel_c // HD_B), 1.0, 0.0).astype(BF16)

    def spread(w):
        hi = w.astype(BF16)
        lo = (w - hi.astype(F32)).astype(BF16)
        return jnp.dot(jnp.concatenate([hi, lo], axis=1), select, preferred_element_type=F32)

    ob = (spread(e0 / tot) * o0_ref[...].astype(F32) + spread(e1 / tot) * tok(0)
          + spread(e2 / tot) * tok(1)).astype(BF16)

    y = x_prev
    merged = None
    for f in range(n_tiles):
        cols = tile_cols(f)
        next_gate = up(f + 1) if f + 1 < n_tiles else None
        if f == MERGE_SLOT_B:
            yb = jnp.dot(ob, wb_ref[...], preferred_element_type=F32)
            merged = (g0_ref[...].astype(F32) * ya + g1_ref[...].astype(F32) * yb
                      + g2_ref[...].astype(F32) * yc).astype(BF16)
        if f == MERGE_SLOT_OUT:
            x_new = x_ref[...] + jnp.dot(merged, wo_ref[...], preferred_element_type=F32)
            ms = jnp.mean(x_new * x_new, axis=-1, keepdims=True)
            xmid_scr[...] = x_new
            hmid_scr[...] = (x_new * lax.rsqrt(ms + EPS) * g_ref[...]).astype(BF16)
        conv = (cb_ref[:, cols] + cw_ref[0:1, cols] * a_scr[halo - 2:halo - 2 + tm, cols]
                + cw_ref[1:2, cols] * a_scr[halo - 1:halo - 1 + tm, cols]
                + cw_ref[2:3, cols] * a_scr[halo:halo + tm, cols])
        act = (conv * jax.nn.sigmoid(conv)) * gate
        y = y + jnp.dot(act.astype(BF16), wdown_ref[cols, :], preferred_element_type=F32)
        gate = next_gate
    o_ref[...] = y


def _merge_ffn(x2d, oa, ob_parts, lse_parts, oc, gates, wa, wb, wc, wo,
               norm_g, wup_bf, conv_w, conv_b, wdown_bf, layer, s_len, tm):
    n = x2d.shape[0]
    n_tiles = n // tm
    row = lambda i: (jnp.minimum(i, n_tiles - 1), 0)
    const = lambda i: (0, 0)
    half = pl.BlockSpec((tm, WIDTH_C), row)
    full = pl.BlockSpec((tm, D_MODEL), row)
    single = pl.Buffered(1)
    wspec_half = pl.BlockSpec((None, WIDTH_C, D_MODEL), lambda i: (layer, 0, 0), pipeline_mode=single)
    gspec = lambda k: pl.BlockSpec((tm, D_MODEL), lambda i: (jnp.minimum(i, n_tiles - 1), k))
    dilated = lambda g, width: pl.BlockSpec((tm // DIL_PAIRS[g][1], DIL_PAIRS[g][1] * width), row)
    merge_specs = [full, half, half, dilated(1, WIDTH_B), dilated(2, WIDTH_B),
                   pl.BlockSpec((tm, LANES), row), dilated(1, LANES), dilated(2, LANES), half,
                   gspec(0), gspec(1), gspec(2),
                   wspec_half, wspec_half, wspec_half,
                   pl.BlockSpec((None, D_MODEL, D_MODEL), lambda i: (layer, 0, 0), pipeline_mode=single)]
    ffn_specs = [
        pl.BlockSpec((1, D_MODEL), const),
        pl.BlockSpec((None, D_MODEL, 2 * D_FF), lambda i: (layer, 0, 0), pipeline_mode=single),
        pl.BlockSpec((CONV_WIDTH, D_FF), const),
        pl.BlockSpec((1, D_FF), const),
        pl.BlockSpec((None, D_FF, D_MODEL), lambda i: (layer, 0, 0), pipeline_mode=single),
    ]
    kern = functools.partial(_merge_ffn_kernel, tm=tm, tiles_per_seq=s_len // tm)
    return pl.pallas_call(
        kern,
        grid=(n_tiles + 1,),
        in_specs=merge_specs + ffn_specs,
        out_specs=pl.BlockSpec((tm, D_MODEL), lambda i: (jnp.maximum(i - 1, 0), 0)),
        out_shape=jax.ShapeDtypeStruct((n, D_MODEL), F32),
        scratch_shapes=[pltpu.VMEM((2, WIDTH_B // LANES, tm, LANES), F32),
                        pltpu.VMEM((2, tm, LANES), F32),
                        pltpu.VMEM((tm + SUBLANES, D_FF), F32),
                        pltpu.VMEM((tm, D_MODEL), F32),
                        pltpu.VMEM((tm, D_MODEL), BF16)],
        compiler_params=_cparams(("arbitrary",)),
        name="merge_conv_ffn",
    )(x2d, oa, *ob_parts, *lse_parts, oc, gates, gates, gates, wa, wb, wc, wo,
      norm_g, wup_bf, conv_w, conv_b, wdown_bf)


def _lane_order(head_dim):
    half = head_dim // ROPE_FRACTION // 2
    heads = LANES // head_dim
    quarter = ROT_SHIFT // heads
    first, second = [], []
    for hd in range(heads):
        base = hd * head_dim
        first += list(range(base, base + half)) + list(range(base + 2 * half, base + half + quarter))
        second += list(range(base + half, base + 2 * half)) + list(range(base + half + quarter, base + head_dim))
    return first + second


def _permute_groups(w_cols, head_dim):
    order = _lane_order(head_dim)
    groups = w_cols.shape[-1] // LANES
    src = jnp.asarray([g * LANES + o for g in range(groups) for o in order])
    perm = jnp.zeros((groups * LANES,) * 2, F32).at[src, jnp.arange(groups * LANES)].set(1.0).astype(w_cols.dtype)
    return jnp.dot(w_cols, perm, preferred_element_type=F32).astype(w_cols.dtype)


def _rope_tables(positions, head_dim):
    rot = head_dim // ROPE_FRACTION
    half = rot // 2
    heads = LANES // head_dim
    quarter = ROT_SHIFT // heads
    inv = ROPE_THETA ** (-jnp.arange(0, rot, 2, dtype=F32) / rot)
    ang = positions.reshape(-1).astype(F32)[:, None] * inv
    cos, sin = jnp.cos(ang), jnp.sin(ang)
    n = ang.shape[0]
    pad = quarter - half
    c_q = jnp.concatenate([cos, jnp.ones((n, pad), F32)], axis=1)
    s_q = jnp.concatenate([sin, jnp.zeros((n, pad), F32)], axis=1)
    c = jnp.tile(c_q, (1, 2 * heads))
    s = jnp.concatenate([jnp.tile(-s_q, (1, heads)), jnp.tile(s_q, (1, heads))], axis=1)
    return c, s


def kernel(x, positions, attn_norm_g, w_in, b_gate, qn_a, kn_a, lam_q1, lam_k1, lam_q2, lam_k2, subln_g, w_br_a, qn_b, kn_b, w_br_b, ssm_a_re, ssm_a_im, ssm_log_dt, ssm_b_re, ssm_b_im, ssm_c_re, ssm_c_im, ssm_d, w_glu, b_glu, w_br_c, w_out, ffn_norm_g, w_up, conv_w, conv_b, w_down):
    bsz, s_len, d_model = x.shape
    depth = w_in.shape[0]
    assert d_model == D_MODEL and w_in.shape[2] == IN_COLS
    assert s_len % DIL_PAIRS[-1][0] == 0, "sequence must be a multiple of the largest dilated window"
    n = bsz * s_len

    tm_proj = 256
    tq_a = min(512, s_len)
    tq_b = 1024
    t_ssm = min(1024, s_len)
    tm_ffn = min(256, s_len)

    rope = _rope_tables(positions, HD_A) + _rope_tables(positions, HD_B)
    w_in_bf = w_in.astype(BF16)
    w_rows = w_in_bf.reshape(depth * D_MODEL, IN_COLS)
    qk_chunk = lambda c0, hd: _permute_groups(w_rows[:, c0:c0 + QK_CHUNK], hd).reshape(depth, D_MODEL, QK_CHUNK)
    w_qk = [qk_chunk(0, HD_A)] + [qk_chunk(TILE_BQ * COL_TILE + k * QK_CHUNK, HD_B)
                                  for k in range(2 * B_COLS // QK_CHUNK)]
    order_a = jnp.asarray(_lane_order(HD_A))
    order_b = jnp.asarray(_lane_order(HD_B))
    wa_bf, wb_bf, wc_bf, wo_bf = (w.astype(BF16) for w in (w_br_a, w_br_b, w_br_c, w_out))
    wglu_bf, wup_bf, wdown_bf = w_glu.astype(BF16), w_up.astype(BF16), w_down.astype(BF16)

    w8, cd, l8 = jax.vmap(_ssm_operands)(ssm_a_re, ssm_a_im, ssm_log_dt, ssm_b_re, ssm_b_im, ssm_c_re, ssm_c_im)

    x2d = x.reshape(n, D_MODEL)
    for l in range(depth):
        lam_init = 0.8 - 0.6 * math.exp(-0.3 * l)
        ones = lambda w: jnp.ones((w,), F32)
        group_a = lambda g: jnp.tile(jnp.tile(g.astype(F32), LANES // HD_A)[order_a], A_Q_COLS // LANES)
        group_b = lambda g: jnp.tile(g.astype(F32)[order_b], B_COLS // LANES)
        col_gain = jnp.concatenate([
            group_a(qn_a[l]) * (math.log2(math.e) / math.sqrt(HD_A)), group_a(kn_a[l]), ones(WIDTH_A),
            group_b(qn_b[l]), group_b(kn_b[l]),
            ones(B_COLS + WIDTH_C + N_BRANCHES * D_MODEL)]).reshape(1, IN_COLS)
        col_bias = jnp.concatenate([jnp.zeros((IN_COLS - N_BRANCHES * D_MODEL,), F32),
                                    b_gate[l].astype(F32)]).reshape(1, IN_COLS)
        qk_a, v_t, *qkv_b, c_u, gates = _in_projection(
            x2d, attn_norm_g[l].reshape(1, D_MODEL).astype(F32), w_in_bf, w_qk, l, col_gain, col_bias, rope,
            bsz, s_len, tm_proj)

        lam_p = jnp.stack([lam_q1[l], lam_k1[l], lam_q2[l], lam_k2[l]]).astype(F32)
        oa = _diff_attention(qk_a.reshape(bsz, s_len, 2 * A_Q_COLS), v_t, lam_p,
                             subln_g[l].reshape(1, 2 * HD_A).astype(F32), lam_init, tq_a)
        oa = oa.reshape(n, WIDTH_A)

        ob_parts, lse_parts = zip(*[_dilated_attention(qkv_b[g], bsz, s_len, g, tq_b) for g in range(N_DIL)])

        oc = _ssm_branch(c_u, bsz, s_len, w8, cd, l8, ssm_d[l].reshape(1, WIDTH_C).astype(F32),
                         wglu_bf, l, b_glu[l].reshape(1, WIDTH_C).astype(F32), t_ssm)

        x2d = _merge_ffn(x2d, oa, ob_parts, lse_parts, oc, gates, wa_bf, wb_bf, wc_bf, wo_bf,
                         ffn_norm_g[l].reshape(1, D_MODEL).astype(F32), wup_bf,
                         conv_w[l].astype(F32), conv_b[l].reshape(1, D_FF).astype(F32), wdown_bf, l,
                         s_len, tm_ffn)
    return x2d.reshape(bsz, s_len, D_MODEL)
```

```python
import functools
import math

import jax
import jax.numpy as jnp
from jax import lax
from jax.experimental import pallas as pl
from jax.experimental.pallas import tpu as pltpu

F32 = jnp.float32
BF16 = jnp.bfloat16

LANES = 128
SUBLANES = 8
ROT_SHIFT = LANES // 2

D_MODEL = 1024
N_HEADS_A = 4
HD_A = 64
N_DIL = 3
DIL_PAIRS = ((128, 1), (512, 4), (2048, 16))
N_HEADS_B = 4
HD_B = 128
WIDTH_B = N_HEADS_B * HD_B
SSM_GROUP = 16
SSM_STATE = 64
WIDTH_C = 512
N_GROUPS_C = WIDTH_C // SSM_GROUP
N_STATE = N_GROUPS_C * SSM_STATE
N_BRANCHES = 3
D_FF = 2816
CONV_WIDTH = 3
ROPE_THETA = 500000.0
ROPE_FRACTION = 4
EPS = 1e-6

COL_TILE = 512
QK_CHUNK = 2 * COL_TILE
A_Q_COLS = 2 * N_HEADS_A * HD_A
WIDTH_A = N_HEADS_A * 2 * HD_A
B_COLS = N_DIL * N_HEADS_B * HD_B
IN_COLS = 2 * A_Q_COLS + WIDTH_A + 3 * B_COLS + WIDTH_C + N_BRANCHES * D_MODEL
N_COL_TILES = IN_COLS // COL_TILE
TILE_AQ, TILE_AK, TILE_AV = 0, 1, 2
TILE_BQ, TILE_BK, TILE_BV = 3, 6, 9
TILE_CU = 12

HEADS_PER_STEP = 4
ATTN_UNROLL = 2
VT_PAD = 16
VT_ROWS = 2 * HD_A + VT_PAD
LSE_LANES = LANES // N_HEADS_B
SSM_SLAB = 256
SSM_TAPS = SUBLANES
FFN_TILE = 256

VMEM_LIMIT = 56 * 1024 * 1024


def _cparams(sem):
    return pltpu.CompilerParams(dimension_semantics=sem, vmem_limit_bytes=VMEM_LIMIT)


def _first_head_lanes():
    lane = lax.broadcasted_iota(jnp.int32, (1, LANES), 1)
    return (lane // (ROT_SHIFT // 2)) % 2 == 0


def _norm_rope_tile(acc, gain, seg, c_ref, s_ref):
    cos = c_ref[...]
    sin = s_ref[...]
    first = _first_head_lanes()
    outs = []
    for gi in range(acc.shape[1] // LANES):
        y = acc[:, gi * LANES:(gi + 1) * LANES]
        ysq = y * y
        tot = jnp.sum(ysq, axis=-1, keepdims=True)
        if seg == LANES:
            ssum = tot
        else:
            one = jnp.sum(jnp.where(first, ysq, 0.0), axis=-1, keepdims=True)
            ssum = jnp.where(first, one, tot - one)
        yn = y * lax.rsqrt(ssum * (1.0 / seg) + EPS) * gain[:, gi * LANES:(gi + 1) * LANES]
        outs.append(yn * cos + pltpu.roll(yn, ROT_SHIFT, 1) * sin)
    return jnp.concatenate(outs, axis=1)


def _inproj_kernel(x_ref, g_ref, w_ref, wqa_ref, wqb0_ref, wqb1_ref, wqb2_ref, gain_ref, bias_ref,
                   ca_ref, sa_ref, cb_ref, sb_ref,
                   qk_ref, vt_ref, b0_ref, b1_ref, b2_ref, cu_ref, gate_ref, dil_scr, *, tm):
    x = x_ref[...]
    ms = jnp.mean(x * x, axis=-1, keepdims=True)
    h = (x * lax.rsqrt(ms + EPS) * g_ref[...]).astype(BF16)
    dil_refs = (b0_ref, b1_ref, b2_ref)
    heavy = [TILE_AQ, TILE_AK] + list(range(TILE_BQ, TILE_BV))
    light = [TILE_AV] + list(range(TILE_BV, TILE_CU + 1))
    order = heavy + [j for j in range(N_COL_TILES) if j not in heavy + light] + light
    wqb_refs = (wqb0_ref, wqb1_ref, wqb2_ref)

    def project(j):
        if j in (TILE_AQ, TILE_AK):
            ref, c0 = wqa_ref, j * COL_TILE
        elif TILE_BQ <= j < TILE_BV:
            off = (j - TILE_BQ) * COL_TILE
            ref, c0 = wqb_refs[off // QK_CHUNK], off % QK_CHUNK
        else:
            ref, c0 = w_ref, j * COL_TILE
        return jnp.dot(h, ref[:, c0:c0 + COL_TILE], preferred_element_type=F32)

    nxt = project(order[0])
    for idx, j in enumerate(order):
        cols = slice(j * COL_TILE, (j + 1) * COL_TILE)
        acc = nxt
        if idx + 1 < len(order):
            nxt = project(order[idx + 1])
        if j in (TILE_AQ, TILE_AK):
            qk_ref[:, cols] = _norm_rope_tile(acc, gain_ref[:, cols], HD_A, ca_ref, sa_ref).astype(BF16)
        elif j == TILE_AV:
            acc_t = acc.T.astype(BF16)
            for hd in range(N_HEADS_A):
                vt_ref[hd * VT_ROWS:hd * VT_ROWS + 2 * HD_A, :] = acc_t[hd * 2 * HD_A:(hd + 1) * 2 * HD_A, :]
                vt_ref[hd * VT_ROWS + 2 * HD_A:(hd + 1) * VT_ROWS, :] = jnp.ones((VT_PAD, tm), BF16)
        elif j < TILE_CU:
            part, group = divmod(j - TILE_BQ, N_DIL)
            if part < 2:
                acc = _norm_rope_tile(acc, gain_ref[:, cols], HD_B, cb_ref, sb_ref)
            dil = DIL_PAIRS[group][1]
            out_ref = dil_refs[group]
            if dil == 1:
                out_ref[:, part * WIDTH_B:(part + 1) * WIDTH_B] = acc.astype(BF16)
            else:
                slot = (group - 1) * 3 + part
                for gi in range(WIDTH_B // LANES):
                    dil_scr[slot, gi] = acc[:, gi * LANES:(gi + 1) * LANES]
                for r in range(dil):
                    for gi in range(WIDTH_B // LANES):
                        c0 = (r * 3 + part) * WIDTH_B + gi * LANES
                        out_ref[:, c0:c0 + LANES] = (
                            dil_scr[slot, gi, pl.ds(r, tm // dil, stride=dil), :].astype(BF16))
        elif j == TILE_CU:
            cu_ref[...] = acc
        else:
            g0 = (j - TILE_CU - 1) * COL_TILE
            gate_ref[:, g0:g0 + COL_TILE] = jax.nn.sigmoid(acc + bias_ref[:, cols]).astype(gate_ref.dtype)


def _in_projection(x2d, norm_g, w_bf, w_qk, layer, col_gain, col_bias, rope, bsz, s_len, tm):
    n = x2d.shape[0]
    tiles_per_seq = s_len // tm
    row = lambda i: (i, 0)
    const = lambda i: (0, 0)
    d1, d2 = DIL_PAIRS[1][1], DIL_PAIRS[2][1]
    kern = functools.partial(_inproj_kernel, tm=tm)
    resident = lambda cols: pl.BlockSpec((None, D_MODEL, cols), lambda i: (layer, 0, 0),
                                         pipeline_mode=pl.Buffered(1))
    return pl.pallas_call(
        kern,
        grid=(n // tm,),
        in_specs=[
            pl.BlockSpec((tm, D_MODEL), row),
            pl.BlockSpec((1, D_MODEL), const),
            resident(IN_COLS), resident(QK_CHUNK), resident(QK_CHUNK), resident(QK_CHUNK), resident(QK_CHUNK),
            pl.BlockSpec((1, IN_COLS), const),
            pl.BlockSpec((1, IN_COLS), const),
        ] + [pl.BlockSpec((tm, LANES), row)] * len(rope),
        out_specs=[
            pl.BlockSpec((tm, 2 * A_Q_COLS), row),
            pl.BlockSpec((None, N_HEADS_A * VT_ROWS, tm), lambda i: (i // tiles_per_seq, 0, i % tiles_per_seq)),
            pl.BlockSpec((tm, 3 * WIDTH_B), row),
            pl.BlockSpec((tm // d1, d1 * 3 * WIDTH_B), row),
            pl.BlockSpec((tm // d2, d2 * 3 * WIDTH_B), row),
            pl.BlockSpec((tm, WIDTH_C), row),
            pl.BlockSpec((tm, N_BRANCHES * D_MODEL), row),
        ],
        out_shape=[jax.ShapeDtypeStruct((n, 2 * A_Q_COLS), BF16),
                   jax.ShapeDtypeStruct((bsz, N_HEADS_A * VT_ROWS, s_len), BF16),
                   jax.ShapeDtypeStruct((n, 3 * WIDTH_B), BF16),
                   jax.ShapeDtypeStruct((n // d1, d1 * 3 * WIDTH_B), BF16),
                   jax.ShapeDtypeStruct((n // d2, d2 * 3 * WIDTH_B), BF16),
                   jax.ShapeDtypeStruct((n, WIDTH_C), F32),
                   jax.ShapeDtypeStruct((n, N_BRANCHES * D_MODEL), BF16)],
        scratch_shapes=[pltpu.VMEM((6, WIDTH_B // LANES, tm, LANES), F32)],
        compiler_params=_cparams(("parallel",)),
        name="in_projection",
    )(x2d, norm_g, w_bf, *w_qk, col_gain, col_bias, *rope)


def _diffattn_kernel(q_ref, k_ref, vt_ref, mask_ref, lam_ref, subg_ref, o_ref, acc_scr, s_scr, *, tq, lam_init):
    i = pl.program_id(2)
    q = q_ref[...]
    first = _first_head_lanes()
    nt = (((1,), (1,)), ((), ()))
    qm = []
    for hh in range(HEADS_PER_STEP):
        qh = q[:, hh * 2 * HD_A:(hh + 1) * 2 * HD_A]
        zero = jnp.zeros_like(qh)
        qm += [jnp.where(first, qh, zero), jnp.where(first, zero, qh)]
    n_maps = len(qm)
    acc_scr[...] = jnp.zeros_like(acc_scr)

    def score(j, slot, mi):
        r0 = pl.multiple_of(j * tq, tq)
        hh = mi // 2
        kblk = k_ref[pl.ds(r0, tq), hh * 2 * HD_A:(hh + 1) * 2 * HD_A]
        s_scr[slot, mi] = lax.dot_general(kblk, qm[mi], nt, preferred_element_type=F32)

    def consume_one(j, slot, mi, m_old, diagonal):
        r0 = pl.multiple_of(j * tq, tq)
        hh = mi // 2
        vtblk = vt_ref[hh * VT_ROWS:(hh + 1) * VT_ROWS, pl.ds(r0, tq)]
        st = s_scr[slot, mi]
        if diagonal:
            st = st + mask_ref[...]
        m_new = jnp.maximum(m_old, jnp.max(st, axis=0, keepdims=True))
        p = jnp.exp2(st - m_new)
        alpha = jnp.exp2(m_old - m_new)
        acc_scr[mi] = alpha * acc_scr[mi] + jnp.dot(vtblk, p.astype(BF16), preferred_element_type=F32)
        return m_new

    def scores(j, slot):
        for mi in range(n_maps):
            score(j, slot, mi)

    def consume(j, slot, ms, diagonal):
        return tuple(consume_one(j, slot, mi, ms[mi], diagonal) for mi in range(n_maps))

    def stage(j, src, ms):
        new = []
        for mi in range(n_maps):
            score(j + 1, 1 - src, mi)
            new.append(consume_one(j, src, mi, ms[mi], False))
        return tuple(new)

    init = tuple(jnp.full((1, tq), -jnp.inf, F32) for _ in range(n_maps))

    def stages(j0, count, ms):
        for u in range(count):
            ms = stage(j0 + u, u % 2, ms)
        return ms

    scores(0, 0)
    ms = lax.fori_loop(0, i // ATTN_UNROLL, lambda t, c: stages(ATTN_UNROLL * t, ATTN_UNROLL, c), init)
    rem = lax.rem(i, ATTN_UNROLL)
    for r in range(ATTN_UNROLL):
        @pl.when(rem == r)
        def _(r=r):
            consume(i, r % 2, stages(i - r, r, ms), True)

    lam_p = lam_ref[...]
    lam = (jnp.exp(jnp.sum(lam_p[0:1] * lam_p[1:2], axis=-1, keepdims=True))
           - jnp.exp(jnp.sum(lam_p[2:3] * lam_p[3:4], axis=-1, keepdims=True)) + lam_init)
    vals = slice(0, 2 * HD_A)
    den = slice(2 * HD_A, 2 * HD_A + 1)
    for hh in range(HEADS_PER_STEP):
        a1, a2 = acc_scr[2 * hh], acc_scr[2 * hh + 1]
        o_t = a1[vals] / a1[den] - lam * (a2[vals] / a2[den])
        msq = jnp.mean(o_t * o_t, axis=0, keepdims=True)
        o = (o_t * lax.rsqrt(msq + EPS)).T
        o_ref[:, hh * 2 * HD_A:(hh + 1) * 2 * HD_A] = ((o * subg_ref[...]) * (1.0 - lam_init)).astype(o_ref.dtype)


def _diff_attention(qk, v_t, lam_p, subln_g, lam_init, tq):
    bsz, s_len, _ = qk.shape
    width = HEADS_PER_STEP * 2 * HD_A
    kern = functools.partial(_diffattn_kernel, tq=tq, lam_init=lam_init)
    key = lax.broadcasted_iota(jnp.int32, (tq, tq), 0)
    qry = lax.broadcasted_iota(jnp.int32, (tq, tq), 1)
    causal = jnp.where(key <= qry, 0.0, -jnp.inf).astype(F32)
    return pl.pallas_call(
        kern,
        grid=(bsz, N_HEADS_A // HEADS_PER_STEP, s_len // tq),
        in_specs=[
            pl.BlockSpec((None, tq, width), lambda b, h, i: (b, i, h)),
            pl.BlockSpec((None, s_len, width), lambda b, h, i: (b, 0, N_HEADS_A // HEADS_PER_STEP + h),
                         pipeline_mode=pl.Buffered(1)),
            pl.BlockSpec((None, HEADS_PER_STEP * VT_ROWS, s_len), lambda b, h, i: (b, h, 0),
                         pipeline_mode=pl.Buffered(1)),
            pl.BlockSpec((tq, tq), lambda b, h, i: (0, 0)),
            pl.BlockSpec((4, HD_A), lambda b, h, i: (0, 0)),
            pl.BlockSpec((1, 2 * HD_A), lambda b, h, i: (0, 0)),
        ],
        out_specs=pl.BlockSpec((None, tq, width), lambda b, h, i: (b, i, h)),
        out_shape=jax.ShapeDtypeStruct((bsz, s_len, WIDTH_A), BF16),
        scratch_shapes=[pltpu.VMEM((2 * HEADS_PER_STEP, VT_ROWS, tq), F32),
                        pltpu.VMEM((2, 2 * HEADS_PER_STEP, tq, tq), F32)],
        compiler_params=_cparams(("parallel", "parallel", "arbitrary")),
        name="diff_attention",
    )(qk, qk, v_t, causal, lam_p, subln_g)


def _dilated_kernel(q_ref, k_ref, kp_ref, v_ref, vp_ref, o_ref, lse_ref, *, tq, blk):
    n = pl.program_id(2)
    scale = 1.0 / math.sqrt(HD_B)
    nt = (((1,), (1,)), ((), ()))
    rr = lax.broadcasted_iota(jnp.int32, (blk, 2 * blk), 0)
    cc = lax.broadcasted_iota(jnp.int32, (blk, 2 * blk), 1)
    band = jnp.logical_and(cc >= rr, cc <= rr + blk)
    lane = lax.broadcasted_iota(jnp.int32, (1, 2 * blk), 1)
    first_bias = jnp.where(lane >= blk, 0.0, jnp.where(n > 0, 0.0, -jnp.inf).astype(F32))

    def window(ref, pref, c, cols):
        if c == 0:
            return jnp.concatenate([pref[:, cols], ref[0:blk, cols]], axis=0)
        return ref[(c - 1) * blk:(c + 1) * blk, cols]

    def scores(unit):
        h, c = unit
        cols = slice(h * HD_B, (h + 1) * HD_B)
        qh = q_ref[c * blk:(c + 1) * blk, cols]
        s = lax.dot_general(qh, window(k_ref, kp_ref, c, cols), nt, preferred_element_type=F32) * scale
        s = jnp.where(band, s, -jnp.inf)
        return s + first_bias if c == 0 else s

    head_lane = lax.broadcasted_iota(jnp.int32, (1, LANES), 1) // LSE_LANES

    def finish(unit, s, lse_acc):
        h, c = unit
        cols = slice(h * HD_B, (h + 1) * HD_B)
        rows = slice(c * blk, (c + 1) * blk)
        m = jnp.max(s, axis=-1, keepdims=True)
        p = jnp.exp(s - m)
        den = jnp.sum(p, axis=-1, keepdims=True)
        pv = jnp.dot(p.astype(BF16), window(v_ref, vp_ref, c, cols), preferred_element_type=F32)
        o_ref[rows, cols] = (pv / den).astype(o_ref.dtype)
        lse = jnp.broadcast_to(m + jnp.log(den), (blk, LANES))
        lse_acc = lse if h == 0 else jnp.where(head_lane == h, lse, lse_acc)
        if h == N_HEADS_B - 1:
            lse_ref[rows, :] = lse_acc
        return lse_acc

    units = [(h, c) for c in range(tq // blk) for h in range(N_HEADS_B)]
    s = scores(units[0])
    lse_acc = None
    for idx, unit in enumerate(units):
        s_next = scores(units[idx + 1]) if idx + 1 < len(units) else None
        lse_acc = finish(unit, s, lse_acc)
        s = s_next


def _dilated_attention(qkv, bsz, s_len, group, tq):
    window, dil = DIL_PAIRS[group]
    blk = window // dil
    rows = s_len // dil
    tq = min(tq, rows)
    per_res = 3
    view = qkv.reshape(bsz, rows, dil * per_res * WIDTH_B)
    sub = tq // blk
    qcol = lambda r: r * per_res
    kcol = lambda r: r * per_res + 1
    vcol = lambda r: r * per_res + 2
    prev = lambda n: jnp.maximum(n * sub - 1, 0)
    kern = functools.partial(_dilated_kernel, tq=tq, blk=blk)
    o, lse = pl.pallas_call(
        kern,
        grid=(bsz, dil, rows // tq),
        in_specs=[
            pl.BlockSpec((None, tq, WIDTH_B), lambda b, r, n: (b, n, qcol(r))),
            pl.BlockSpec((None, tq, WIDTH_B), lambda b, r, n: (b, n, kcol(r))),
            pl.BlockSpec((None, blk, WIDTH_B), lambda b, r, n: (b, prev(n), kcol(r))),
            pl.BlockSpec((None, tq, WIDTH_B), lambda b, r, n: (b, n, vcol(r))),
            pl.BlockSpec((None, blk, WIDTH_B), lambda b, r, n: (b, prev(n), vcol(r))),
        ],
        out_specs=[pl.BlockSpec((None, tq, WIDTH_B), lambda b, r, n: (b, n, r)),
                   pl.BlockSpec((None, tq, LANES), lambda b, r, n: (b, n, r))],
        out_shape=[jax.ShapeDtypeStruct((bsz, rows, dil * WIDTH_B), BF16),
                   jax.ShapeDtypeStruct((bsz, rows, dil * LANES), F32)],
        compiler_params=_cparams(("parallel", "parallel", "arbitrary")),
        name=f"dilated_attention_g{group}",
    )(view, view, view, view, view)
    return o.reshape(bsz * rows, dil * WIDTH_B), lse.reshape(bsz * rows, dil * LANES)


def _ssm_kernel(u_ref, w8_ref, cd_ref, l8_ref, dskip_ref, wglu_ref, bglu_ref, o_ref,
                u_scr, x_scr, carry_scr, *, t_chunk):
    halo = SUBLANES

    @pl.when(pl.program_id(1) == 0)
    def _():
        carry_scr[...] = jnp.zeros_like(carry_scr)
        u_scr[0:halo, :] = jnp.zeros((halo, WIDTH_C), F32)

    @pl.when(pl.program_id(1) != 0)
    def _():
        u_scr[0:halo, :] = u_scr[t_chunk:t_chunk + halo, :]

    u = u_ref[...]
    u_scr[halo:halo + t_chunk, :] = u
    re = slice(0, N_STATE)
    im = slice(N_STATE, 2 * N_STATE)
    ch = WIDTH_C // 2
    sh = N_STATE // 2
    blk_ch = LANES // 2
    blk_st = blk_ch * SSM_STATE // SSM_GROUP
    low = lax.broadcasted_iota(jnp.int32, (1, LANES), 1) < blk_ch
    c_re, c_im = carry_scr[:, re], carry_scr[:, im]
    a8, b8 = l8_ref[:, re], l8_ref[:, im]
    n_slabs = t_chunk // SSM_SLAB
    slab_rows = lambda s: slice(s * SSM_SLAB, (s + 1) * SSM_SLAB)

    def drive(s):
        r0 = halo + s * SSM_SLAB
        for pair in range(WIDTH_C // LANES):
            lanes = slice(pair * LANES, (pair + 1) * LANES)
            taps = [u_scr[r0 - t:r0 - t + SSM_SLAB, lanes] for t in range(SSM_TAPS)]
            for half in range(2):
                cb = 2 * pair + half
                pieces = []
                for a in range(SSM_TAPS // 2):
                    even, odd = taps[2 * a], taps[2 * a + 1]
                    if half == 0:
                        v = jnp.where(low, even, pltpu.roll(odd, blk_ch, 1))
                    else:
                        v = jnp.where(low, pltpu.roll(even, blk_ch, 1), odd)
                    pieces.append(v.astype(BF16))
                w = jnp.dot(jnp.concatenate(pieces, axis=1), w8_ref[cb], preferred_element_type=F32)
                x_scr[slab_rows(s), cb * blk_st:(cb + 1) * blk_st] = w[:, :blk_st]
                x_scr[slab_rows(s), N_STATE + cb * blk_st:N_STATE + (cb + 1) * blk_st] = w[:, blk_st:]

    drive(0)
    if n_slabs > 1:
        drive(1)
    for s in range(n_slabs):
        rows = slab_rows(s)
        if s + 2 < n_slabs:
            drive(s + 2)
        for t in range(SSM_SLAB // SUBLANES):
            r8 = slice(s * SSM_SLAB + t * SUBLANES, s * SSM_SLAB + (t + 1) * SUBLANES)
            x_re = x_scr[r8, re] + (a8 * c_re - b8 * c_im)
            x_im = x_scr[r8, im] + (a8 * c_im + b8 * c_re)
            x_scr[r8, re] = x_re
            x_scr[r8, im] = x_im
            c_re, c_im = x_re, x_im
        halves = []
        for hh in range(2):
            acc = None
            for part in range(2):
                c0 = part * N_STATE + hh * sh
                d = jnp.dot(x_scr[rows, c0:c0 + sh].astype(BF16), cd_ref[c0:c0 + sh, hh * ch:(hh + 1) * ch],
                            preferred_element_type=F32)
                acc = d if acc is None else acc + d
            halves.append(acc)
        y = jnp.concatenate(halves, axis=1) + dskip_ref[...] * u[rows]
        cdf = 0.5 * (1.0 + jnp.tanh(math.sqrt(2.0 / math.pi) * (y + 0.044715 * (y * y * y))))
        z = y * cdf
        gate = jnp.dot(z.astype(BF16), wglu_ref[...], preferred_element_type=F32) + bglu_ref[...]
        o_ref[rows, :] = (z * jax.nn.sigmoid(gate)).astype(o_ref.dtype)
    carry_scr[:, re] = c_re
    carry_scr[:, im] = c_im


def _ssm_operands(a_re, a_im, log_dt, b_re, b_im, c_re, c_im):
    a_re, a_im = a_re.astype(F32), a_im.astype(F32)
    dt = jnp.exp(log_dt.astype(F32))[:, None]
    mag = jnp.exp(a_re * dt)
    lb_re, lb_im = mag * jnp.cos(a_im * dt), mag * jnp.sin(a_im * dt)
    n_re, n_im = lb_re - 1.0, lb_im
    den = a_re * a_re + a_im * a_im
    f_re = (n_re * a_re + n_im * a_im) / den
    f_im = (n_im * a_re - n_re * a_im) / den
    b_re, b_im = b_re.astype(F32), b_im.astype(F32)
    bb_re = f_re[..., None] * b_re - f_im[..., None] * b_im
    bb_im = f_re[..., None] * b_im + f_im[..., None] * b_re
    eye = jnp.eye(N_GROUPS_C, dtype=F32)
    blockdiag_out = lambda t: jnp.einsum('gcp,gh->gphc', t, eye).reshape(N_STATE, WIDTH_C)
    cd = jnp.concatenate([blockdiag_out(c_re.astype(F32)), -blockdiag_out(c_im.astype(F32))],
                         axis=0).astype(BF16)

    def power(k):
        return (mag ** k) * jnp.cos(a_im * dt * k), (mag ** k) * jnp.sin(a_im * dt * k)

    taps = []
    for s in range(SSM_TAPS):
        p_re, p_im = power(float(s))
        taps.append(jnp.stack([p_re[..., None] * bb_re - p_im[..., None] * bb_im,
                               p_re[..., None] * bb_im + p_im[..., None] * bb_re], axis=-1))
    grp = (LANES // 2) // SSM_GROUP
    n_blk = N_GROUPS_C // grp
    v = jnp.stack(taps, axis=0).reshape(SSM_TAPS, n_blk, grp, SSM_STATE, SSM_GROUP, 2)
    w8 = jnp.einsum('sbgpcq,gh->bsgcqhp', v, jnp.eye(grp, dtype=F32))
    w8 = w8.reshape(n_blk, SSM_TAPS * grp * SSM_GROUP, 2 * grp * SSM_STATE).astype(BF16)
    p_re, p_im = power(float(SSM_TAPS))
    l8 = jnp.concatenate([p_re.reshape(1, N_STATE), p_im.reshape(1, N_STATE)], axis=1)
    return w8, cd, jnp.broadcast_to(l8, (SUBLANES, 2 * N_STATE))


def _ssm_branch(c_u, bsz, s_len, w8, cd, l8, d_skip, w_glu_bf, layer, b_glu, t_chunk):
    n_chunks = s_len // t_chunk
    const2 = lambda b, c: (0, 0)
    kern = functools.partial(_ssm_kernel, t_chunk=t_chunk)
    return pl.pallas_call(
        kern,
        grid=(bsz, n_chunks),
        in_specs=[
            pl.BlockSpec((t_chunk, WIDTH_C), lambda b, c: (b * n_chunks + c, 0)),
            pl.BlockSpec((None,) + w8.shape[1:], lambda b, c: (layer, 0, 0, 0)),
            pl.BlockSpec((None, 2 * N_STATE, WIDTH_C), lambda b, c: (layer, 0, 0)),
            pl.BlockSpec((None, SUBLANES, 2 * N_STATE), lambda b, c: (layer, 0, 0)),
            pl.BlockSpec((1, WIDTH_C), const2),
            pl.BlockSpec((None, WIDTH_C, WIDTH_C), lambda b, c: (layer, 0, 0)),
            pl.BlockSpec((1, WIDTH_C), const2),
        ],
        out_specs=pl.BlockSpec((t_chunk, WIDTH_C), lambda b, c: (b * n_chunks + c, 0)),
        out_shape=jax.ShapeDtypeStruct((bsz * s_len, WIDTH_C), BF16),
        scratch_shapes=[pltpu.VMEM((t_chunk + SUBLANES, WIDTH_C), F32),
                        pltpu.VMEM((t_chunk, 2 * N_STATE), F32),
                        pltpu.VMEM((SUBLANES, 2 * N_STATE), F32)],
        compiler_params=_cparams(("parallel", "arbitrary")),
        name="s5_scan_glu",
    )(c_u, w8, cd, l8, d_skip, w_glu_bf, b_glu)


def _merge_body(x_ref, oa_ref, o0_ref, o1_ref, o2_ref, l0_ref, l1_ref, l2_ref, oc_ref,
                g0_ref, g1_ref, g2_ref, wa_ref, wb_ref, wc_ref, wo_ref, tok_scr, lse_scr, tm):
    for gi, (o_ref, l_ref) in enumerate(((o1_ref, l1_ref), (o2_ref, l2_ref))):
        dil = DIL_PAIRS[gi + 1][1]
        for r in range(dil):
            lse_scr[gi, pl.ds(r, tm // dil, stride=dil), :] = l_ref[:, r * LANES:(r + 1) * LANES]
            for li in range(WIDTH_B // LANES):
                cols = slice(r * WIDTH_B + li * LANES, r * WIDTH_B + (li + 1) * LANES)
                tok_scr[gi, li, pl.ds(r, tm // dil, stride=dil), :] = o_ref[:, cols].astype(F32)
    tok = lambda k: jnp.concatenate([tok_scr[k, li] for li in range(WIDTH_B // LANES)], axis=1)
    l0, l1, l2 = l0_ref[...], lse_scr[0], lse_scr[1]
    m = jnp.maximum(jnp.maximum(l0, l1), l2)
    e0, e1, e2 = jnp.exp(l0 - m), jnp.exp(l1 - m), jnp.exp(l2 - m)
    tot = e0 + e1 + e2
    sel_r = lax.broadcasted_iota(jnp.int32, (LANES, WIDTH_B), 0)
    sel_c = lax.broadcasted_iota(jnp.int32, (LANES, WIDTH_B), 1)
    select = jnp.where(sel_r == LSE_LANES * (sel_c // HD_B), 1.0, 0.0).astype(BF16)

    def spread(w):
        hi = w.astype(BF16)
        lo = (w - hi.astype(F32)).astype(BF16)
        return (jnp.dot(hi, select, preferred_element_type=F32)
                + jnp.dot(lo, select, preferred_element_type=F32))

    ob = (spread(e0 / tot) * o0_ref[...].astype(F32) + spread(e1 / tot) * tok(0)
          + spread(e2 / tot) * tok(1))
    ya = jnp.dot(oa_ref[...], wa_ref[...], preferred_element_type=F32)
    yb = jnp.dot(ob.astype(BF16), wb_ref[...], preferred_element_type=F32)
    yc = jnp.dot(oc_ref[...], wc_ref[...], preferred_element_type=F32)
    merged = (g0_ref[...].astype(F32) * ya + g1_ref[...].astype(F32) * yb
              + g2_ref[...].astype(F32) * yc)
    return x_ref[...] + jnp.dot(merged.astype(BF16), wo_ref[...], preferred_element_type=F32)


def _ffn_conv_halo(a_scr, tm, tiles_per_seq):
    halo = SUBLANES

    @pl.when(pl.program_id(0) % tiles_per_seq == 0)
    def _():
        a_scr[0:halo, :] = jnp.zeros((halo, D_FF), F32)

    @pl.when(pl.program_id(0) % tiles_per_seq != 0)
    def _():
        a_scr[0:halo, :] = a_scr[tm:tm + halo, :]


def _ffn_body(x, g_ref, wup_ref, cw_ref, cb_ref, wdown_ref, o_ref, a_scr, tm):
    halo = SUBLANES
    ms = jnp.mean(x * x, axis=-1, keepdims=True)
    h = (x * lax.rsqrt(ms + EPS) * g_ref[...]).astype(BF16)

    n_tiles = D_FF // FFN_TILE
    tile_cols = lambda f: slice(f * FFN_TILE, (f + 1) * FFN_TILE)

    def up(f):
        a_scr[halo:halo + tm, tile_cols(f)] = jnp.dot(h, wup_ref[:, tile_cols(f)], preferred_element_type=F32)
        return jnp.dot(h, wup_ref[:, D_FF + f * FFN_TILE:D_FF + (f + 1) * FFN_TILE],
                       preferred_element_type=F32)

    y = x
    gate = up(0)
    for f in range(n_tiles):
        cols = tile_cols(f)
        next_gate = up(f + 1) if f + 1 < n_tiles else None
        conv = (cb_ref[:, cols] + cw_ref[0:1, cols] * a_scr[halo - 2:halo - 2 + tm, cols]
                + cw_ref[1:2, cols] * a_scr[halo - 1:halo - 1 + tm, cols]
                + cw_ref[2:3, cols] * a_scr[halo:halo + tm, cols])
        act = (conv * jax.nn.sigmoid(conv)) * gate
        y = y + jnp.dot(act.astype(BF16), wdown_ref[cols, :], preferred_element_type=F32)
        gate = next_gate
    o_ref[...] = y


N_MERGE_INPUTS = 16


def _merge_ffn_kernel(*refs, tm, tiles_per_seq):
    merge_in = refs[:N_MERGE_INPUTS]
    g_ref, wup_ref, cw_ref, cb_ref, wdown_ref, o_ref, tok_scr, lse_scr, a_scr = refs[N_MERGE_INPUTS:]
    _ffn_conv_halo(a_scr, tm, tiles_per_seq)
    x_mid = _merge_body(*merge_in, tok_scr, lse_scr, tm)
    _ffn_body(x_mid, g_ref, wup_ref, cw_ref, cb_ref, wdown_ref, o_ref, a_scr, tm)


def _merge_ffn(x2d, oa, ob_parts, lse_parts, oc, gates, wa, wb, wc, wo,
               norm_g, wup_bf, conv_w, conv_b, wdown_bf, layer, s_len, tm):
    n = x2d.shape[0]
    row = lambda i: (i, 0)
    const = lambda i: (0, 0)
    half = pl.BlockSpec((tm, WIDTH_C), row)
    full = pl.BlockSpec((tm, D_MODEL), row)
    single = pl.Buffered(1)
    wspec_half = pl.BlockSpec((None, WIDTH_C, D_MODEL), lambda i: (layer, 0, 0), pipeline_mode=single)
    gspec = lambda k: pl.BlockSpec((tm, D_MODEL), lambda i: (i, k))
    dilated = lambda g, width: pl.BlockSpec((tm // DIL_PAIRS[g][1], DIL_PAIRS[g][1] * width), row)
    merge_specs = [full, half, half, dilated(1, WIDTH_B), dilated(2, WIDTH_B),
                   pl.BlockSpec((tm, LANES), row), dilated(1, LANES), dilated(2, LANES), half,
                   gspec(0), gspec(1), gspec(2),
                   wspec_half, wspec_half, wspec_half,
                   pl.BlockSpec((None, D_MODEL, D_MODEL), lambda i: (layer, 0, 0), pipeline_mode=single)]
    assert len(merge_specs) == N_MERGE_INPUTS
    ffn_specs = [
        pl.BlockSpec((1, D_MODEL), const),
        pl.BlockSpec((None, D_MODEL, 2 * D_FF), lambda i: (layer, 0, 0), pipeline_mode=single),
        pl.BlockSpec((CONV_WIDTH, D_FF), const),
        pl.BlockSpec((1, D_FF), const),
        pl.BlockSpec((None, D_FF, D_MODEL), lambda i: (layer, 0, 0), pipeline_mode=single),
    ]
    kern = functools.partial(_merge_ffn_kernel, tm=tm, tiles_per_seq=s_len // tm)
    return pl.pallas_call(
        kern,
        grid=(n // tm,),
        in_specs=merge_specs + ffn_specs,
        out_specs=full,
        out_shape=jax.ShapeDtypeStruct((n, D_MODEL), F32),
        scratch_shapes=[pltpu.VMEM((2, WIDTH_B // LANES, tm, LANES), F32),
                        pltpu.VMEM((2, tm, LANES), F32),
                        pltpu.VMEM((tm + SUBLANES, D_FF), F32)],
        compiler_params=_cparams(("arbitrary",)),
        name="merge_conv_ffn",
    )(x2d, oa, *ob_parts, *lse_parts, oc, gates, gates, gates, wa, wb, wc, wo,
      norm_g, wup_bf, conv_w, conv_b, wdown_bf)


def _lane_order(head_dim):
    half = head_dim // ROPE_FRACTION // 2
    heads = LANES // head_dim
    quarter = ROT_SHIFT // heads
    first, second = [], []
    for hd in range(heads):
        base = hd * head_dim
        first += list(range(base, base + half)) + list(range(base + 2 * half, base + half + quarter))
        second += list(range(base + half, base + 2 * half)) + list(range(base + half + quarter, base + head_dim))
    return first + second


def _permute_groups(w_cols, head_dim):
    order = _lane_order(head_dim)
    groups = w_cols.shape[-1] // LANES
    src = jnp.asarray([g * LANES + o for g in range(groups) for o in order])
    perm = jnp.zeros((groups * LANES,) * 2, F32).at[src, jnp.arange(groups * LANES)].set(1.0).astype(w_cols.dtype)
    return jnp.dot(w_cols, perm, preferred_element_type=F32).astype(w_cols.dtype)


def _rope_tables(positions, head_dim):
    rot = head_dim // ROPE_FRACTION
    half = rot // 2
    heads = LANES // head_dim
    quarter = ROT_SHIFT // heads
    inv = ROPE_THETA ** (-jnp.arange(0, rot, 2, dtype=F32) / rot)
    ang = positions.reshape(-1).astype(F32)[:, None] * inv
    cos, sin = jnp.cos(ang), jnp.sin(ang)
    n = ang.shape[0]
    pad = quarter - half
    c_q = jnp.concatenate([cos, jnp.ones((n, pad), F32)], axis=1)
    s_q = jnp.concatenate([sin, jnp.zeros((n, pad), F32)], axis=1)
    c = jnp.tile(c_q, (1, 2 * heads))
    s = jnp.concatenate([jnp.tile(-s_q, (1, heads)), jnp.tile(s_q, (1, heads))], axis=1)
    return c, s


def kernel(x, positions, attn_norm_g, w_in, b_gate, qn_a, kn_a, lam_q1, lam_k1, lam_q2, lam_k2, subln_g, w_br_a, qn_b, kn_b, w_br_b, ssm_a_re, ssm_a_im, ssm_log_dt, ssm_b_re, ssm_b_im, ssm_c_re, ssm_c_im, ssm_d, w_glu, b_glu, w_br_c, w_out, ffn_norm_g, w_up, conv_w, conv_b, w_down):
    bsz, s_len, d_model = x.shape
    depth = w_in.shape[0]
    assert d_model == D_MODEL and w_in.shape[2] == IN_COLS
    assert s_len % DIL_PAIRS[-1][0] == 0, "sequence must be a multiple of the largest dilated window"
    n = bsz * s_len

    tm_proj = 256
    tq_a = min(512, s_len)
    tq_b = 1024
    t_ssm = min(1024, s_len)
    tm_ffn = min(256, s_len)

    rope = _rope_tables(positions, HD_A) + _rope_tables(positions, HD_B)
    w_in_bf = w_in.astype(BF16)
    w_rows = w_in_bf.reshape(depth * D_MODEL, IN_COLS)
    qk_chunk = lambda c0, hd: _permute_groups(w_rows[:, c0:c0 + QK_CHUNK], hd).reshape(depth, D_MODEL, QK_CHUNK)
    w_qk = [qk_chunk(0, HD_A)] + [qk_chunk(TILE_BQ * COL_TILE + k * QK_CHUNK, HD_B)
                                  for k in range(2 * B_COLS // QK_CHUNK)]
    order_a = jnp.asarray(_lane_order(HD_A))
    order_b = jnp.asarray(_lane_order(HD_B))
    wa_bf, wb_bf, wc_bf, wo_bf = (w.astype(BF16) for w in (w_br_a, w_br_b, w_br_c, w_out))
    wglu_bf, wup_bf, wdown_bf = w_glu.astype(BF16), w_up.astype(BF16), w_down.astype(BF16)

    w8, cd, l8 = jax.vmap(_ssm_operands)(ssm_a_re, ssm_a_im, ssm_log_dt, ssm_b_re, ssm_b_im, ssm_c_re, ssm_c_im)

    x2d = x.reshape(n, D_MODEL)
    for l in range(depth):
        lam_init = 0.8 - 0.6 * math.exp(-0.3 * l)
        ones = lambda w: jnp.ones((w,), F32)
        group_a = lambda g: jnp.tile(jnp.tile(g.astype(F32), LANES // HD_A)[order_a], A_Q_COLS // LANES)
        group_b = lambda g: jnp.tile(g.astype(F32)[order_b], B_COLS // LANES)
        col_gain = jnp.concatenate([
            group_a(qn_a[l]) * (math.log2(math.e) / math.sqrt(HD_A)), group_a(kn_a[l]), ones(WIDTH_A),
            group_b(qn_b[l]), group_b(kn_b[l]),
            ones(B_COLS + WIDTH_C + N_BRANCHES * D_MODEL)]).reshape(1, IN_COLS)
        col_bias = jnp.concatenate([jnp.zeros((IN_COLS - N_BRANCHES * D_MODEL,), F32),
                                    b_gate[l].astype(F32)]).reshape(1, IN_COLS)
        qk_a, v_t, *qkv_b, c_u, gates = _in_projection(
            x2d, attn_norm_g[l].reshape(1, D_MODEL).astype(F32), w_in_bf, w_qk, l, col_gain, col_bias, rope,
            bsz, s_len, tm_proj)

        lam_p = jnp.stack([lam_q1[l], lam_k1[l], lam_q2[l], lam_k2[l]]).astype(F32)
        oa = _diff_attention(qk_a.reshape(bsz, s_len, 2 * A_Q_COLS), v_t, lam_p,
                             subln_g[l].reshape(1, 2 * HD_A).astype(F32), lam_init, tq_a)
        oa = oa.reshape(n, WIDTH_A)

        ob_parts, lse_parts = zip(*[_dilated_attention(qkv_b[g], bsz, s_len, g, tq_b) for g in range(N_DIL)])

        oc = _ssm_branch(c_u, bsz, s_len, w8, cd, l8, ssm_d[l].reshape(1, WIDTH_C).astype(F32),
                         wglu_bf, l, b_glu[l].reshape(1, WIDTH_C).astype(F32), t_ssm)

        x2d = _merge_ffn(x2d, oa, ob_parts, lse_parts, oc, gates, wa_bf, wb_bf, wc_bf, wo_bf,
                         ffn_norm_g[l].reshape(1, D_MODEL).astype(F32), wup_bf,
                         conv_w[l].astype(F32), conv_b[l].reshape(1, D_FF).astype(F32), wdown_bf, l,
                         s_len, tm_ffn)
    return x2d.reshape(bsz, s_len, D_MODEL)
```

```python
import functools
import math

import jax
import jax.numpy as jnp
from jax import lax
from jax.experimental import pallas as pl
from jax.experimental.pallas import tpu as pltpu

F32 = jnp.float32
BF16 = jnp.bfloat16

LANES = 128
SUBLANES = 8
ROT_SHIFT = LANES // 2

D_MODEL = 1024
N_HEADS_A = 4
HD_A = 64
N_DIL = 3
DIL_PAIRS = ((128, 1), (512, 4), (2048, 16))
N_HEADS_B = 4
HD_B = 128
WIDTH_B = N_HEADS_B * HD_B
SSM_GROUP = 16
SSM_STATE = 64
WIDTH_C = 512
N_GROUPS_C = WIDTH_C // SSM_GROUP
N_STATE = N_GROUPS_C * SSM_STATE
N_BRANCHES = 3
D_FF = 2816
CONV_WIDTH = 3
ROPE_THETA = 500000.0
ROPE_FRACTION = 4
EPS = 1e-6

COL_TILE = 512
QK_CHUNK = 2 * COL_TILE
A_Q_COLS = 2 * N_HEADS_A * HD_A
WIDTH_A = N_HEADS_A * 2 * HD_A
B_COLS = N_DIL * N_HEADS_B * HD_B
IN_COLS = 2 * A_Q_COLS + WIDTH_A + 3 * B_COLS + WIDTH_C + N_BRANCHES * D_MODEL
N_COL_TILES = IN_COLS // COL_TILE
TILE_AQ, TILE_AK, TILE_AV = 0, 1, 2
TILE_BQ, TILE_BK, TILE_BV = 3, 6, 9
TILE_CU = 12

HEADS_PER_STEP = 4
N_SCORE_BUF = 4
AHEAD = N_SCORE_BUF - 1
VT_PAD = 16
VT_ROWS = 2 * HD_A + VT_PAD
LSE_LANES = LANES // N_HEADS_B
SSM_SLAB = 256
SSM_TAPS = SUBLANES
FFN_TILE = 256

VMEM_LIMIT = 56 * 1024 * 1024


def _cparams(sem):
    return pltpu.CompilerParams(dimension_semantics=sem, vmem_limit_bytes=VMEM_LIMIT)


def _first_head_lanes():
    lane = lax.broadcasted_iota(jnp.int32, (1, LANES), 1)
    return (lane // (ROT_SHIFT // 2)) % 2 == 0


def _norm_rope_tile(acc, gain, seg, c_ref, s_ref):
    cos = c_ref[...]
    sin = s_ref[...]
    first = _first_head_lanes()
    outs = []
    for gi in range(acc.shape[1] // LANES):
        y = acc[:, gi * LANES:(gi + 1) * LANES]
        ysq = y * y
        tot = jnp.sum(ysq, axis=-1, keepdims=True)
        if seg == LANES:
            ssum = tot
        else:
            one = jnp.sum(jnp.where(first, ysq, 0.0), axis=-1, keepdims=True)
            ssum = jnp.where(first, one, tot - one)
        yn = y * lax.rsqrt(ssum * (1.0 / seg) + EPS) * gain[:, gi * LANES:(gi + 1) * LANES]
        outs.append(yn * cos + pltpu.roll(yn, ROT_SHIFT, 1) * sin)
    return jnp.concatenate(outs, axis=1)


def _inproj_kernel(x_ref, g_ref, w_ref, wqa_ref, wqb0_ref, wqb1_ref, wqb2_ref, gain_ref, bias_ref,
                   ca_ref, sa_ref, cb_ref, sb_ref,
                   qk_ref, vt_ref, b0_ref, b1_ref, b2_ref, cu_ref, gate_ref, dil_scr, *, tm):
    x = x_ref[...]
    ms = jnp.mean(x * x, axis=-1, keepdims=True)
    h = (x * lax.rsqrt(ms + EPS) * g_ref[...]).astype(BF16)
    dil_refs = (b0_ref, b1_ref, b2_ref)
    heavy = [TILE_AQ, TILE_AK] + list(range(TILE_BQ, TILE_BV))
    light = [TILE_AV] + list(range(TILE_BV, TILE_CU + 1))
    order = heavy + [j for j in range(N_COL_TILES) if j not in heavy + light] + light
    wqb_refs = (wqb0_ref, wqb1_ref, wqb2_ref)

    def project(j):
        if j in (TILE_AQ, TILE_AK):
            ref, c0 = wqa_ref, j * COL_TILE
        elif TILE_BQ <= j < TILE_BV:
            off = (j - TILE_BQ) * COL_TILE
            ref, c0 = wqb_refs[off // QK_CHUNK], off % QK_CHUNK
        else:
            ref, c0 = w_ref, j * COL_TILE
        return jnp.dot(h, ref[:, c0:c0 + COL_TILE], preferred_element_type=F32)

    nxt = project(order[0])
    for idx, j in enumerate(order):
        cols = slice(j * COL_TILE, (j + 1) * COL_TILE)
        acc = nxt
        if idx + 1 < len(order):
            nxt = project(order[idx + 1])
        if j in (TILE_AQ, TILE_AK):
            qk_ref[:, cols] = _norm_rope_tile(acc, gain_ref[:, cols], HD_A, ca_ref, sa_ref).astype(BF16)
        elif j == TILE_AV:
            acc_t = acc.T.astype(BF16)
            for hd in range(N_HEADS_A):
                vt_ref[hd * VT_ROWS:hd * VT_ROWS + 2 * HD_A, :] = acc_t[hd * 2 * HD_A:(hd + 1) * 2 * HD_A, :]
                vt_ref[hd * VT_ROWS + 2 * HD_A:(hd + 1) * VT_ROWS, :] = jnp.ones((VT_PAD, tm), BF16)
        elif j < TILE_CU:
            part, group = divmod(j - TILE_BQ, N_DIL)
            if part < 2:
                acc = _norm_rope_tile(acc, gain_ref[:, cols], HD_B, cb_ref, sb_ref)
            dil = DIL_PAIRS[group][1]
            out_ref = dil_refs[group]
            if dil == 1:
                out_ref[:, part * WIDTH_B:(part + 1) * WIDTH_B] = acc.astype(BF16)
            else:
                slot = (group - 1) * 3 + part
                for gi in range(WIDTH_B // LANES):
                    dil_scr[slot, gi] = acc[:, gi * LANES:(gi + 1) * LANES]
                for r in range(dil):
                    for gi in range(WIDTH_B // LANES):
                        c0 = (r * 3 + part) * WIDTH_B + gi * LANES
                        out_ref[:, c0:c0 + LANES] = (
                            dil_scr[slot, gi, pl.ds(r, tm // dil, stride=dil), :].astype(BF16))
        elif j == TILE_CU:
            cu_ref[...] = acc
        else:
            g0 = (j - TILE_CU - 1) * COL_TILE
            gate_ref[:, g0:g0 + COL_TILE] = jax.nn.sigmoid(acc + bias_ref[:, cols]).astype(gate_ref.dtype)


def _in_projection(x2d, norm_g, w_bf, w_qk, layer, col_gain, col_bias, rope, bsz, s_len, tm):
    n = x2d.shape[0]
    tiles_per_seq = s_len // tm
    row = lambda i: (i, 0)
    const = lambda i: (0, 0)
    d1, d2 = DIL_PAIRS[1][1], DIL_PAIRS[2][1]
    kern = functools.partial(_inproj_kernel, tm=tm)
    resident = lambda cols: pl.BlockSpec((None, D_MODEL, cols), lambda i: (layer, 0, 0),
                                         pipeline_mode=pl.Buffered(1))
    return pl.pallas_call(
        kern,
        grid=(n // tm,),
        in_specs=[
            pl.BlockSpec((tm, D_MODEL), row),
            pl.BlockSpec((1, D_MODEL), const),
            resident(IN_COLS), resident(QK_CHUNK), resident(QK_CHUNK), resident(QK_CHUNK), resident(QK_CHUNK),
            pl.BlockSpec((1, IN_COLS), const),
            pl.BlockSpec((1, IN_COLS), const),
        ] + [pl.BlockSpec((tm, LANES), row)] * len(rope),
        out_specs=[
            pl.BlockSpec((tm, 2 * A_Q_COLS), row),
            pl.BlockSpec((None, N_HEADS_A * VT_ROWS, tm), lambda i: (i // tiles_per_seq, 0, i % tiles_per_seq)),
            pl.BlockSpec((tm, 3 * WIDTH_B), row),
            pl.BlockSpec((tm // d1, d1 * 3 * WIDTH_B), row),
            pl.BlockSpec((tm // d2, d2 * 3 * WIDTH_B), row),
            pl.BlockSpec((tm, WIDTH_C), row),
            pl.BlockSpec((tm, N_BRANCHES * D_MODEL), row),
        ],
        out_shape=[jax.ShapeDtypeStruct((n, 2 * A_Q_COLS), BF16),
                   jax.ShapeDtypeStruct((bsz, N_HEADS_A * VT_ROWS, s_len), BF16),
                   jax.ShapeDtypeStruct((n, 3 * WIDTH_B), BF16),
                   jax.ShapeDtypeStruct((n // d1, d1 * 3 * WIDTH_B), BF16),
                   jax.ShapeDtypeStruct((n // d2, d2 * 3 * WIDTH_B), BF16),
                   jax.ShapeDtypeStruct((n, WIDTH_C), F32),
                   jax.ShapeDtypeStruct((n, N_BRANCHES * D_MODEL), BF16)],
        scratch_shapes=[pltpu.VMEM((6, WIDTH_B // LANES, tm, LANES), F32)],
        compiler_params=_cparams(("parallel",)),
        name="in_projection",
    )(x2d, norm_g, w_bf, *w_qk, col_gain, col_bias, *rope)


def _diffattn_kernel(q_ref, k_ref, vt_ref, mask_ref, lam_ref, subg_ref, o_ref, acc_scr, s_scr, *, tq, lam_init):
    i = pl.program_id(2)
    q = q_ref[...]
    first = _first_head_lanes()
    nt = (((1,), (1,)), ((), ()))
    qm = []
    for hh in range(HEADS_PER_STEP):
        qh = q[:, hh * 2 * HD_A:(hh + 1) * 2 * HD_A]
        zero = jnp.zeros_like(qh)
        qm += [jnp.where(first, qh, zero), jnp.where(first, zero, qh)]
    n_maps = len(qm)
    acc_scr[...] = jnp.zeros_like(acc_scr)

    def score(j, mi):
        r0 = pl.multiple_of(j * tq, tq)
        hh = mi // 2
        kblk = k_ref[pl.ds(r0, tq), hh * 2 * HD_A:(hh + 1) * 2 * HD_A]
        s_scr[mi % N_SCORE_BUF] = lax.dot_general(kblk, qm[mi], nt, preferred_element_type=F32)

    def consume(j, mi, m_old, diagonal):
        r0 = pl.multiple_of(j * tq, tq)
        hh = mi // 2
        vtblk = vt_ref[hh * VT_ROWS:(hh + 1) * VT_ROWS, pl.ds(r0, tq)]
        st = s_scr[mi % N_SCORE_BUF]
        if diagonal:
            st = st + mask_ref[...]
        m_new = jnp.maximum(m_old, jnp.max(st, axis=0, keepdims=True))
        p = jnp.exp2(st - m_new)
        alpha = jnp.exp2(m_old - m_new)
        acc_scr[mi] = alpha * acc_scr[mi] + jnp.dot(vtblk, p.astype(BF16), preferred_element_type=F32)
        return m_new

    def stage(j, ms, diagonal):
        new = []
        for mi in range(n_maps):
            ahead = mi + AHEAD
            if ahead < n_maps:
                score(j, ahead)
            elif not diagonal:
                score(j + 1, ahead - n_maps)
            new.append(consume(j, mi, ms[mi], diagonal))
        return tuple(new)

    init = tuple(jnp.full((1, tq), -jnp.inf, F32) for _ in range(n_maps))

    for mi in range(AHEAD):
        score(0, mi)
    ms = lax.fori_loop(0, i // 2, lambda t, c: stage(2 * t + 1, stage(2 * t, c, False), False), init)
    odd = lax.rem(i, 2) == 1

    @pl.when(odd)
    def _():
        stage(i, stage(i - 1, ms, False), True)

    @pl.when(jnp.logical_not(odd))
    def _():
        stage(i, ms, True)

    lam_p = lam_ref[...]
    lam = (jnp.exp(jnp.sum(lam_p[0:1] * lam_p[1:2], axis=-1, keepdims=True))
           - jnp.exp(jnp.sum(lam_p[2:3] * lam_p[3:4], axis=-1, keepdims=True)) + lam_init)
    vals = slice(0, 2 * HD_A)
    den = slice(2 * HD_A, 2 * HD_A + 1)
    for hh in range(HEADS_PER_STEP):
        a1, a2 = acc_scr[2 * hh], acc_scr[2 * hh + 1]
        o_t = a1[vals] / a1[den] - lam * (a2[vals] / a2[den])
        msq = jnp.mean(o_t * o_t, axis=0, keepdims=True)
        o = (o_t * lax.rsqrt(msq + EPS)).T
        o_ref[:, hh * 2 * HD_A:(hh + 1) * 2 * HD_A] = ((o * subg_ref[...]) * (1.0 - lam_init)).astype(o_ref.dtype)


def _diff_attention(qk, v_t, lam_p, subln_g, lam_init, tq):
    bsz, s_len, _ = qk.shape
    width = HEADS_PER_STEP * 2 * HD_A
    kern = functools.partial(_diffattn_kernel, tq=tq, lam_init=lam_init)
    key = lax.broadcasted_iota(jnp.int32, (tq, tq), 0)
    qry = lax.broadcasted_iota(jnp.int32, (tq, tq), 1)
    causal = jnp.where(key <= qry, 0.0, -jnp.inf).astype(F32)
    return pl.pallas_call(
        kern,
        grid=(bsz, N_HEADS_A // HEADS_PER_STEP, s_len // tq),
        in_specs=[
            pl.BlockSpec((None, tq, width), lambda b, h, i: (b, i, h)),
            pl.BlockSpec((None, s_len, width), lambda b, h, i: (b, 0, N_HEADS_A // HEADS_PER_STEP + h)),
            pl.BlockSpec((None, HEADS_PER_STEP * VT_ROWS, s_len), lambda b, h, i: (b, h, 0)),
            pl.BlockSpec((tq, tq), lambda b, h, i: (0, 0)),
            pl.BlockSpec((4, HD_A), lambda b, h, i: (0, 0)),
            pl.BlockSpec((1, 2 * HD_A), lambda b, h, i: (0, 0)),
        ],
        out_specs=pl.BlockSpec((None, tq, width), lambda b, h, i: (b, i, h)),
        out_shape=jax.ShapeDtypeStruct((bsz, s_len, WIDTH_A), BF16),
        scratch_shapes=[pltpu.VMEM((2 * HEADS_PER_STEP, VT_ROWS, tq), F32),
                        pltpu.VMEM((N_SCORE_BUF, tq, tq), F32)],
        compiler_params=_cparams(("parallel", "parallel", "arbitrary")),
        name="diff_attention",
    )(qk, qk, v_t, causal, lam_p, subln_g)


def _dilated_kernel(q_ref, k_ref, kp_ref, v_ref, vp_ref, o_ref, lse_ref, *, tq, blk):
    n = pl.program_id(2)
    scale = 1.0 / math.sqrt(HD_B)
    nt = (((1,), (1,)), ((), ()))
    rr = lax.broadcasted_iota(jnp.int32, (blk, 2 * blk), 0)
    cc = lax.broadcasted_iota(jnp.int32, (blk, 2 * blk), 1)
    band = jnp.logical_and(cc >= rr, cc <= rr + blk)
    lane = lax.broadcasted_iota(jnp.int32, (1, 2 * blk), 1)
    first_bias = jnp.where(lane >= blk, 0.0, jnp.where(n > 0, 0.0, -jnp.inf).astype(F32))

    def window(ref, pref, c, cols):
        if c == 0:
            return jnp.concatenate([pref[:, cols], ref[0:blk, cols]], axis=0)
        return ref[(c - 1) * blk:(c + 1) * blk, cols]

    def scores(unit):
        h, c = unit
        cols = slice(h * HD_B, (h + 1) * HD_B)
        qh = q_ref[c * blk:(c + 1) * blk, cols]
        s = lax.dot_general(qh, window(k_ref, kp_ref, c, cols), nt, preferred_element_type=F32) * scale
        s = jnp.where(band, s, -jnp.inf)
        return s + first_bias if c == 0 else s

    head_lane = lax.broadcasted_iota(jnp.int32, (1, LANES), 1) // LSE_LANES

    def finish(unit, s, lse_acc):
        h, c = unit
        cols = slice(h * HD_B, (h + 1) * HD_B)
        rows = slice(c * blk, (c + 1) * blk)
        m = jnp.max(s, axis=-1, keepdims=True)
        p = jnp.exp(s - m)
        den = jnp.sum(p, axis=-1, keepdims=True)
        pv = jnp.dot(p.astype(BF16), window(v_ref, vp_ref, c, cols), preferred_element_type=F32)
        o_ref[rows, cols] = (pv / den).astype(o_ref.dtype)
        lse = jnp.broadcast_to(m + jnp.log(den), (blk, LANES))
        lse_acc = lse if h == 0 else jnp.where(head_lane == h, lse, lse_acc)
        if h == N_HEADS_B - 1:
            lse_ref[rows, :] = lse_acc
        return lse_acc

    units = [(h, c) for c in range(tq // blk) for h in range(N_HEADS_B)]
    s = scores(units[0])
    lse_acc = None
    for idx, unit in enumerate(units):
        s_next = scores(units[idx + 1]) if idx + 1 < len(units) else None
        lse_acc = finish(unit, s, lse_acc)
        s = s_next


def _dilated_attention(qkv, bsz, s_len, group, tq):
    window, dil = DIL_PAIRS[group]
    blk = window // dil
    rows = s_len // dil
    tq = min(tq, rows)
    per_res = 3
    view = qkv.reshape(bsz, rows, dil * per_res * WIDTH_B)
    sub = tq // blk
    qcol = lambda r: r * per_res
    kcol = lambda r: r * per_res + 1
    vcol = lambda r: r * per_res + 2
    prev = lambda n: jnp.maximum(n * sub - 1, 0)
    kern = functools.partial(_dilated_kernel, tq=tq, blk=blk)
    o, lse = pl.pallas_call(
        kern,
        grid=(bsz, dil, rows // tq),
        in_specs=[
            pl.BlockSpec((None, tq, WIDTH_B), lambda b, r, n: (b, n, qcol(r))),
            pl.BlockSpec((None, tq, WIDTH_B), lambda b, r, n: (b, n, kcol(r))),
            pl.BlockSpec((None, blk, WIDTH_B), lambda b, r, n: (b, prev(n), kcol(r))),
            pl.BlockSpec((None, tq, WIDTH_B), lambda b, r, n: (b, n, vcol(r))),
            pl.BlockSpec((None, blk, WIDTH_B), lambda b, r, n: (b, prev(n), vcol(r))),
        ],
        out_specs=[pl.BlockSpec((None, tq, WIDTH_B), lambda b, r, n: (b, n, r)),
                   pl.BlockSpec((None, tq, LANES), lambda b, r, n: (b, n, r))],
        out_shape=[jax.ShapeDtypeStruct((bsz, rows, dil * WIDTH_B), BF16),
                   jax.ShapeDtypeStruct((bsz, rows, dil * LANES), F32)],
        compiler_params=_cparams(("parallel", "parallel", "arbitrary")),
        name=f"dilated_attention_g{group}",
    )(view, view, view, view, view)
    return o.reshape(bsz * rows, dil * WIDTH_B), lse.reshape(bsz * rows, dil * LANES)


def _ssm_kernel(u_ref, w8_ref, cd_ref, l8_ref, dskip_ref, wglu_ref, bglu_ref, o_ref,
                u_scr, x_scr, carry_scr, *, t_chunk):
    halo = SUBLANES

    @pl.when(pl.program_id(1) == 0)
    def _():
        carry_scr[...] = jnp.zeros_like(carry_scr)
        u_scr[0:halo, :] = jnp.zeros((halo, WIDTH_C), F32)

    @pl.when(pl.program_id(1) != 0)
    def _():
        u_scr[0:halo, :] = u_scr[t_chunk:t_chunk + halo, :]

    u = u_ref[...]
    u_scr[halo:halo + t_chunk, :] = u
    re = slice(0, N_STATE)
    im = slice(N_STATE, 2 * N_STATE)
    ch = WIDTH_C // 2
    sh = N_STATE // 2
    blk_ch = LANES // 2
    blk_st = blk_ch * SSM_STATE // SSM_GROUP
    low = lax.broadcasted_iota(jnp.int32, (1, LANES), 1) < blk_ch
    c_re, c_im = carry_scr[:, re], carry_scr[:, im]
    a8, b8 = l8_ref[:, re], l8_ref[:, im]
    n_slabs = t_chunk // SSM_SLAB
    slab_rows = lambda s: slice(s * SSM_SLAB, (s + 1) * SSM_SLAB)

    def drive(s):
        r0 = halo + s * SSM_SLAB
        for pair in range(WIDTH_C // LANES):
            lanes = slice(pair * LANES, (pair + 1) * LANES)
            taps = [u_scr[r0 - t:r0 - t + SSM_SLAB, lanes] for t in range(SSM_TAPS)]
            for half in range(2):
                cb = 2 * pair + half
                pieces = []
                for a in range(SSM_TAPS // 2):
                    even, odd = taps[2 * a], taps[2 * a + 1]
                    if half == 0:
                        v = jnp.where(low, even, pltpu.roll(odd, blk_ch, 1))
                    else:
                        v = jnp.where(low, pltpu.roll(even, blk_ch, 1), odd)
                    pieces.append(v.astype(BF16))
                w = jnp.dot(jnp.concatenate(pieces, axis=1), w8_ref[cb], preferred_element_type=F32)
                x_scr[slab_rows(s), cb * blk_st:(cb + 1) * blk_st] = w[:, :blk_st]
                x_scr[slab_rows(s), N_STATE + cb * blk_st:N_STATE + (cb + 1) * blk_st] = w[:, blk_st:]

    drive(0)
    if n_slabs > 1:
        drive(1)
    for s in range(n_slabs):
        rows = slab_rows(s)
        if s + 2 < n_slabs:
            drive(s + 2)
        for t in range(SSM_SLAB // SUBLANES):
            r8 = slice(s * SSM_SLAB + t * SUBLANES, s * SSM_SLAB + (t + 1) * SUBLANES)
            x_re = x_scr[r8, re] + (a8 * c_re - b8 * c_im)
            x_im = x_scr[r8, im] + (a8 * c_im + b8 * c_re)
            x_scr[r8, re] = x_re
            x_scr[r8, im] = x_im
            c_re, c_im = x_re, x_im
        halves = []
        for hh in range(2):
            acc = None
            for part in range(2):
                c0 = part * N_STATE + hh * sh
                d = jnp.dot(x_scr[rows, c0:c0 + sh].astype(BF16), cd_ref[c0:c0 + sh, hh * ch:(hh + 1) * ch],
                            preferred_element_type=F32)
                acc = d if acc is None else acc + d
            halves.append(acc)
        y = jnp.concatenate(halves, axis=1) + dskip_ref[...] * u[rows]
        cdf = 0.5 * (1.0 + jnp.tanh(math.sqrt(2.0 / math.pi) * (y + 0.044715 * (y * y * y))))
        z = y * cdf
        gate = jnp.dot(z.astype(BF16), wglu_ref[...], preferred_element_type=F32) + bglu_ref[...]
        o_ref[rows, :] = (z * jax.nn.sigmoid(gate)).astype(o_ref.dtype)
    carry_scr[:, re] = c_re
    carry_scr[:, im] = c_im


def _ssm_operands(a_re, a_im, log_dt, b_re, b_im, c_re, c_im):
    a_re, a_im = a_re.astype(F32), a_im.astype(F32)
    dt = jnp.exp(log_dt.astype(F32))[:, None]
    mag = jnp.exp(a_re * dt)
    lb_re, lb_im = mag * jnp.cos(a_im * dt), mag * jnp.sin(a_im * dt)
    n_re, n_im = lb_re - 1.0, lb_im
    den = a_re * a_re + a_im * a_im
    f_re = (n_re * a_re + n_im * a_im) / den
    f_im = (n_im * a_re - n_re * a_im) / den
    b_re, b_im = b_re.astype(F32), b_im.astype(F32)
    bb_re = f_re[..., None] * b_re - f_im[..., None] * b_im
    bb_im = f_re[..., None] * b_im + f_im[..., None] * b_re
    eye = jnp.eye(N_GROUPS_C, dtype=F32)
    blockdiag_out = lambda t: jnp.einsum('gcp,gh->gphc', t, eye).reshape(N_STATE, WIDTH_C)
    cd = jnp.concatenate([blockdiag_out(c_re.astype(F32)), -blockdiag_out(c_im.astype(F32))],
                         axis=0).astype(BF16)

    def power(k):
        return (mag ** k) * jnp.cos(a_im * dt * k), (mag ** k) * jnp.sin(a_im * dt * k)

    taps = []
    for s in range(SSM_TAPS):
        p_re, p_im = power(float(s))
        taps.append(jnp.stack([p_re[..., None] * bb_re - p_im[..., None] * bb_im,
                               p_re[..., None] * bb_im + p_im[..., None] * bb_re], axis=-1))
    grp = (LANES // 2) // SSM_GROUP
    n_blk = N_GROUPS_C // grp
    v = jnp.stack(taps, axis=0).reshape(SSM_TAPS, n_blk, grp, SSM_STATE, SSM_GROUP, 2)
    w8 = jnp.einsum('sbgpcq,gh->bsgcqhp', v, jnp.eye(grp, dtype=F32))
    w8 = w8.reshape(n_blk, SSM_TAPS * grp * SSM_GROUP, 2 * grp * SSM_STATE).astype(BF16)
    p_re, p_im = power(float(SSM_TAPS))
    l8 = jnp.concatenate([p_re.reshape(1, N_STATE), p_im.reshape(1, N_STATE)], axis=1)
    return w8, cd, jnp.broadcast_to(l8, (SUBLANES, 2 * N_STATE))


def _ssm_branch(c_u, bsz, s_len, w8, cd, l8, d_skip, w_glu_bf, layer, b_glu, t_chunk):
    n_chunks = s_len // t_chunk
    const2 = lambda b, c: (0, 0)
    kern = functools.partial(_ssm_kernel, t_chunk=t_chunk)
    return pl.pallas_call(
        kern,
        grid=(bsz, n_chunks),
        in_specs=[
            pl.BlockSpec((t_chunk, WIDTH_C), lambda b, c: (b * n_chunks + c, 0)),
            pl.BlockSpec((None,) + w8.shape[1:], lambda b, c: (layer, 0, 0, 0)),
            pl.BlockSpec((None, 2 * N_STATE, WIDTH_C), lambda b, c: (layer, 0, 0)),
            pl.BlockSpec((None, SUBLANES, 2 * N_STATE), lambda b, c: (layer, 0, 0)),
            pl.BlockSpec((1, WIDTH_C), const2),
            pl.BlockSpec((None, WIDTH_C, WIDTH_C), lambda b, c: (layer, 0, 0)),
            pl.BlockSpec((1, WIDTH_C), const2),
        ],
        out_specs=pl.BlockSpec((t_chunk, WIDTH_C), lambda b, c: (b * n_chunks + c, 0)),
        out_shape=jax.ShapeDtypeStruct((bsz * s_len, WIDTH_C), BF16),
        scratch_shapes=[pltpu.VMEM((t_chunk + SUBLANES, WIDTH_C), F32),
                        pltpu.VMEM((t_chunk, 2 * N_STATE), F32),
                        pltpu.VMEM((SUBLANES, 2 * N_STATE), F32)],
        compiler_params=_cparams(("parallel", "arbitrary")),
        name="s5_scan_glu",
    )(c_u, w8, cd, l8, d_skip, w_glu_bf, b_glu)


def _merge_body(x_ref, oa_ref, o0_ref, o1_ref, o2_ref, l0_ref, l1_ref, l2_ref, oc_ref,
                g0_ref, g1_ref, g2_ref, wa_ref, wb_ref, wc_ref, wo_ref, tok_scr, lse_scr, tm):
    for gi, (o_ref, l_ref) in enumerate(((o1_ref, l1_ref), (o2_ref, l2_ref))):
        dil = DIL_PAIRS[gi + 1][1]
        for r in range(dil):
            lse_scr[gi, pl.ds(r, tm // dil, stride=dil), :] = l_ref[:, r * LANES:(r + 1) * LANES]
            for li in range(WIDTH_B // LANES):
                cols = slice(r * WIDTH_B + li * LANES, r * WIDTH_B + (li + 1) * LANES)
                tok_scr[gi, li, pl.ds(r, tm // dil, stride=dil), :] = o_ref[:, cols].astype(F32)
    tok = lambda k: jnp.concatenate([tok_scr[k, li] for li in range(WIDTH_B // LANES)], axis=1)
    l0, l1, l2 = l0_ref[...], lse_scr[0], lse_scr[1]
    m = jnp.maximum(jnp.maximum(l0, l1), l2)
    e0, e1, e2 = jnp.exp(l0 - m), jnp.exp(l1 - m), jnp.exp(l2 - m)
    tot = e0 + e1 + e2
    sel_r = lax.broadcasted_iota(jnp.int32, (LANES, WIDTH_B), 0)
    sel_c = lax.broadcasted_iota(jnp.int32, (LANES, WIDTH_B), 1)
    select = jnp.where(sel_r == LSE_LANES * (sel_c // HD_B), 1.0, 0.0).astype(BF16)

    def spread(w):
        hi = w.astype(BF16)
        lo = (w - hi.astype(F32)).astype(BF16)
        return (jnp.dot(hi, select, preferred_element_type=F32)
                + jnp.dot(lo, select, preferred_element_type=F32))

    ob = (spread(e0 / tot) * o0_ref[...].astype(F32) + spread(e1 / tot) * tok(0)
          + spread(e2 / tot) * tok(1))
    ya = jnp.dot(oa_ref[...], wa_ref[...], preferred_element_type=F32)
    yb = jnp.dot(ob.astype(BF16), wb_ref[...], preferred_element_type=F32)
    yc = jnp.dot(oc_ref[...], wc_ref[...], preferred_element_type=F32)
    merged = (g0_ref[...].astype(F32) * ya + g1_ref[...].astype(F32) * yb
              + g2_ref[...].astype(F32) * yc)
    return x_ref[...] + jnp.dot(merged.astype(BF16), wo_ref[...], preferred_element_type=F32)


def _ffn_conv_halo(a_scr, tm, tiles_per_seq):
    halo = SUBLANES

    @pl.when(pl.program_id(0) % tiles_per_seq == 0)
    def _():
        a_scr[0:halo, :] = jnp.zeros((halo, D_FF), F32)

    @pl.when(pl.program_id(0) % tiles_per_seq != 0)
    def _():
        a_scr[0:halo, :] = a_scr[tm:tm + halo, :]


def _ffn_body(x, g_ref, wup_ref, cw_ref, cb_ref, wdown_ref, o_ref, a_scr, tm):
    halo = SUBLANES
    ms = jnp.mean(x * x, axis=-1, keepdims=True)
    h = (x * lax.rsqrt(ms + EPS) * g_ref[...]).astype(BF16)

    n_tiles = D_FF // FFN_TILE
    tile_cols = lambda f: slice(f * FFN_TILE, (f + 1) * FFN_TILE)

    def up(f):
        a_scr[halo:halo + tm, tile_cols(f)] = jnp.dot(h, wup_ref[:, tile_cols(f)], preferred_element_type=F32)
        return jnp.dot(h, wup_ref[:, D_FF + f * FFN_TILE:D_FF + (f + 1) * FFN_TILE],
                       preferred_element_type=F32)

    y = x
    gate = up(0)
    for f in range(n_tiles):
        cols = tile_cols(f)
        next_gate = up(f + 1) if f + 1 < n_tiles else None
        conv = (cb_ref[:, cols] + cw_ref[0:1, cols] * a_scr[halo - 2:halo - 2 + tm, cols]
                + cw_ref[1:2, cols] * a_scr[halo - 1:halo - 1 + tm, cols]
                + cw_ref[2:3, cols] * a_scr[halo:halo + tm, cols])
        act = (conv * jax.nn.sigmoid(conv)) * gate
        y = y + jnp.dot(act.astype(BF16), wdown_ref[cols, :], preferred_element_type=F32)
        gate = next_gate
    o_ref[...] = y


N_MERGE_INPUTS = 16


def _merge_ffn_kernel(*refs, tm, tiles_per_seq):
    merge_in = refs[:N_MERGE_INPUTS]
    g_ref, wup_ref, cw_ref, cb_ref, wdown_ref, o_ref, tok_scr, lse_scr, a_scr = refs[N_MERGE_INPUTS:]
    _ffn_conv_halo(a_scr, tm, tiles_per_seq)
    x_mid = _merge_body(*merge_in, tok_scr, lse_scr, tm)
    _ffn_body(x_mid, g_ref, wup_ref, cw_ref, cb_ref, wdown_ref, o_ref, a_scr, tm)


def _merge_ffn(x2d, oa, ob_parts, lse_parts, oc, gates, wa, wb, wc, wo,
               norm_g, wup_bf, conv_w, conv_b, wdown_bf, layer, s_len, tm):
    n = x2d.shape[0]
    row = lambda i: (i, 0)
    const = lambda i: (0, 0)
    half = pl.BlockSpec((tm, WIDTH_C), row)
    full = pl.BlockSpec((tm, D_MODEL), row)
    single = pl.Buffered(1)
    wspec_half = pl.BlockSpec((None, WIDTH_C, D_MODEL), lambda i: (layer, 0, 0), pipeline_mode=single)
    gspec = lambda k: pl.BlockSpec((tm, D_MODEL), lambda i: (i, k))
    dilated = lambda g, width: pl.BlockSpec((tm // DIL_PAIRS[g][1], DIL_PAIRS[g][1] * width), row)
    merge_specs = [full, half, half, dilated(1, WIDTH_B), dilated(2, WIDTH_B),
                   pl.BlockSpec((tm, LANES), row), dilated(1, LANES), dilated(2, LANES), half,
                   gspec(0), gspec(1), gspec(2),
                   wspec_half, wspec_half, wspec_half,
                   pl.BlockSpec((None, D_MODEL, D_MODEL), lambda i: (layer, 0, 0), pipeline_mode=single)]
    assert len(merge_specs) == N_MERGE_INPUTS
    ffn_specs = [
        pl.BlockSpec((1, D_MODEL), const),
        pl.BlockSpec((None, D_MODEL, 2 * D_FF), lambda i: (layer, 0, 0), pipeline_mode=single),
        pl.BlockSpec((CONV_WIDTH, D_FF), const),
        pl.BlockSpec((1, D_FF), const),
        pl.BlockSpec((None, D_FF, D_MODEL), lambda i: (layer, 0, 0), pipeline_mode=single),
    ]
    kern = functools.partial(_merge_ffn_kernel, tm=tm, tiles_per_seq=s_len // tm)
    return pl.pallas_call(
        kern,
        grid=(n // tm,),
        in_specs=merge_specs + ffn_specs,
        out_specs=full,
        out_shape=jax.ShapeDtypeStruct((n, D_MODEL), F32),
        scratch_shapes=[pltpu.VMEM((2, WIDTH_B // LANES, tm, LANES), F32),
                        pltpu.VMEM((2, tm, LANES), F32),
                        pltpu.VMEM((tm + SUBLANES, D_FF), F32)],
        compiler_params=_cparams(("arbitrary",)),
        name="merge_conv_ffn",
    )(x2d, oa, *ob_parts, *lse_parts, oc, gates, gates, gates, wa, wb, wc, wo,
      norm_g, wup_bf, conv_w, conv_b, wdown_bf)


def _lane_order(head_dim):
    half = head_dim // ROPE_FRACTION // 2
    heads = LANES // head_dim
    quarter = ROT_SHIFT // heads
    first, second = [], []
    for hd in range(heads):
        base = hd * head_dim
        first += list(range(base, base + half)) + list(range(base + 2 * half, base + half + quarter))
        second += list(range(base + half, base + 2 * half)) + list(range(base + half + quarter, base + head_dim))
    return first + second


def _permute_groups(w_cols, head_dim):
    order = _lane_order(head_dim)
    groups = w_cols.shape[-1] // LANES
    src = jnp.asarray([g * LANES + o for g in range(groups) for o in order])
    perm = jnp.zeros((groups * LANES,) * 2, F32).at[src, jnp.arange(groups * LANES)].set(1.0).astype(w_cols.dtype)
    return jnp.dot(w_cols, perm, preferred_element_type=F32).astype(w_cols.dtype)


def _rope_tables(positions, head_dim):
    rot = head_dim // ROPE_FRACTION
    half = rot // 2
    heads = LANES // head_dim
    quarter = ROT_SHIFT // heads
    inv = ROPE_THETA ** (-jnp.arange(0, rot, 2, dtype=F32) / rot)
    ang = positions.reshape(-1).astype(F32)[:, None] * inv
    cos, sin = jnp.cos(ang), jnp.sin(ang)
    n = ang.shape[0]
    pad = quarter - half
    c_q = jnp.concatenate([cos, jnp.ones((n, pad), F32)], axis=1)
    s_q = jnp.concatenate([sin, jnp.zeros((n, pad), F32)], axis=1)
    c = jnp.tile(c_q, (1, 2 * heads))
    s = jnp.concatenate([jnp.tile(-s_q, (1, heads)), jnp.tile(s_q, (1, heads))], axis=1)
    return c, s


def kernel(x, positions, attn_norm_g, w_in, b_gate, qn_a, kn_a, lam_q1, lam_k1, lam_q2, lam_k2, subln_g, w_br_a, qn_b, kn_b, w_br_b, ssm_a_re, ssm_a_im, ssm_log_dt, ssm_b_re, ssm_b_im, ssm_c_re, ssm_c_im, ssm_d, w_glu, b_glu, w_br_c, w_out, ffn_norm_g, w_up, conv_w, conv_b, w_down):
    bsz, s_len, d_model = x.shape
    depth = w_in.shape[0]
    assert d_model == D_MODEL and w_in.shape[2] == IN_COLS
    assert s_len % DIL_PAIRS[-1][0] == 0, "sequence must be a multiple of the largest dilated window"
    n = bsz * s_len

    tm_proj = 256
    tq_a = min(512, s_len)
    tq_b = 1024
    t_ssm = min(1024, s_len)
    tm_ffn = min(256, s_len)

    rope = _rope_tables(positions, HD_A) + _rope_tables(positions, HD_B)
    w_in_bf = w_in.astype(BF16)
    w_rows = w_in_bf.reshape(depth * D_MODEL, IN_COLS)
    qk_chunk = lambda c0, hd: _permute_groups(w_rows[:, c0:c0 + QK_CHUNK], hd).reshape(depth, D_MODEL, QK_CHUNK)
    w_qk = [qk_chunk(0, HD_A)] + [qk_chunk(TILE_BQ * COL_TILE + k * QK_CHUNK, HD_B)
                                  for k in range(2 * B_COLS // QK_CHUNK)]
    order_a = jnp.asarray(_lane_order(HD_A))
    order_b = jnp.asarray(_lane_order(HD_B))
    wa_bf, wb_bf, wc_bf, wo_bf = (w.astype(BF16) for w in (w_br_a, w_br_b, w_br_c, w_out))
    wglu_bf, wup_bf, wdown_bf = w_glu.astype(BF16), w_up.astype(BF16), w_down.astype(BF16)

    w8, cd, l8 = jax.vmap(_ssm_operands)(ssm_a_re, ssm_a_im, ssm_log_dt, ssm_b_re, ssm_b_im, ssm_c_re, ssm_c_im)

    x2d = x.reshape(n, D_MODEL)
    for l in range(depth):
        lam_init = 0.8 - 0.6 * math.exp(-0.3 * l)
        ones = lambda w: jnp.ones((w,), F32)
        group_a = lambda g: jnp.tile(jnp.tile(g.astype(F32), LANES // HD_A)[order_a], A_Q_COLS // LANES)
        group_b = lambda g: jnp.tile(g.astype(F32)[order_b], B_COLS // LANES)
        col_gain = jnp.concatenate([
            group_a(qn_a[l]) * (math.log2(math.e) / math.sqrt(HD_A)), group_a(kn_a[l]), ones(WIDTH_A),
            group_b(qn_b[l]), group_b(kn_b[l]),
            ones(B_COLS + WIDTH_C + N_BRANCHES * D_MODEL)]).reshape(1, IN_COLS)
        col_bias = jnp.concatenate([jnp.zeros((IN_COLS - N_BRANCHES * D_MODEL,), F32),
                                    b_gate[l].astype(F32)]).reshape(1, IN_COLS)
        qk_a, v_t, *qkv_b, c_u, gates = _in_projection(
            x2d, attn_norm_g[l].reshape(1, D_MODEL).astype(F32), w_in_bf, w_qk, l, col_gain, col_bias, rope,
            bsz, s_len, tm_proj)

        lam_p = jnp.stack([lam_q1[l], lam_k1[l], lam_q2[l], lam_k2[l]]).astype(F32)
        oa = _diff_attention(qk_a.reshape(bsz, s_len, 2 * A_Q_COLS), v_t, lam_p,
                             subln_g[l].reshape(1, 2 * HD_A).astype(F32), lam_init, tq_a)
        oa = oa.reshape(n, WIDTH_A)

        ob_parts, lse_parts = zip(*[_dilated_attention(qkv_b[g], bsz, s_len, g, tq_b) for g in range(N_DIL)])

        oc = _ssm_branch(c_u, bsz, s_len, w8, cd, l8, ssm_d[l].reshape(1, WIDTH_C).astype(F32),
                         wglu_bf, l, b_glu[l].reshape(1, WIDTH_C).astype(F32), t_ssm)

        x2d = _merge_ffn(x2d, oa, ob_parts, lse_parts, oc, gates, wa_bf, wb_bf, wc_bf, wo_bf,
                         ffn_norm_g[l].reshape(1, D_MODEL).astype(F32), wup_bf,
                         conv_w[l].astype(F32), conv_b[l].reshape(1, D_FF).astype(F32), wdown_bf, l,
                         s_len, tm_ffn)
    return x2d.reshape(bsz, s_len, D_MODEL)
```

```python
import functools
import math

import jax
import jax.numpy as jnp
from jax import lax
from jax.experimental import pallas as pl
from jax.experimental.pallas import tpu as pltpu

F32 = jnp.float32
BF16 = jnp.bfloat16

LANES = 128
SUBLANES = 8
ROT_SHIFT = LANES // 2

D_MODEL = 1024
N_HEADS_A = 4
HD_A = 64
N_DIL = 3
DIL_PAIRS = ((128, 1), (512, 4), (2048, 16))
N_HEADS_B = 4
HD_B = 128
WIDTH_B = N_HEADS_B * HD_B
SSM_GROUP = 16
SSM_STATE = 64
WIDTH_C = 512
N_GROUPS_C = WIDTH_C // SSM_GROUP
N_STATE = N_GROUPS_C * SSM_STATE
N_BRANCHES = 3
D_FF = 2816
CONV_WIDTH = 3
ROPE_THETA = 500000.0
ROPE_FRACTION = 4
EPS = 1e-6

COL_TILE = 512
QK_CHUNK = 2 * COL_TILE
A_Q_COLS = 2 * N_HEADS_A * HD_A
WIDTH_A = N_HEADS_A * 2 * HD_A
B_COLS = N_DIL * N_HEADS_B * HD_B
IN_COLS = 2 * A_Q_COLS + WIDTH_A + 3 * B_COLS + WIDTH_C + N_BRANCHES * D_MODEL
N_COL_TILES = IN_COLS // COL_TILE
TILE_AQ, TILE_AK, TILE_AV = 0, 1, 2
TILE_BQ, TILE_BK, TILE_BV = 3, 6, 9
TILE_CU = 12

HEADS_PER_STEP = 4
N_SCORE_BUF = 4
AHEAD = N_SCORE_BUF - 1
VT_PAD = 16
VT_ROWS = 2 * HD_A + VT_PAD
LSE_LANES = LANES // N_HEADS_B
SSM_SLAB = 256
SSM_TAPS = SUBLANES
FFN_TILE = 256

VMEM_LIMIT = 56 * 1024 * 1024


def _cparams(sem):
    return pltpu.CompilerParams(dimension_semantics=sem, vmem_limit_bytes=VMEM_LIMIT)


def _first_head_lanes():
    lane = lax.broadcasted_iota(jnp.int32, (1, LANES), 1)
    return (lane // (ROT_SHIFT // 2)) % 2 == 0


def _norm_rope_tile(acc, gain, seg, c_ref, s_ref):
    cos = c_ref[...]
    sin = s_ref[...]
    first = _first_head_lanes()
    outs = []
    for gi in range(acc.shape[1] // LANES):
        y = acc[:, gi * LANES:(gi + 1) * LANES]
        ysq = y * y
        tot = jnp.sum(ysq, axis=-1, keepdims=True)
        if seg == LANES:
            ssum = tot
        else:
            one = jnp.sum(jnp.where(first, ysq, 0.0), axis=-1, keepdims=True)
            ssum = jnp.where(first, one, tot - one)
        yn = y * lax.rsqrt(ssum * (1.0 / seg) + EPS) * gain[:, gi * LANES:(gi + 1) * LANES]
        outs.append(yn * cos + pltpu.roll(yn, ROT_SHIFT, 1) * sin)
    return jnp.concatenate(outs, axis=1)


def _inproj_kernel(x_ref, g_ref, w_ref, wqa_ref, wqb0_ref, wqb1_ref, wqb2_ref, gain_ref, bias_ref,
                   ca_ref, sa_ref, cb_ref, sb_ref,
                   qk_ref, vt_ref, b0_ref, b1_ref, b2_ref, cu_ref, gate_ref, dil_scr, *, tm):
    x = x_ref[...]
    ms = jnp.mean(x * x, axis=-1, keepdims=True)
    h = (x * lax.rsqrt(ms + EPS) * g_ref[...]).astype(BF16)
    dil_refs = (b0_ref, b1_ref, b2_ref)
    heavy = [TILE_AQ, TILE_AK] + list(range(TILE_BQ, TILE_BV))
    light = [TILE_AV] + list(range(TILE_BV, TILE_CU + 1))
    order = heavy + [j for j in range(N_COL_TILES) if j not in heavy + light] + light
    wqb_refs = (wqb0_ref, wqb1_ref, wqb2_ref)

    def project(j):
        if j in (TILE_AQ, TILE_AK):
            ref, c0 = wqa_ref, j * COL_TILE
        elif TILE_BQ <= j < TILE_BV:
            off = (j - TILE_BQ) * COL_TILE
            ref, c0 = wqb_refs[off // QK_CHUNK], off % QK_CHUNK
        else:
            ref, c0 = w_ref, j * COL_TILE
        return jnp.dot(h, ref[:, c0:c0 + COL_TILE], preferred_element_type=F32)

    nxt = project(order[0])
    for idx, j in enumerate(order):
        cols = slice(j * COL_TILE, (j + 1) * COL_TILE)
        acc = nxt
        if idx + 1 < len(order):
            nxt = project(order[idx + 1])
        if j in (TILE_AQ, TILE_AK):
            qk_ref[:, cols] = _norm_rope_tile(acc, gain_ref[:, cols], HD_A, ca_ref, sa_ref).astype(BF16)
        elif j == TILE_AV:
            acc_t = acc.T.astype(BF16)
            for hd in range(N_HEADS_A):
                vt_ref[hd * VT_ROWS:hd * VT_ROWS + 2 * HD_A, :] = acc_t[hd * 2 * HD_A:(hd + 1) * 2 * HD_A, :]
                vt_ref[hd * VT_ROWS + 2 * HD_A:(hd + 1) * VT_ROWS, :] = jnp.ones((VT_PAD, tm), BF16)
        elif j < TILE_CU:
            part, group = divmod(j - TILE_BQ, N_DIL)
            if part < 2:
                acc = _norm_rope_tile(acc, gain_ref[:, cols], HD_B, cb_ref, sb_ref)
            dil = DIL_PAIRS[group][1]
            out_ref = dil_refs[group]
            if dil == 1:
                out_ref[:, part * WIDTH_B:(part + 1) * WIDTH_B] = acc.astype(BF16)
            else:
                slot = (group - 1) * 3 + part
                for gi in range(WIDTH_B // LANES):
                    dil_scr[slot, gi] = acc[:, gi * LANES:(gi + 1) * LANES]
                for r in range(dil):
                    for gi in range(WIDTH_B // LANES):
                        c0 = (r * 3 + part) * WIDTH_B + gi * LANES
                        out_ref[:, c0:c0 + LANES] = (
                            dil_scr[slot, gi, pl.ds(r, tm // dil, stride=dil), :].astype(BF16))
        elif j == TILE_CU:
            cu_ref[...] = acc
        else:
            g0 = (j - TILE_CU - 1) * COL_TILE
            gate_ref[:, g0:g0 + COL_TILE] = jax.nn.sigmoid(acc + bias_ref[:, cols]).astype(gate_ref.dtype)


def _in_projection(x2d, norm_g, w_bf, w_qk, layer, col_gain, col_bias, rope, bsz, s_len, tm):
    n = x2d.shape[0]
    tiles_per_seq = s_len // tm
    row = lambda i: (i, 0)
    const = lambda i: (0, 0)
    d1, d2 = DIL_PAIRS[1][1], DIL_PAIRS[2][1]
    kern = functools.partial(_inproj_kernel, tm=tm)
    resident = lambda cols: pl.BlockSpec((None, D_MODEL, cols), lambda i: (layer, 0, 0),
                                         pipeline_mode=pl.Buffered(1))
    return pl.pallas_call(
        kern,
        grid=(n // tm,),
        in_specs=[
            pl.BlockSpec((tm, D_MODEL), row),
            pl.BlockSpec((1, D_MODEL), const),
            resident(IN_COLS), resident(QK_CHUNK), resident(QK_CHUNK), resident(QK_CHUNK), resident(QK_CHUNK),
            pl.BlockSpec((1, IN_COLS), const),
            pl.BlockSpec((1, IN_COLS), const),
        ] + [pl.BlockSpec((tm, LANES), row)] * len(rope),
        out_specs=[
            pl.BlockSpec((tm, 2 * A_Q_COLS), row),
            pl.BlockSpec((None, N_HEADS_A * VT_ROWS, tm), lambda i: (i // tiles_per_seq, 0, i % tiles_per_seq)),
            pl.BlockSpec((tm, 3 * WIDTH_B), row),
            pl.BlockSpec((tm // d1, d1 * 3 * WIDTH_B), row),
            pl.BlockSpec((tm // d2, d2 * 3 * WIDTH_B), row),
            pl.BlockSpec((tm, WIDTH_C), row),
            pl.BlockSpec((tm, N_BRANCHES * D_MODEL), row),
        ],
        out_shape=[jax.ShapeDtypeStruct((n, 2 * A_Q_COLS), BF16),
                   jax.ShapeDtypeStruct((bsz, N_HEADS_A * VT_ROWS, s_len), BF16),
                   jax.ShapeDtypeStruct((n, 3 * WIDTH_B), BF16),
                   jax.ShapeDtypeStruct((n // d1, d1 * 3 * WIDTH_B), BF16),
                   jax.ShapeDtypeStruct((n // d2, d2 * 3 * WIDTH_B), BF16),
                   jax.ShapeDtypeStruct((n, WIDTH_C), F32),
                   jax.ShapeDtypeStruct((n, N_BRANCHES * D_MODEL), BF16)],
        scratch_shapes=[pltpu.VMEM((6, WIDTH_B // LANES, tm, LANES), F32)],
        compiler_params=_cparams(("parallel",)),
        name="in_projection",
    )(x2d, norm_g, w_bf, *w_qk, col_gain, col_bias, *rope)


def _diffattn_kernel(q_ref, k_ref, vt_ref, mask_ref, lam_ref, subg_ref, o_ref, acc_scr, s_scr, *, tq, lam_init):
    i = pl.program_id(2)
    q = q_ref[...]
    first = _first_head_lanes()
    nt = (((1,), (1,)), ((), ()))
    qm = []
    for hh in range(HEADS_PER_STEP):
        qh = q[:, hh * 2 * HD_A:(hh + 1) * 2 * HD_A]
        zero = jnp.zeros_like(qh)
        qm += [jnp.where(first, qh, zero), jnp.where(first, zero, qh)]
    n_maps = len(qm)
    acc_scr[...] = jnp.zeros_like(acc_scr)

    def score(j, mi):
        r0 = pl.multiple_of(j * tq, tq)
        hh = mi // 2
        kblk = k_ref[pl.ds(r0, tq), hh * 2 * HD_A:(hh + 1) * 2 * HD_A]
        s_scr[mi % N_SCORE_BUF] = lax.dot_general(kblk, qm[mi], nt, preferred_element_type=F32)

    def consume(j, mi, m_old, diagonal):
        r0 = pl.multiple_of(j * tq, tq)
        hh = mi // 2
        vtblk = vt_ref[hh * VT_ROWS:(hh + 1) * VT_ROWS, pl.ds(r0, tq)]
        st = s_scr[mi % N_SCORE_BUF]
        if diagonal:
            st = st + mask_ref[...]
        m_new = jnp.maximum(m_old, jnp.max(st, axis=0, keepdims=True))
        p = jnp.exp2(st - m_new)
        alpha = jnp.exp2(m_old - m_new)
        acc_scr[mi] = alpha * acc_scr[mi] + jnp.dot(vtblk, p.astype(BF16), preferred_element_type=F32)
        return m_new

    def stage(j, ms, diagonal):
        new = []
        for mi in range(n_maps):
            ahead = mi + AHEAD
            if ahead < n_maps:
                score(j, ahead)
            elif not diagonal:
                score(j + 1, ahead - n_maps)
            new.append(consume(j, mi, ms[mi], diagonal))
        return tuple(new)

    init = tuple(jnp.full((1, tq), -jnp.inf, F32) for _ in range(n_maps))

    for mi in range(AHEAD):
        score(0, mi)
    ms = lax.fori_loop(0, i // 2, lambda t, c: stage(2 * t + 1, stage(2 * t, c, False), False), init)
    odd = lax.rem(i, 2) == 1

    @pl.when(odd)
    def _():
        stage(i, stage(i - 1, ms, False), True)

    @pl.when(jnp.logical_not(odd))
    def _():
        stage(i, ms, True)

    lam_p = lam_ref[...]
    lam = (jnp.exp(jnp.sum(lam_p[0:1] * lam_p[1:2], axis=-1, keepdims=True))
           - jnp.exp(jnp.sum(lam_p[2:3] * lam_p[3:4], axis=-1, keepdims=True)) + lam_init)
    vals = slice(0, 2 * HD_A)
    den = slice(2 * HD_A, 2 * HD_A + 1)
    for hh in range(HEADS_PER_STEP):
        a1, a2 = acc_scr[2 * hh], acc_scr[2 * hh + 1]
        o_t = a1[vals] / a1[den] - lam * (a2[vals] / a2[den])
        msq = jnp.mean(o_t * o_t, axis=0, keepdims=True)
        o = (o_t * lax.rsqrt(msq + EPS)).T
        o_ref[:, hh * 2 * HD_A:(hh + 1) * 2 * HD_A] = ((o * subg_ref[...]) * (1.0 - lam_init)).astype(o_ref.dtype)


def _diff_attention(qk, v_t, lam_p, subln_g, lam_init, tq):
    bsz, s_len, _ = qk.shape
    width = HEADS_PER_STEP * 2 * HD_A
    kern = functools.partial(_diffattn_kernel, tq=tq, lam_init=lam_init)
    key = lax.broadcasted_iota(jnp.int32, (tq, tq), 0)
    qry = lax.broadcasted_iota(jnp.int32, (tq, tq), 1)
    causal = jnp.where(key <= qry, 0.0, -jnp.inf).astype(F32)
    return pl.pallas_call(
        kern,
        grid=(bsz, N_HEADS_A // HEADS_PER_STEP, s_len // tq),
        in_specs=[
            pl.BlockSpec((None, tq, width), lambda b, h, i: (b, i, h)),
            pl.BlockSpec((None, s_len, width), lambda b, h, i: (b, 0, N_HEADS_A // HEADS_PER_STEP + h)),
            pl.BlockSpec((None, HEADS_PER_STEP * VT_ROWS, s_len), lambda b, h, i: (b, h, 0)),
            pl.BlockSpec((tq, tq), lambda b, h, i: (0, 0)),
            pl.BlockSpec((4, HD_A), lambda b, h, i: (0, 0)),
            pl.BlockSpec((1, 2 * HD_A), lambda b, h, i: (0, 0)),
        ],
        out_specs=pl.BlockSpec((None, tq, width), lambda b, h, i: (b, i, h)),
        out_shape=jax.ShapeDtypeStruct((bsz, s_len, WIDTH_A), BF16),
        scratch_shapes=[pltpu.VMEM((2 * HEADS_PER_STEP, VT_ROWS, tq), F32),
                        pltpu.VMEM((N_SCORE_BUF, tq, tq), F32)],
        compiler_params=_cparams(("parallel", "parallel", "arbitrary")),
        name="diff_attention",
    )(qk, qk, v_t, causal, lam_p, subln_g)


def _dilated_kernel(q_ref, k_ref, kp_ref, v_ref, vp_ref, o_ref, lse_ref, *, tq, blk):
    n = pl.program_id(2)
    scale = 1.0 / math.sqrt(HD_B)
    nt = (((1,), (1,)), ((), ()))
    rr = lax.broadcasted_iota(jnp.int32, (blk, 2 * blk), 0)
    cc = lax.broadcasted_iota(jnp.int32, (blk, 2 * blk), 1)
    band = jnp.logical_and(cc >= rr, cc <= rr + blk)
    lane = lax.broadcasted_iota(jnp.int32, (1, 2 * blk), 1)
    first_bias = jnp.where(lane >= blk, 0.0, jnp.where(n > 0, 0.0, -jnp.inf).astype(F32))

    def window(ref, pref, c, cols):
        if c == 0:
            return jnp.concatenate([pref[:, cols], ref[0:blk, cols]], axis=0)
        return ref[(c - 1) * blk:(c + 1) * blk, cols]

    def scores(unit):
        h, c = unit
        cols = slice(h * HD_B, (h + 1) * HD_B)
        qh = q_ref[c * blk:(c + 1) * blk, cols]
        s = lax.dot_general(qh, window(k_ref, kp_ref, c, cols), nt, preferred_element_type=F32) * scale
        s = jnp.where(band, s, -jnp.inf)
        return s + first_bias if c == 0 else s

    head_lane = lax.broadcasted_iota(jnp.int32, (1, LANES), 1) // LSE_LANES

    def finish(unit, s, lse_acc):
        h, c = unit
        cols = slice(h * HD_B, (h + 1) * HD_B)
        rows = slice(c * blk, (c + 1) * blk)
        m = jnp.max(s, axis=-1, keepdims=True)
        p = jnp.exp(s - m)
        den = jnp.sum(p, axis=-1, keepdims=True)
        pv = jnp.dot(p.astype(BF16), window(v_ref, vp_ref, c, cols), preferred_element_type=F32)
        o_ref[rows, cols] = (pv / den).astype(o_ref.dtype)
        lse = jnp.broadcast_to(m + jnp.log(den), (blk, LANES))
        lse_acc = lse if h == 0 else jnp.where(head_lane == h, lse, lse_acc)
        if h == N_HEADS_B - 1:
            lse_ref[rows, :] = lse_acc
        return lse_acc

    units = [(h, c) for c in range(tq // blk) for h in range(N_HEADS_B)]
    s = scores(units[0])
    lse_acc = None
    for idx, unit in enumerate(units):
        s_next = scores(units[idx + 1]) if idx + 1 < len(units) else None
        lse_acc = finish(unit, s, lse_acc)
        s = s_next


def _dilated_attention(qkv, bsz, s_len, group, tq):
    window, dil = DIL_PAIRS[group]
    blk = window // dil
    rows = s_len // dil
    tq = min(tq, rows)
    per_res = 3
    view = qkv.reshape(bsz, rows, dil * per_res * WIDTH_B)
    sub = tq // blk
    qcol = lambda r: r * per_res
    kcol = lambda r: r * per_res + 1
    vcol = lambda r: r * per_res + 2
    prev = lambda n: jnp.maximum(n * sub - 1, 0)
    kern = functools.partial(_dilated_kernel, tq=tq, blk=blk)
    o, lse = pl.pallas_call(
        kern,
        grid=(bsz, dil, rows // tq),
        in_specs=[
            pl.BlockSpec((None, tq, WIDTH_B), lambda b, r, n: (b, n, qcol(r))),
            pl.BlockSpec((None, tq, WIDTH_B), lambda b, r, n: (b, n, kcol(r))),
            pl.BlockSpec((None, blk, WIDTH_B), lambda b, r, n: (b, prev(n), kcol(r))),
            pl.BlockSpec((None, tq, WIDTH_B), lambda b, r, n: (b, n, vcol(r))),
            pl.BlockSpec((None, blk, WIDTH_B), lambda b, r, n: (b, prev(n), vcol(r))),
        ],
        out_specs=[pl.BlockSpec((None, tq, WIDTH_B), lambda b, r, n: (b, n, r)),
                   pl.BlockSpec((None, tq, LANES), lambda b, r, n: (b, n, r))],
        out_shape=[jax.ShapeDtypeStruct((bsz, rows, dil * WIDTH_B), BF16),
                   jax.ShapeDtypeStruct((bsz, rows, dil * LANES), F32)],
        compiler_params=_cparams(("parallel", "parallel", "arbitrary")),
        name=f"dilated_attention_g{group}",
    )(view, view, view, view, view)
    return o.reshape(bsz * rows, dil * WIDTH_B), lse.reshape(bsz * rows, dil * LANES)


def _ssm_kernel(u_ref, w8_ref, cd_ref, l8_ref, dskip_ref, wglu_ref, bglu_ref, o_ref,
                u_scr, x_scr, carry_scr, *, t_chunk):
    halo = SUBLANES

    @pl.when(pl.program_id(1) == 0)
    def _():
        carry_scr[...] = jnp.zeros_like(carry_scr)
        u_scr[0:halo, :] = jnp.zeros((halo, WIDTH_C), F32)

    @pl.when(pl.program_id(1) != 0)
    def _():
        u_scr[0:halo, :] = u_scr[t_chunk:t_chunk + halo, :]

    u = u_ref[...]
    u_scr[halo:halo + t_chunk, :] = u
    re = slice(0, N_STATE)
    im = slice(N_STATE, 2 * N_STATE)
    ch = WIDTH_C // 2
    sh = N_STATE // 2
    blk_ch = LANES // 2
    blk_st = blk_ch * SSM_STATE // SSM_GROUP
    low = lax.broadcasted_iota(jnp.int32, (1, LANES), 1) < blk_ch
    c_re, c_im = carry_scr[:, re], carry_scr[:, im]
    a8, b8 = l8_ref[:, re], l8_ref[:, im]
    n_slabs = t_chunk // SSM_SLAB
    slab_rows = lambda s: slice(s * SSM_SLAB, (s + 1) * SSM_SLAB)

    def drive(s):
        r0 = halo + s * SSM_SLAB
        for pair in range(WIDTH_C // LANES):
            lanes = slice(pair * LANES, (pair + 1) * LANES)
            taps = [u_scr[r0 - t:r0 - t + SSM_SLAB, lanes] for t in range(SSM_TAPS)]
            for half in range(2):
                cb = 2 * pair + half
                pieces = []
                for a in range(SSM_TAPS // 2):
                    even, odd = taps[2 * a], taps[2 * a + 1]
                    if half == 0:
                        v = jnp.where(low, even, pltpu.roll(odd, blk_ch, 1))
                    else:
                        v = jnp.where(low, pltpu.roll(even, blk_ch, 1), odd)
                    pieces.append(v.astype(BF16))
                w = jnp.dot(jnp.concatenate(pieces, axis=1), w8_ref[cb], preferred_element_type=F32)
                x_scr[slab_rows(s), cb * blk_st:(cb + 1) * blk_st] = w[:, :blk_st]
                x_scr[slab_rows(s), N_STATE + cb * blk_st:N_STATE + (cb + 1) * blk_st] = w[:, blk_st:]

    drive(0)
    if n_slabs > 1:
        drive(1)
    for s in range(n_slabs):
        rows = slab_rows(s)
        if s + 2 < n_slabs:
            drive(s + 2)
        for t in range(SSM_SLAB // SUBLANES):
            r8 = slice(s * SSM_SLAB + t * SUBLANES, s * SSM_SLAB + (t + 1) * SUBLANES)
            x_re = x_scr[r8, re] + (a8 * c_re - b8 * c_im)
            x_im = x_scr[r8, im] + (a8 * c_im + b8 * c_re)
            x_scr[r8, re] = x_re
            x_scr[r8, im] = x_im
            c_re, c_im = x_re, x_im
        halves = []
        for hh in range(2):
            acc = None
            for part in range(2):
                c0 = part * N_STATE + hh * sh
                d = jnp.dot(x_scr[rows, c0:c0 + sh].astype(BF16), cd_ref[c0:c0 + sh, hh * ch:(hh + 1) * ch],
                            preferred_element_type=F32)
                acc = d if acc is None else acc + d
            halves.append(acc)
        y = jnp.concatenate(halves, axis=1) + dskip_ref[...] * u[rows]
        cdf = 0.5 * (1.0 + jnp.tanh(math.sqrt(2.0 / math.pi) * (y + 0.044715 * (y * y * y))))
        z = y * cdf
        gate = jnp.dot(z.astype(BF16), wglu_ref[...], preferred_element_type=F32) + bglu_ref[...]
        o_ref[rows, :] = (z * jax.nn.sigmoid(gate)).astype(o_ref.dtype)
    carry_scr[:, re] = c_re
    carry_scr[:, im] = c_im


def _ssm_operands(a_re, a_im, log_dt, b_re, b_im, c_re, c_im):
    a_re, a_im = a_re.astype(F32), a_im.astype(F32)
    dt = jnp.exp(log_dt.astype(F32))[:, None]
    mag = jnp.exp(a_re * dt)
    lb_re, lb_im = mag * jnp.cos(a_im * dt), mag * jnp.sin(a_im * dt)
    n_re, n_im = lb_re - 1.0, lb_im
    den = a_re * a_re + a_im * a_im
    f_re = (n_re * a_re + n_im * a_im) / den
    f_im = (n_im * a_re - n_re * a_im) / den
    b_re, b_im = b_re.astype(F32), b_im.astype(F32)
    bb_re = f_re[..., None] * b_re - f_im[..., None] * b_im
    bb_im = f_re[..., None] * b_im + f_im[..., None] * b_re
    def blockdiag_out(t):
        flat = t.transpose(0, 2, 1).reshape(N_STATE, SSM_GROUP).astype(BF16)
        col = jnp.arange(WIDTH_C)
        spread = (col[None, :] % SSM_GROUP == jnp.arange(SSM_GROUP)[:, None]).astype(BF16)
        full = jnp.dot(flat, spread, preferred_element_type=F32)
        keep = jnp.arange(N_STATE)[:, None] // SSM_STATE == col[None, :] // SSM_GROUP
        return jnp.where(keep, full, 0.0)

    cd = jnp.concatenate([blockdiag_out(c_re.astype(F32)), -blockdiag_out(c_im.astype(F32))],
                         axis=0).astype(BF16)

    def power(k):
        return (mag ** k) * jnp.cos(a_im * dt * k), (mag ** k) * jnp.sin(a_im * dt * k)

    taps = []
    for s in range(SSM_TAPS):
        p_re, p_im = power(float(s))
        taps.append(jnp.stack([p_re[..., None] * bb_re - p_im[..., None] * bb_im,
                               p_re[..., None] * bb_im + p_im[..., None] * bb_re], axis=-1))
    grp = (LANES // 2) // SSM_GROUP
    n_blk = N_GROUPS_C // grp
    k_blk = SSM_TAPS * grp * SSM_GROUP
    n_col = 2 * grp * SSM_STATE
    v = jnp.stack(taps, axis=0).reshape(SSM_TAPS, n_blk, grp, SSM_STATE, SSM_GROUP, 2)
    dense = v.transpose(1, 0, 2, 4, 5, 3).reshape(n_blk, k_blk, 2 * SSM_STATE).astype(BF16)
    col = jnp.arange(n_col)
    src = (col // (grp * SSM_STATE)) * SSM_STATE + col % SSM_STATE
    spread = (src[None, :] == jnp.arange(2 * SSM_STATE)[:, None]).astype(BF16)
    full = jnp.einsum('bkd,dn->bkn', dense, spread, preferred_element_type=F32)
    row_grp = (jnp.arange(k_blk) // SSM_GROUP) % grp
    col_grp = (col // SSM_STATE) % grp
    w8 = jnp.where(row_grp[:, None] == col_grp[None, :], full, 0.0).astype(BF16)
    p_re, p_im = power(float(SSM_TAPS))
    l8 = jnp.concatenate([p_re.reshape(1, N_STATE), p_im.reshape(1, N_STATE)], axis=1)
    return w8, cd, jnp.broadcast_to(l8, (SUBLANES, 2 * N_STATE))


def _ssm_branch(c_u, bsz, s_len, w8, cd, l8, d_skip, w_glu_bf, layer, b_glu, t_chunk):
    n_chunks = s_len // t_chunk
    const2 = lambda b, c: (0, 0)
    kern = functools.partial(_ssm_kernel, t_chunk=t_chunk)
    return pl.pallas_call(
        kern,
        grid=(bsz, n_chunks),
        in_specs=[
            pl.BlockSpec((t_chunk, WIDTH_C), lambda b, c: (b * n_chunks + c, 0)),
            pl.BlockSpec((None,) + w8.shape[1:], lambda b, c: (layer, 0, 0, 0)),
            pl.BlockSpec((None, 2 * N_STATE, WIDTH_C), lambda b, c: (layer, 0, 0)),
            pl.BlockSpec((None, SUBLANES, 2 * N_STATE), lambda b, c: (layer, 0, 0)),
            pl.BlockSpec((1, WIDTH_C), const2),
            pl.BlockSpec((None, WIDTH_C, WIDTH_C), lambda b, c: (layer, 0, 0)),
            pl.BlockSpec((1, WIDTH_C), const2),
        ],
        out_specs=pl.BlockSpec((t_chunk, WIDTH_C), lambda b, c: (b * n_chunks + c, 0)),
        out_shape=jax.ShapeDtypeStruct((bsz * s_len, WIDTH_C), BF16),
        scratch_shapes=[pltpu.VMEM((t_chunk + SUBLANES, WIDTH_C), F32),
                        pltpu.VMEM((t_chunk, 2 * N_STATE), F32),
                        pltpu.VMEM((SUBLANES, 2 * N_STATE), F32)],
        compiler_params=_cparams(("parallel", "arbitrary")),
        name="s5_scan_glu",
    )(c_u, w8, cd, l8, d_skip, w_glu_bf, b_glu)


def _merge_body(x_ref, oa_ref, o0_ref, o1_ref, o2_ref, l0_ref, l1_ref, l2_ref, oc_ref,
                g0_ref, g1_ref, g2_ref, wa_ref, wb_ref, wc_ref, wo_ref, tok_scr, lse_scr, tm):
    for gi, (o_ref, l_ref) in enumerate(((o1_ref, l1_ref), (o2_ref, l2_ref))):
        dil = DIL_PAIRS[gi + 1][1]
        for r in range(dil):
            lse_scr[gi, pl.ds(r, tm // dil, stride=dil), :] = l_ref[:, r * LANES:(r + 1) * LANES]
            for li in range(WIDTH_B // LANES):
                cols = slice(r * WIDTH_B + li * LANES, r * WIDTH_B + (li + 1) * LANES)
                tok_scr[gi, li, pl.ds(r, tm // dil, stride=dil), :] = o_ref[:, cols].astype(F32)
    tok = lambda k: jnp.concatenate([tok_scr[k, li] for li in range(WIDTH_B // LANES)], axis=1)
    l0, l1, l2 = l0_ref[...], lse_scr[0], lse_scr[1]
    m = jnp.maximum(jnp.maximum(l0, l1), l2)
    e0, e1, e2 = jnp.exp(l0 - m), jnp.exp(l1 - m), jnp.exp(l2 - m)
    tot = e0 + e1 + e2
    sel_r = lax.broadcasted_iota(jnp.int32, (LANES, WIDTH_B), 0)
    sel_c = lax.broadcasted_iota(jnp.int32, (LANES, WIDTH_B), 1)
    select = jnp.where(sel_r == LSE_LANES * (sel_c // HD_B), 1.0, 0.0).astype(BF16)

    def spread(w):
        hi = w.astype(BF16)
        lo = (w - hi.astype(F32)).astype(BF16)
        return (jnp.dot(hi, select, preferred_element_type=F32)
                + jnp.dot(lo, select, preferred_element_type=F32))

    ob = (spread(e0 / tot) * o0_ref[...].astype(F32) + spread(e1 / tot) * tok(0)
          + spread(e2 / tot) * tok(1))
    ya = jnp.dot(oa_ref[...], wa_ref[...], preferred_element_type=F32)
    yb = jnp.dot(ob.astype(BF16), wb_ref[...], preferred_element_type=F32)
    yc = jnp.dot(oc_ref[...], wc_ref[...], preferred_element_type=F32)
    merged = (g0_ref[...].astype(F32) * ya + g1_ref[...].astype(F32) * yb
              + g2_ref[...].astype(F32) * yc)
    return x_ref[...] + jnp.dot(merged.astype(BF16), wo_ref[...], preferred_element_type=F32)


def _ffn_conv_halo(a_scr, tm, tiles_per_seq):
    halo = SUBLANES

    @pl.when(pl.program_id(0) % tiles_per_seq == 0)
    def _():
        a_scr[0:halo, :] = jnp.zeros((halo, D_FF), F32)

    @pl.when(pl.program_id(0) % tiles_per_seq != 0)
    def _():
        a_scr[0:halo, :] = a_scr[tm:tm + halo, :]


def _ffn_body(x, g_ref, wup_ref, cw_ref, cb_ref, wdown_ref, o_ref, a_scr, tm):
    halo = SUBLANES
    ms = jnp.mean(x * x, axis=-1, keepdims=True)
    h = (x * lax.rsqrt(ms + EPS) * g_ref[...]).astype(BF16)

    n_tiles = D_FF // FFN_TILE
    tile_cols = lambda f: slice(f * FFN_TILE, (f + 1) * FFN_TILE)

    def up(f):
        a_scr[halo:halo + tm, tile_cols(f)] = jnp.dot(h, wup_ref[:, tile_cols(f)], preferred_element_type=F32)
        return jnp.dot(h, wup_ref[:, D_FF + f * FFN_TILE:D_FF + (f + 1) * FFN_TILE],
                       preferred_element_type=F32)

    y = x
    gate = up(0)
    for f in range(n_tiles):
        cols = tile_cols(f)
        next_gate = up(f + 1) if f + 1 < n_tiles else None
        conv = (cb_ref[:, cols] + cw_ref[0:1, cols] * a_scr[halo - 2:halo - 2 + tm, cols]
                + cw_ref[1:2, cols] * a_scr[halo - 1:halo - 1 + tm, cols]
                + cw_ref[2:3, cols] * a_scr[halo:halo + tm, cols])
        act = (conv * jax.nn.sigmoid(conv)) * gate
        y = y + jnp.dot(act.astype(BF16), wdown_ref[cols, :], preferred_element_type=F32)
        gate = next_gate
    o_ref[...] = y


N_MERGE_INPUTS = 16


def _merge_ffn_kernel(*refs, tm, tiles_per_seq):
    merge_in = refs[:N_MERGE_INPUTS]
    g_ref, wup_ref, cw_ref, cb_ref, wdown_ref, o_ref, tok_scr, lse_scr, a_scr = refs[N_MERGE_INPUTS:]
    _ffn_conv_halo(a_scr, tm, tiles_per_seq)
    x_mid = _merge_body(*merge_in, tok_scr, lse_scr, tm)
    _ffn_body(x_mid, g_ref, wup_ref, cw_ref, cb_ref, wdown_ref, o_ref, a_scr, tm)


def _merge_ffn(x2d, oa, ob_parts, lse_parts, oc, gates, wa, wb, wc, wo,
               norm_g, wup_bf, conv_w, conv_b, wdown_bf, layer, s_len, tm):
    n = x2d.shape[0]
    row = lambda i: (i, 0)
    const = lambda i: (0, 0)
    half = pl.BlockSpec((tm, WIDTH_C), row)
    full = pl.BlockSpec((tm, D_MODEL), row)
    single = pl.Buffered(1)
    wspec_half = pl.BlockSpec((None, WIDTH_C, D_MODEL), lambda i: (layer, 0, 0), pipeline_mode=single)
    gspec = lambda k: pl.BlockSpec((tm, D_MODEL), lambda i: (i, k))
    dilated = lambda g, width: pl.BlockSpec((tm // DIL_PAIRS[g][1], DIL_PAIRS[g][1] * width), row)
    merge_specs = [full, half, half, dilated(1, WIDTH_B), dilated(2, WIDTH_B),
                   pl.BlockSpec((tm, LANES), row), dilated(1, LANES), dilated(2, LANES), half,
                   gspec(0), gspec(1), gspec(2),
                   wspec_half, wspec_half, wspec_half,
                   pl.BlockSpec((None, D_MODEL, D_MODEL), lambda i: (layer, 0, 0), pipeline_mode=single)]
    assert len(merge_specs) == N_MERGE_INPUTS
    ffn_specs = [
        pl.BlockSpec((1, D_MODEL), const),
        pl.BlockSpec((None, D_MODEL, 2 * D_FF), lambda i: (layer, 0, 0), pipeline_mode=single),
        pl.BlockSpec((CONV_WIDTH, D_FF), const),
        pl.BlockSpec((1, D_FF), const),
        pl.BlockSpec((None, D_FF, D_MODEL), lambda i: (layer, 0, 0), pipeline_mode=single),
    ]
    kern = functools.partial(_merge_ffn_kernel, tm=tm, tiles_per_seq=s_len // tm)
    return pl.pallas_call(
        kern,
        grid=(n // tm,),
        in_specs=merge_specs + ffn_specs,
        out_specs=full,
        out_shape=jax.ShapeDtypeStruct((n, D_MODEL), F32),
        scratch_shapes=[pltpu.VMEM((2, WIDTH_B // LANES, tm, LANES), F32),
                        pltpu.VMEM((2, tm, LANES), F32),
                        pltpu.VMEM((tm + SUBLANES, D_FF), F32)],
        compiler_params=_cparams(("arbitrary",)),
        name="merge_conv_ffn",
    )(x2d, oa, *ob_parts, *lse_parts, oc, gates, gates, gates, wa, wb, wc, wo,
      norm_g, wup_bf, conv_w, conv_b, wdown_bf)


def _lane_order(head_dim):
    half = head_dim // ROPE_FRACTION // 2
    heads = LANES // head_dim
    quarter = ROT_SHIFT // heads
    first, second = [], []
    for hd in range(heads):
        base = hd * head_dim
        first += list(range(base, base + half)) + list(range(base + 2 * half, base + half + quarter))
        second += list(range(base + half, base + 2 * half)) + list(range(base + half + quarter, base + head_dim))
    return first + second


def _permute_groups(w_cols, head_dim):
    order = _lane_order(head_dim)
    groups = w_cols.shape[-1] // LANES
    src = jnp.asarray([g * LANES + o for g in range(groups) for o in order])
    perm = jnp.zeros((groups * LANES,) * 2, F32).at[src, jnp.arange(groups * LANES)].set(1.0).astype(w_cols.dtype)
    return jnp.dot(w_cols, perm, preferred_element_type=F32).astype(w_cols.dtype)


def _rope_tables(positions, head_dim):
    rot = head_dim // ROPE_FRACTION
    half = rot // 2
    heads = LANES // head_dim
    quarter = ROT_SHIFT // heads
    inv = ROPE_THETA ** (-jnp.arange(0, rot, 2, dtype=F32) / rot)
    ang = positions.reshape(-1).astype(F32)[:, None] * inv
    cos, sin = jnp.cos(ang), jnp.sin(ang)
    n = ang.shape[0]
    pad = quarter - half
    c_q = jnp.concatenate([cos, jnp.ones((n, pad), F32)], axis=1)
    s_q = jnp.concatenate([sin, jnp.zeros((n, pad), F32)], axis=1)
    c = jnp.tile(c_q, (1, 2 * heads))
    s = jnp.concatenate([jnp.tile(-s_q, (1, heads)), jnp.tile(s_q, (1, heads))], axis=1)
    return c, s


def kernel(x, positions, attn_norm_g, w_in, b_gate, qn_a, kn_a, lam_q1, lam_k1, lam_q2, lam_k2, subln_g, w_br_a, qn_b, kn_b, w_br_b, ssm_a_re, ssm_a_im, ssm_log_dt, ssm_b_re, ssm_b_im, ssm_c_re, ssm_c_im, ssm_d, w_glu, b_glu, w_br_c, w_out, ffn_norm_g, w_up, conv_w, conv_b, w_down):
    bsz, s_len, d_model = x.shape
    depth = w_in.shape[0]
    assert d_model == D_MODEL and w_in.shape[2] == IN_COLS
    assert s_len % DIL_PAIRS[-1][0] == 0, "sequence must be a multiple of the largest dilated window"
    n = bsz * s_len

    tm_proj = 256
    tq_a = min(512, s_len)
    tq_b = 1024
    t_ssm = min(1024, s_len)
    tm_ffn = min(256, s_len)

    rope = _rope_tables(positions, HD_A) + _rope_tables(positions, HD_B)
    w_in_bf = w_in.astype(BF16)
    w_rows = w_in_bf.reshape(depth * D_MODEL, IN_COLS)
    qk_chunk = lambda c0, hd: _permute_groups(w_rows[:, c0:c0 + QK_CHUNK], hd).reshape(depth, D_MODEL, QK_CHUNK)
    w_qk = [qk_chunk(0, HD_A)] + [qk_chunk(TILE_BQ * COL_TILE + k * QK_CHUNK, HD_B)
                                  for k in range(2 * B_COLS // QK_CHUNK)]
    order_a = jnp.asarray(_lane_order(HD_A))
    order_b = jnp.asarray(_lane_order(HD_B))
    wa_bf, wb_bf, wc_bf, wo_bf = (w.astype(BF16) for w in (w_br_a, w_br_b, w_br_c, w_out))
    wglu_bf, wup_bf, wdown_bf = w_glu.astype(BF16), w_up.astype(BF16), w_down.astype(BF16)

    w8, cd, l8 = jax.vmap(_ssm_operands)(ssm_a_re, ssm_a_im, ssm_log_dt, ssm_b_re, ssm_b_im, ssm_c_re, ssm_c_im)

    x2d = x.reshape(n, D_MODEL)
    for l in range(depth):
        lam_init = 0.8 - 0.6 * math.exp(-0.3 * l)
        ones = lambda w: jnp.ones((w,), F32)
        group_a = lambda g: jnp.tile(jnp.tile(g.astype(F32), LANES // HD_A)[order_a], A_Q_COLS // LANES)
        group_b = lambda g: jnp.tile(g.astype(F32)[order_b], B_COLS // LANES)
        col_gain = jnp.concatenate([
            group_a(qn_a[l]) * (math.log2(math.e) / math.sqrt(HD_A)), group_a(kn_a[l]), ones(WIDTH_A),
            group_b(qn_b[l]), group_b(kn_b[l]),
            ones(B_COLS + WIDTH_C + N_BRANCHES * D_MODEL)]).reshape(1, IN_COLS)
        col_bias = jnp.concatenate([jnp.zeros((IN_COLS - N_BRANCHES * D_MODEL,), F32),
                                    b_gate[l].astype(F32)]).reshape(1, IN_COLS)
        qk_a, v_t, *qkv_b, c_u, gates = _in_projection(
            x2d, attn_norm_g[l].reshape(1, D_MODEL).astype(F32), w_in_bf, w_qk, l, col_gain, col_bias, rope,
            bsz, s_len, tm_proj)

        lam_p = jnp.stack([lam_q1[l], lam_k1[l], lam_q2[l], lam_k2[l]]).astype(F32)
        oa = _diff_attention(qk_a.reshape(bsz, s_len, 2 * A_Q_COLS), v_t, lam_p,
                             subln_g[l].reshape(1, 2 * HD_A).astype(F32), lam_init, tq_a)
        oa = oa.reshape(n, WIDTH_A)

        ob_parts, lse_parts = zip(*[_dilated_attention(qkv_b[g], bsz, s_len, g, tq_b) for g in range(N_DIL)])

        oc = _ssm_branch(c_u, bsz, s_len, w8, cd, l8, ssm_d[l].reshape(1, WIDTH_C).astype(F32),
                         wglu_bf, l, b_glu[l].reshape(1, WIDTH_C).astype(F32), t_ssm)

        x2d = _merge_ffn(x2d, oa, ob_parts, lse_parts, oc, gates, wa_bf, wb_bf, wc_bf, wo_bf,
                         ffn_norm_g[l].reshape(1, D_MODEL).astype(F32), wup_bf,
                         conv_w[l].astype(F32), conv_b[l].reshape(1, D_FF).astype(F32), wdown_bf, l,
                         s_len, tm_ffn)
    return x2d.reshape(bsz, s_len, D_MODEL)
```

```python
import functools
import math

import jax
import jax.numpy as jnp
from jax import lax
from jax.experimental import pallas as pl
from jax.experimental.pallas import tpu as pltpu

F32 = jnp.float32
BF16 = jnp.bfloat16

LANES = 128
SUBLANES = 8
ROT_SHIFT = LANES // 2

D_MODEL = 1024
N_HEADS_A = 4
HD_A = 64
N_DIL = 3
DIL_PAIRS = ((128, 1), (512, 4), (2048, 16))
N_HEADS_B = 4
HD_B = 128
WIDTH_B = N_HEADS_B * HD_B
SSM_GROUP = 16
SSM_STATE = 64
WIDTH_C = 512
N_GROUPS_C = WIDTH_C // SSM_GROUP
N_STATE = N_GROUPS_C * SSM_STATE
N_BRANCHES = 3
D_FF = 2816
CONV_WIDTH = 3
ROPE_THETA = 500000.0
ROPE_FRACTION = 4
EPS = 1e-6

COL_TILE = 512
QK_CHUNK = 2 * COL_TILE
A_Q_COLS = 2 * N_HEADS_A * HD_A
WIDTH_A = N_HEADS_A * 2 * HD_A
B_COLS = N_DIL * N_HEADS_B * HD_B
IN_COLS = 2 * A_Q_COLS + WIDTH_A + 3 * B_COLS + WIDTH_C + N_BRANCHES * D_MODEL
N_COL_TILES = IN_COLS // COL_TILE
TILE_AQ, TILE_AK, TILE_AV = 0, 1, 2
TILE_BQ, TILE_BK, TILE_BV = 3, 6, 9
TILE_CU = 12

HEADS_PER_STEP = 4
N_SCORE_BUF = 4
AHEAD = N_SCORE_BUF - 1
VT_PAD = 16
VT_ROWS = 2 * HD_A + VT_PAD
LSE_LANES = LANES // N_HEADS_B
SSM_SLAB = 256
SSM_TAPS = SUBLANES
FFN_TILE = 256

VMEM_LIMIT = 56 * 1024 * 1024


def _cparams(sem):
    return pltpu.CompilerParams(dimension_semantics=sem, vmem_limit_bytes=VMEM_LIMIT)


def _first_head_lanes():
    lane = lax.broadcasted_iota(jnp.int32, (1, LANES), 1)
    return (lane // (ROT_SHIFT // 2)) % 2 == 0


def _norm_rope_tile(acc, gain, seg, c_ref, s_ref):
    cos = c_ref[...]
    sin = s_ref[...]
    first = _first_head_lanes()
    outs = []
    for gi in range(acc.shape[1] // LANES):
        y = acc[:, gi * LANES:(gi + 1) * LANES]
        ysq = y * y
        tot = jnp.sum(ysq, axis=-1, keepdims=True)
        if seg == LANES:
            ssum = tot
        else:
            one = jnp.sum(jnp.where(first, ysq, 0.0), axis=-1, keepdims=True)
            ssum = jnp.where(first, one, tot - one)
        yn = y * lax.rsqrt(ssum * (1.0 / seg) + EPS) * gain[:, gi * LANES:(gi + 1) * LANES]
        outs.append(yn * cos + pltpu.roll(yn, ROT_SHIFT, 1) * sin)
    return jnp.concatenate(outs, axis=1)


def _inproj_kernel(x_ref, g_ref, w_ref, wqa_ref, wqb0_ref, wqb1_ref, wqb2_ref, gain_ref, bias_ref,
                   ca_ref, sa_ref, cb_ref, sb_ref,
                   qk_ref, vt_ref, b0_ref, b1_ref, b2_ref, cu_ref, gate_ref, dil_scr, *, tm):
    x = x_ref[...]
    ms = jnp.mean(x * x, axis=-1, keepdims=True)
    h = (x * lax.rsqrt(ms + EPS) * g_ref[...]).astype(BF16)
    dil_refs = (b0_ref, b1_ref, b2_ref)
    heavy = [TILE_AQ, TILE_AK] + list(range(TILE_BQ, TILE_BV))
    light = [TILE_AV] + list(range(TILE_BV, TILE_CU + 1))
    order = heavy + [j for j in range(N_COL_TILES) if j not in heavy + light] + light
    wqb_refs = (wqb0_ref, wqb1_ref, wqb2_ref)

    def project(j):
        if j in (TILE_AQ, TILE_AK):
            ref, c0 = wqa_ref, j * COL_TILE
        elif TILE_BQ <= j < TILE_BV:
            off = (j - TILE_BQ) * COL_TILE
            ref, c0 = wqb_refs[off // QK_CHUNK], off % QK_CHUNK
        else:
            ref, c0 = w_ref, j * COL_TILE
        return jnp.dot(h, ref[:, c0:c0 + COL_TILE], preferred_element_type=F32)

    nxt = project(order[0])
    for idx, j in enumerate(order):
        cols = slice(j * COL_TILE, (j + 1) * COL_TILE)
        acc = nxt
        if idx + 1 < len(order):
            nxt = project(order[idx + 1])
        if j in (TILE_AQ, TILE_AK):
            qk_ref[:, cols] = _norm_rope_tile(acc, gain_ref[:, cols], HD_A, ca_ref, sa_ref).astype(BF16)
        elif j == TILE_AV:
            acc_t = acc.T.astype(BF16)
            for hd in range(N_HEADS_A):
                vt_ref[hd * VT_ROWS:hd * VT_ROWS + 2 * HD_A, :] = acc_t[hd * 2 * HD_A:(hd + 1) * 2 * HD_A, :]
                vt_ref[hd * VT_ROWS + 2 * HD_A:(hd + 1) * VT_ROWS, :] = jnp.ones((VT_PAD, tm), BF16)
        elif j < TILE_CU:
            part, group = divmod(j - TILE_BQ, N_DIL)
            if part < 2:
                acc = _norm_rope_tile(acc, gain_ref[:, cols], HD_B, cb_ref, sb_ref)
            dil = DIL_PAIRS[group][1]
            out_ref = dil_refs[group]
            if dil == 1:
                out_ref[:, part * WIDTH_B:(part + 1) * WIDTH_B] = acc.astype(BF16)
            else:
                slot = (group - 1) * 3 + part
                for gi in range(WIDTH_B // LANES):
                    dil_scr[slot, gi] = acc[:, gi * LANES:(gi + 1) * LANES]
                for r in range(dil):
                    for gi in range(WIDTH_B // LANES):
                        c0 = (r * 3 + part) * WIDTH_B + gi * LANES
                        out_ref[:, c0:c0 + LANES] = (
                            dil_scr[slot, gi, pl.ds(r, tm // dil, stride=dil), :].astype(BF16))
        elif j == TILE_CU:
            cu_ref[...] = acc
        else:
            g0 = (j - TILE_CU - 1) * COL_TILE
            gate_ref[:, g0:g0 + COL_TILE] = jax.nn.sigmoid(acc + bias_ref[:, cols]).astype(gate_ref.dtype)


def _in_projection(x2d, norm_g, w_bf, w_qk, layer, col_gain, col_bias, rope, bsz, s_len, tm):
    n = x2d.shape[0]
    tiles_per_seq = s_len // tm
    row = lambda i: (i, 0)
    const = lambda i: (0, 0)
    d1, d2 = DIL_PAIRS[1][1], DIL_PAIRS[2][1]
    kern = functools.partial(_inproj_kernel, tm=tm)
    resident = lambda cols: pl.BlockSpec((None, D_MODEL, cols), lambda i: (layer, 0, 0),
                                         pipeline_mode=pl.Buffered(1))
    return pl.pallas_call(
        kern,
        grid=(n // tm,),
        in_specs=[
            pl.BlockSpec((tm, D_MODEL), row),
            pl.BlockSpec((1, D_MODEL), const),
            resident(IN_COLS), resident(QK_CHUNK), resident(QK_CHUNK), resident(QK_CHUNK), resident(QK_CHUNK),
            pl.BlockSpec((None, 1, IN_COLS), lambda i: (layer, 0, 0)),
            pl.BlockSpec((None, 1, IN_COLS), lambda i: (layer, 0, 0)),
        ] + [pl.BlockSpec((tm, LANES), row)] * len(rope),
        out_specs=[
            pl.BlockSpec((tm, 2 * A_Q_COLS), row),
            pl.BlockSpec((None, N_HEADS_A * VT_ROWS, tm), lambda i: (i // tiles_per_seq, 0, i % tiles_per_seq)),
            pl.BlockSpec((tm, 3 * WIDTH_B), row),
            pl.BlockSpec((tm // d1, d1 * 3 * WIDTH_B), row),
            pl.BlockSpec((tm // d2, d2 * 3 * WIDTH_B), row),
            pl.BlockSpec((tm, WIDTH_C), row),
            pl.BlockSpec((tm, N_BRANCHES * D_MODEL), row),
        ],
        out_shape=[jax.ShapeDtypeStruct((n, 2 * A_Q_COLS), BF16),
                   jax.ShapeDtypeStruct((bsz, N_HEADS_A * VT_ROWS, s_len), BF16),
                   jax.ShapeDtypeStruct((n, 3 * WIDTH_B), BF16),
                   jax.ShapeDtypeStruct((n // d1, d1 * 3 * WIDTH_B), BF16),
                   jax.ShapeDtypeStruct((n // d2, d2 * 3 * WIDTH_B), BF16),
                   jax.ShapeDtypeStruct((n, WIDTH_C), F32),
                   jax.ShapeDtypeStruct((n, N_BRANCHES * D_MODEL), BF16)],
        scratch_shapes=[pltpu.VMEM((6, WIDTH_B // LANES, tm, LANES), F32)],
        compiler_params=_cparams(("parallel",)),
        name="in_projection",
    )(x2d, norm_g, w_bf, *w_qk, col_gain, col_bias, *rope)


def _diffattn_kernel(q_ref, k_ref, vt_ref, mask_ref, lam_ref, subg_ref, o_ref, acc_scr, s_scr, *, tq, lam_init):
    i = pl.program_id(2)
    q = q_ref[...]
    first = _first_head_lanes()
    nt = (((1,), (1,)), ((), ()))
    qm = []
    for hh in range(HEADS_PER_STEP):
        qh = q[:, hh * 2 * HD_A:(hh + 1) * 2 * HD_A]
        zero = jnp.zeros_like(qh)
        qm += [jnp.where(first, qh, zero), jnp.where(first, zero, qh)]
    n_maps = len(qm)
    acc_scr[...] = jnp.zeros_like(acc_scr)

    def score(j, mi):
        r0 = pl.multiple_of(j * tq, tq)
        hh = mi // 2
        kblk = k_ref[pl.ds(r0, tq), hh * 2 * HD_A:(hh + 1) * 2 * HD_A]
        s_scr[mi % N_SCORE_BUF] = lax.dot_general(kblk, qm[mi], nt, preferred_element_type=F32)

    def consume(j, mi, m_old, diagonal):
        r0 = pl.multiple_of(j * tq, tq)
        hh = mi // 2
        vtblk = vt_ref[hh * VT_ROWS:(hh + 1) * VT_ROWS, pl.ds(r0, tq)]
        st = s_scr[mi % N_SCORE_BUF]
        if diagonal:
            st = st + mask_ref[...]
        m_new = jnp.maximum(m_old, jnp.max(st, axis=0, keepdims=True))
        p = jnp.exp2(st - m_new)
        alpha = jnp.exp2(m_old - m_new)
        acc_scr[mi] = alpha * acc_scr[mi] + jnp.dot(vtblk, p.astype(BF16), preferred_element_type=F32)
        return m_new

    def stage(j, ms, diagonal):
        new = []
        for mi in range(n_maps):
            ahead = mi + AHEAD
            if ahead < n_maps:
                score(j, ahead)
            elif not diagonal:
                score(j + 1, ahead - n_maps)
            new.append(consume(j, mi, ms[mi], diagonal))
        return tuple(new)

    init = tuple(jnp.full((1, tq), -jnp.inf, F32) for _ in range(n_maps))

    for mi in range(AHEAD):
        score(0, mi)
    ms = lax.fori_loop(0, i // 2, lambda t, c: stage(2 * t + 1, stage(2 * t, c, False), False), init)
    odd = lax.rem(i, 2) == 1

    @pl.when(odd)
    def _():
        stage(i, stage(i - 1, ms, False), True)

    @pl.when(jnp.logical_not(odd))
    def _():
        stage(i, ms, True)

    lam_p = lam_ref[...]
    lam = (jnp.exp(jnp.sum(lam_p[0:1] * lam_p[1:2], axis=-1, keepdims=True))
           - jnp.exp(jnp.sum(lam_p[2:3] * lam_p[3:4], axis=-1, keepdims=True)) + lam_init)
    vals = slice(0, 2 * HD_A)
    den = slice(2 * HD_A, 2 * HD_A + 1)
    for hh in range(HEADS_PER_STEP):
        a1, a2 = acc_scr[2 * hh], acc_scr[2 * hh + 1]
        o_t = a1[vals] / a1[den] - lam * (a2[vals] / a2[den])
        msq = jnp.mean(o_t * o_t, axis=0, keepdims=True)
        o = (o_t * lax.rsqrt(msq + EPS)).T
        o_ref[:, hh * 2 * HD_A:(hh + 1) * 2 * HD_A] = ((o * subg_ref[...]) * (1.0 - lam_init)).astype(o_ref.dtype)


def _diff_attention(qk, v_t, lam_p, subln_g, lam_init, tq):
    bsz, s_len, _ = qk.shape
    width = HEADS_PER_STEP * 2 * HD_A
    kern = functools.partial(_diffattn_kernel, tq=tq, lam_init=lam_init)
    key = lax.broadcasted_iota(jnp.int32, (tq, tq), 0)
    qry = lax.broadcasted_iota(jnp.int32, (tq, tq), 1)
    causal = jnp.where(key <= qry, 0.0, -jnp.inf).astype(F32)
    return pl.pallas_call(
        kern,
        grid=(bsz, N_HEADS_A // HEADS_PER_STEP, s_len // tq),
        in_specs=[
            pl.BlockSpec((None, tq, width), lambda b, h, i: (b, i, h)),
            pl.BlockSpec((None, s_len, width), lambda b, h, i: (b, 0, N_HEADS_A // HEADS_PER_STEP + h)),
            pl.BlockSpec((None, HEADS_PER_STEP * VT_ROWS, s_len), lambda b, h, i: (b, h, 0)),
            pl.BlockSpec((tq, tq), lambda b, h, i: (0, 0)),
            pl.BlockSpec((4, HD_A), lambda b, h, i: (0, 0)),
            pl.BlockSpec((1, 2 * HD_A), lambda b, h, i: (0, 0)),
        ],
        out_specs=pl.BlockSpec((None, tq, width), lambda b, h, i: (b, i, h)),
        out_shape=jax.ShapeDtypeStruct((bsz, s_len, WIDTH_A), BF16),
        scratch_shapes=[pltpu.VMEM((2 * HEADS_PER_STEP, VT_ROWS, tq), F32),
                        pltpu.VMEM((N_SCORE_BUF, tq, tq), F32)],
        compiler_params=_cparams(("parallel", "parallel", "arbitrary")),
        name="diff_attention",
    )(qk, qk, v_t, causal, lam_p, subln_g)


def _dilated_kernel(q_ref, k_ref, kp_ref, v_ref, vp_ref, o_ref, lse_ref, *, tq, blk):
    n = pl.program_id(2)
    scale = 1.0 / math.sqrt(HD_B)
    nt = (((1,), (1,)), ((), ()))
    rr = lax.broadcasted_iota(jnp.int32, (blk, 2 * blk), 0)
    cc = lax.broadcasted_iota(jnp.int32, (blk, 2 * blk), 1)
    band = jnp.logical_and(cc >= rr, cc <= rr + blk)
    lane = lax.broadcasted_iota(jnp.int32, (1, 2 * blk), 1)
    first_bias = jnp.where(lane >= blk, 0.0, jnp.where(n > 0, 0.0, -jnp.inf).astype(F32))

    def window(ref, pref, c, cols):
        if c == 0:
            return jnp.concatenate([pref[:, cols], ref[0:blk, cols]], axis=0)
        return ref[(c - 1) * blk:(c + 1) * blk, cols]

    def scores(unit):
        h, c = unit
        cols = slice(h * HD_B, (h + 1) * HD_B)
        qh = q_ref[c * blk:(c + 1) * blk, cols]
        s = lax.dot_general(qh, window(k_ref, kp_ref, c, cols), nt, preferred_element_type=F32) * scale
        s = jnp.where(band, s, -jnp.inf)
        return s + first_bias if c == 0 else s

    head_lane = lax.broadcasted_iota(jnp.int32, (1, LANES), 1) // LSE_LANES

    def finish(unit, s, lse_acc):
        h, c = unit
        cols = slice(h * HD_B, (h + 1) * HD_B)
        rows = slice(c * blk, (c + 1) * blk)
        m = jnp.max(s, axis=-1, keepdims=True)
        p = jnp.exp(s - m)
        den = jnp.sum(p, axis=-1, keepdims=True)
        pv = jnp.dot(p.astype(BF16), window(v_ref, vp_ref, c, cols), preferred_element_type=F32)
        o_ref[rows, cols] = (pv / den).astype(o_ref.dtype)
        lse = jnp.broadcast_to(m + jnp.log(den), (blk, LANES))
        lse_acc = lse if h == 0 else jnp.where(head_lane == h, lse, lse_acc)
        if h == N_HEADS_B - 1:
            lse_ref[rows, :] = lse_acc
        return lse_acc

    units = [(h, c) for c in range(tq // blk) for h in range(N_HEADS_B)]
    s = scores(units[0])
    lse_acc = None
    for idx, unit in enumerate(units):
        s_next = scores(units[idx + 1]) if idx + 1 < len(units) else None
        lse_acc = finish(unit, s, lse_acc)
        s = s_next


def _dilated_attention(qkv, bsz, s_len, group, tq):
    window, dil = DIL_PAIRS[group]
    blk = window // dil
    rows = s_len // dil
    tq = min(tq, rows)
    per_res = 3
    view = qkv.reshape(bsz, rows, dil * per_res * WIDTH_B)
    sub = tq // blk
    qcol = lambda r: r * per_res
    kcol = lambda r: r * per_res + 1
    vcol = lambda r: r * per_res + 2
    prev = lambda n: jnp.maximum(n * sub - 1, 0)
    kern = functools.partial(_dilated_kernel, tq=tq, blk=blk)
    o, lse = pl.pallas_call(
        kern,
        grid=(bsz, dil, rows // tq),
        in_specs=[
            pl.BlockSpec((None, tq, WIDTH_B), lambda b, r, n: (b, n, qcol(r))),
            pl.BlockSpec((None, tq, WIDTH_B), lambda b, r, n: (b, n, kcol(r))),
            pl.BlockSpec((None, blk, WIDTH_B), lambda b, r, n: (b, prev(n), kcol(r))),
            pl.BlockSpec((None, tq, WIDTH_B), lambda b, r, n: (b, n, vcol(r))),
            pl.BlockSpec((None, blk, WIDTH_B), lambda b, r, n: (b, prev(n), vcol(r))),
        ],
        out_specs=[pl.BlockSpec((None, tq, WIDTH_B), lambda b, r, n: (b, n, r)),
                   pl.BlockSpec((None, tq, LANES), lambda b, r, n: (b, n, r))],
        out_shape=[jax.ShapeDtypeStruct((bsz, rows, dil * WIDTH_B), BF16),
                   jax.ShapeDtypeStruct((bsz, rows, dil * LANES), F32)],
        compiler_params=_cparams(("parallel", "parallel", "arbitrary")),
        name=f"dilated_attention_g{group}",
    )(view, view, view, view, view)
    return o.reshape(bsz * rows, dil * WIDTH_B), lse.reshape(bsz * rows, dil * LANES)


def _ssm_kernel(u_ref, w8_ref, cd_ref, l8_ref, dskip_ref, wglu_ref, bglu_ref, o_ref,
                u_scr, x_scr, carry_scr, *, t_chunk):
    halo = SUBLANES

    @pl.when(pl.program_id(1) == 0)
    def _():
        carry_scr[...] = jnp.zeros_like(carry_scr)
        u_scr[0:halo, :] = jnp.zeros((halo, WIDTH_C), F32)

    @pl.when(pl.program_id(1) != 0)
    def _():
        u_scr[0:halo, :] = u_scr[t_chunk:t_chunk + halo, :]

    u = u_ref[...]
    u_scr[halo:halo + t_chunk, :] = u
    re = slice(0, N_STATE)
    im = slice(N_STATE, 2 * N_STATE)
    ch = WIDTH_C // 2
    sh = N_STATE // 2
    blk_ch = LANES // 2
    blk_st = blk_ch * SSM_STATE // SSM_GROUP
    low = lax.broadcasted_iota(jnp.int32, (1, LANES), 1) < blk_ch
    c_re, c_im = carry_scr[:, re], carry_scr[:, im]
    a8, b8 = l8_ref[:, re], l8_ref[:, im]
    n_slabs = t_chunk // SSM_SLAB
    slab_rows = lambda s: slice(s * SSM_SLAB, (s + 1) * SSM_SLAB)

    def drive(s):
        r0 = halo + s * SSM_SLAB
        for pair in range(WIDTH_C // LANES):
            lanes = slice(pair * LANES, (pair + 1) * LANES)
            taps = [u_scr[r0 - t:r0 - t + SSM_SLAB, lanes] for t in range(SSM_TAPS)]
            for half in range(2):
                cb = 2 * pair + half
                pieces = []
                for a in range(SSM_TAPS // 2):
                    even, odd = taps[2 * a], taps[2 * a + 1]
                    if half == 0:
                        v = jnp.where(low, even, pltpu.roll(odd, blk_ch, 1))
                    else:
                        v = jnp.where(low, pltpu.roll(even, blk_ch, 1), odd)
                    pieces.append(v.astype(BF16))
                w = jnp.dot(jnp.concatenate(pieces, axis=1), w8_ref[cb], preferred_element_type=F32)
                x_scr[slab_rows(s), cb * blk_st:(cb + 1) * blk_st] = w[:, :blk_st]
                x_scr[slab_rows(s), N_STATE + cb * blk_st:N_STATE + (cb + 1) * blk_st] = w[:, blk_st:]

    drive(0)
    if n_slabs > 1:
        drive(1)
    for s in range(n_slabs):
        rows = slab_rows(s)
        if s + 2 < n_slabs:
            drive(s + 2)
        for t in range(SSM_SLAB // SUBLANES):
            r8 = slice(s * SSM_SLAB + t * SUBLANES, s * SSM_SLAB + (t + 1) * SUBLANES)
            x_re = x_scr[r8, re] + (a8 * c_re - b8 * c_im)
            x_im = x_scr[r8, im] + (a8 * c_im + b8 * c_re)
            x_scr[r8, re] = x_re
            x_scr[r8, im] = x_im
            c_re, c_im = x_re, x_im
        halves = []
        for hh in range(2):
            acc = None
            for part in range(2):
                c0 = part * N_STATE + hh * sh
                d = jnp.dot(x_scr[rows, c0:c0 + sh].astype(BF16), cd_ref[c0:c0 + sh, hh * ch:(hh + 1) * ch],
                            preferred_element_type=F32)
                acc = d if acc is None else acc + d
            halves.append(acc)
        y = jnp.concatenate(halves, axis=1) + dskip_ref[...] * u[rows]
        cdf = 0.5 * (1.0 + jnp.tanh(math.sqrt(2.0 / math.pi) * (y + 0.044715 * (y * y * y))))
        z = y * cdf
        gate = jnp.dot(z.astype(BF16), wglu_ref[...], preferred_element_type=F32) + bglu_ref[...]
        o_ref[rows, :] = (z * jax.nn.sigmoid(gate)).astype(o_ref.dtype)
    carry_scr[:, re] = c_re
    carry_scr[:, im] = c_im


def _ssm_operands(a_re, a_im, log_dt, b_re, b_im, c_re, c_im):
    a_re, a_im = a_re.astype(F32), a_im.astype(F32)
    dt = jnp.exp(log_dt.astype(F32))[:, None]
    mag = jnp.exp(a_re * dt)
    lb_re, lb_im = mag * jnp.cos(a_im * dt), mag * jnp.sin(a_im * dt)
    n_re, n_im = lb_re - 1.0, lb_im
    den = a_re * a_re + a_im * a_im
    f_re = (n_re * a_re + n_im * a_im) / den
    f_im = (n_im * a_re - n_re * a_im) / den
    b_re, b_im = b_re.astype(F32), b_im.astype(F32)
    bb_re = f_re[..., None] * b_re - f_im[..., None] * b_im
    bb_im = f_re[..., None] * b_im + f_im[..., None] * b_re
    def blockdiag_out(t):
        flat = t.transpose(0, 2, 1).reshape(N_STATE, SSM_GROUP).astype(BF16)
        col = jnp.arange(WIDTH_C)
        spread = (col[None, :] % SSM_GROUP == jnp.arange(SSM_GROUP)[:, None]).astype(BF16)
        full = jnp.dot(flat, spread, preferred_element_type=F32)
        keep = jnp.arange(N_STATE)[:, None] // SSM_STATE == col[None, :] // SSM_GROUP
        return jnp.where(keep, full, 0.0)

    cd = jnp.concatenate([blockdiag_out(c_re.astype(F32)), -blockdiag_out(c_im.astype(F32))],
                         axis=0).astype(BF16)

    def power(k):
        return (mag ** k) * jnp.cos(a_im * dt * k), (mag ** k) * jnp.sin(a_im * dt * k)

    taps = []
    for s in range(SSM_TAPS):
        p_re, p_im = power(float(s))
        taps.append(jnp.stack([p_re[..., None] * bb_re - p_im[..., None] * bb_im,
                               p_re[..., None] * bb_im + p_im[..., None] * bb_re], axis=-1))
    grp = (LANES // 2) // SSM_GROUP
    n_blk = N_GROUPS_C // grp
    k_blk = SSM_TAPS * grp * SSM_GROUP
    n_col = 2 * grp * SSM_STATE
    v = jnp.stack(taps, axis=0).reshape(SSM_TAPS, n_blk, grp, SSM_STATE, SSM_GROUP, 2)
    dense = v.transpose(1, 0, 2, 4, 5, 3).reshape(n_blk, k_blk, 2 * SSM_STATE).astype(BF16)
    col = jnp.arange(n_col)
    src = (col // (grp * SSM_STATE)) * SSM_STATE + col % SSM_STATE
    spread = (src[None, :] == jnp.arange(2 * SSM_STATE)[:, None]).astype(BF16)
    full = jnp.einsum('bkd,dn->bkn', dense, spread, preferred_element_type=F32)
    row_grp = (jnp.arange(k_blk) // SSM_GROUP) % grp
    col_grp = (col // SSM_STATE) % grp
    w8 = jnp.where(row_grp[:, None] == col_grp[None, :], full, 0.0).astype(BF16)
    p_re, p_im = power(float(SSM_TAPS))
    l8 = jnp.concatenate([p_re.reshape(1, N_STATE), p_im.reshape(1, N_STATE)], axis=1)
    return w8, cd, jnp.broadcast_to(l8, (SUBLANES, 2 * N_STATE))


def _ssm_branch(c_u, bsz, s_len, w8, cd, l8, d_skip, w_glu_bf, layer, b_glu, t_chunk):
    n_chunks = s_len // t_chunk
    const2 = lambda b, c: (0, 0)
    kern = functools.partial(_ssm_kernel, t_chunk=t_chunk)
    return pl.pallas_call(
        kern,
        grid=(bsz, n_chunks),
        in_specs=[
            pl.BlockSpec((t_chunk, WIDTH_C), lambda b, c: (b * n_chunks + c, 0)),
            pl.BlockSpec((None,) + w8.shape[1:], lambda b, c: (layer, 0, 0, 0)),
            pl.BlockSpec((None, 2 * N_STATE, WIDTH_C), lambda b, c: (layer, 0, 0)),
            pl.BlockSpec((None, SUBLANES, 2 * N_STATE), lambda b, c: (layer, 0, 0)),
            pl.BlockSpec((1, WIDTH_C), const2),
            pl.BlockSpec((None, WIDTH_C, WIDTH_C), lambda b, c: (layer, 0, 0)),
            pl.BlockSpec((1, WIDTH_C), const2),
        ],
        out_specs=pl.BlockSpec((t_chunk, WIDTH_C), lambda b, c: (b * n_chunks + c, 0)),
        out_shape=jax.ShapeDtypeStruct((bsz * s_len, WIDTH_C), BF16),
        scratch_shapes=[pltpu.VMEM((t_chunk + SUBLANES, WIDTH_C), F32),
                        pltpu.VMEM((t_chunk, 2 * N_STATE), F32),
                        pltpu.VMEM((SUBLANES, 2 * N_STATE), F32)],
        compiler_params=_cparams(("parallel", "arbitrary")),
        name="s5_scan_glu",
    )(c_u, w8, cd, l8, d_skip, w_glu_bf, b_glu)


def _merge_body(x_ref, oa_ref, o0_ref, o1_ref, o2_ref, l0_ref, l1_ref, l2_ref, oc_ref,
                g0_ref, g1_ref, g2_ref, wa_ref, wb_ref, wc_ref, wo_ref, tok_scr, lse_scr, tm):
    for gi, (o_ref, l_ref) in enumerate(((o1_ref, l1_ref), (o2_ref, l2_ref))):
        dil = DIL_PAIRS[gi + 1][1]
        for r in range(dil):
            lse_scr[gi, pl.ds(r, tm // dil, stride=dil), :] = l_ref[:, r * LANES:(r + 1) * LANES]
            for li in range(WIDTH_B // LANES):
                cols = slice(r * WIDTH_B + li * LANES, r * WIDTH_B + (li + 1) * LANES)
                tok_scr[gi, li, pl.ds(r, tm // dil, stride=dil), :] = o_ref[:, cols].astype(F32)
    tok = lambda k: jnp.concatenate([tok_scr[k, li] for li in range(WIDTH_B // LANES)], axis=1)
    l0, l1, l2 = l0_ref[...], lse_scr[0], lse_scr[1]
    m = jnp.maximum(jnp.maximum(l0, l1), l2)
    e0, e1, e2 = jnp.exp(l0 - m), jnp.exp(l1 - m), jnp.exp(l2 - m)
    tot = e0 + e1 + e2
    sel_r = lax.broadcasted_iota(jnp.int32, (LANES, WIDTH_B), 0)
    sel_c = lax.broadcasted_iota(jnp.int32, (LANES, WIDTH_B), 1)
    select = jnp.where(sel_r == LSE_LANES * (sel_c // HD_B), 1.0, 0.0).astype(BF16)

    def spread(w):
        hi = w.astype(BF16)
        lo = (w - hi.astype(F32)).astype(BF16)
        return (jnp.dot(hi, select, preferred_element_type=F32)
                + jnp.dot(lo, select, preferred_element_type=F32))

    ob = (spread(e0 / tot) * o0_ref[...].astype(F32) + spread(e1 / tot) * tok(0)
          + spread(e2 / tot) * tok(1))
    ya = jnp.dot(oa_ref[...], wa_ref[...], preferred_element_type=F32)
    yb = jnp.dot(ob.astype(BF16), wb_ref[...], preferred_element_type=F32)
    yc = jnp.dot(oc_ref[...], wc_ref[...], preferred_element_type=F32)
    merged = (g0_ref[...].astype(F32) * ya + g1_ref[...].astype(F32) * yb
              + g2_ref[...].astype(F32) * yc)
    return x_ref[...] + jnp.dot(merged.astype(BF16), wo_ref[...], preferred_element_type=F32)


def _ffn_conv_halo(a_scr, tm, tiles_per_seq):
    halo = SUBLANES

    @pl.when(pl.program_id(0) % tiles_per_seq == 0)
    def _():
        a_scr[0:halo, :] = jnp.zeros((halo, D_FF), F32)

    @pl.when(pl.program_id(0) % tiles_per_seq != 0)
    def _():
        a_scr[0:halo, :] = a_scr[tm:tm + halo, :]


def _ffn_body(x, g_ref, wup_ref, cw_ref, cb_ref, wdown_ref, o_ref, a_scr, tm):
    halo = SUBLANES
    ms = jnp.mean(x * x, axis=-1, keepdims=True)
    h = (x * lax.rsqrt(ms + EPS) * g_ref[...]).astype(BF16)

    n_tiles = D_FF // FFN_TILE
    tile_cols = lambda f: slice(f * FFN_TILE, (f + 1) * FFN_TILE)

    def up(f):
        a_scr[halo:halo + tm, tile_cols(f)] = jnp.dot(h, wup_ref[:, tile_cols(f)], preferred_element_type=F32)
        return jnp.dot(h, wup_ref[:, D_FF + f * FFN_TILE:D_FF + (f + 1) * FFN_TILE],
                       preferred_element_type=F32)

    y = x
    gate = up(0)
    for f in range(n_tiles):
        cols = tile_cols(f)
        next_gate = up(f + 1) if f + 1 < n_tiles else None
        conv = (cb_ref[:, cols] + cw_ref[0:1, cols] * a_scr[halo - 2:halo - 2 + tm, cols]
                + cw_ref[1:2, cols] * a_scr[halo - 1:halo - 1 + tm, cols]
                + cw_ref[2:3, cols] * a_scr[halo:halo + tm, cols])
        act = (conv * jax.nn.sigmoid(conv)) * gate
        y = y + jnp.dot(act.astype(BF16), wdown_ref[cols, :], preferred_element_type=F32)
        gate = next_gate
    o_ref[...] = y


N_MERGE_INPUTS = 16


def _merge_ffn_kernel(*refs, tm, tiles_per_seq):
    merge_in = refs[:N_MERGE_INPUTS]
    g_ref, wup_ref, cw_ref, cb_ref, wdown_ref, o_ref, tok_scr, lse_scr, a_scr = refs[N_MERGE_INPUTS:]
    _ffn_conv_halo(a_scr, tm, tiles_per_seq)
    x_mid = _merge_body(*merge_in, tok_scr, lse_scr, tm)
    _ffn_body(x_mid, g_ref, wup_ref, cw_ref, cb_ref, wdown_ref, o_ref, a_scr, tm)


def _merge_ffn(x2d, oa, ob_parts, lse_parts, oc, gates, wa, wb, wc, wo,
               norm_g, wup_bf, conv_w, conv_b, wdown_bf, layer, s_len, tm):
    n = x2d.shape[0]
    row = lambda i: (i, 0)
    const = lambda i: (0, 0)
    half = pl.BlockSpec((tm, WIDTH_C), row)
    full = pl.BlockSpec((tm, D_MODEL), row)
    single = pl.Buffered(1)
    wspec_half = pl.BlockSpec((None, WIDTH_C, D_MODEL), lambda i: (layer, 0, 0), pipeline_mode=single)
    gspec = lambda k: pl.BlockSpec((tm, D_MODEL), lambda i: (i, k))
    dilated = lambda g, width: pl.BlockSpec((tm // DIL_PAIRS[g][1], DIL_PAIRS[g][1] * width), row)
    merge_specs = [full, half, half, dilated(1, WIDTH_B), dilated(2, WIDTH_B),
                   pl.BlockSpec((tm, LANES), row), dilated(1, LANES), dilated(2, LANES), half,
                   gspec(0), gspec(1), gspec(2),
                   wspec_half, wspec_half, wspec_half,
                   pl.BlockSpec((None, D_MODEL, D_MODEL), lambda i: (layer, 0, 0), pipeline_mode=single)]
    assert len(merge_specs) == N_MERGE_INPUTS
    ffn_specs = [
        pl.BlockSpec((1, D_MODEL), const),
        pl.BlockSpec((None, D_MODEL, 2 * D_FF), lambda i: (layer, 0, 0), pipeline_mode=single),
        pl.BlockSpec((CONV_WIDTH, D_FF), const),
        pl.BlockSpec((1, D_FF), const),
        pl.BlockSpec((None, D_FF, D_MODEL), lambda i: (layer, 0, 0), pipeline_mode=single),
    ]
    kern = functools.partial(_merge_ffn_kernel, tm=tm, tiles_per_seq=s_len // tm)
    return pl.pallas_call(
        kern,
        grid=(n // tm,),
        in_specs=merge_specs + ffn_specs,
        out_specs=full,
        out_shape=jax.ShapeDtypeStruct((n, D_MODEL), F32),
        scratch_shapes=[pltpu.VMEM((2, WIDTH_B // LANES, tm, LANES), F32),
                        pltpu.VMEM((2, tm, LANES), F32),
                        pltpu.VMEM((tm + SUBLANES, D_FF), F32)],
        compiler_params=_cparams(("arbitrary",)),
        name="merge_conv_ffn",
    )(x2d, oa, *ob_parts, *lse_parts, oc, gates, gates, gates, wa, wb, wc, wo,
      norm_g, wup_bf, conv_w, conv_b, wdown_bf)


def _lane_order(head_dim):
    half = head_dim // ROPE_FRACTION // 2
    heads = LANES // head_dim
    quarter = ROT_SHIFT // heads
    first, second = [], []
    for hd in range(heads):
        base = hd * head_dim
        first += list(range(base, base + half)) + list(range(base + 2 * half, base + half + quarter))
        second += list(range(base + half, base + 2 * half)) + list(range(base + half + quarter, base + head_dim))
    return first + second


def _permute_groups(w_cols, head_dim):
    order = _lane_order(head_dim)
    groups = w_cols.shape[-1] // LANES
    src = jnp.asarray([g * LANES + o for g in range(groups) for o in order])
    return jnp.take(w_cols, src, axis=1)


def _rope_tables(positions, head_dim):
    rot = head_dim // ROPE_FRACTION
    half = rot // 2
    heads = LANES // head_dim
    quarter = ROT_SHIFT // heads
    inv = ROPE_THETA ** (-jnp.arange(0, rot, 2, dtype=F32) / rot)
    ang = positions.reshape(-1).astype(F32)[:, None] * inv
    cos, sin = jnp.cos(ang), jnp.sin(ang)
    n = ang.shape[0]
    pad = quarter - half
    c_q = jnp.concatenate([cos, jnp.ones((n, pad), F32)], axis=1)
    s_q = jnp.concatenate([sin, jnp.zeros((n, pad), F32)], axis=1)
    c = jnp.tile(c_q, (1, 2 * heads))
    s = jnp.concatenate([jnp.tile(-s_q, (1, heads)), jnp.tile(s_q, (1, heads))], axis=1)
    return c, s


def kernel(x, positions, attn_norm_g, w_in, b_gate, qn_a, kn_a, lam_q1, lam_k1, lam_q2, lam_k2, subln_g, w_br_a, qn_b, kn_b, w_br_b, ssm_a_re, ssm_a_im, ssm_log_dt, ssm_b_re, ssm_b_im, ssm_c_re, ssm_c_im, ssm_d, w_glu, b_glu, w_br_c, w_out, ffn_norm_g, w_up, conv_w, conv_b, w_down):
    bsz, s_len, d_model = x.shape
    depth = w_in.shape[0]
    assert d_model == D_MODEL and w_in.shape[2] == IN_COLS
    assert s_len % DIL_PAIRS[-1][0] == 0, "sequence must be a multiple of the largest dilated window"
    n = bsz * s_len

    tm_proj = 256
    tq_a = min(512, s_len)
    tq_b = 1024
    t_ssm = min(1024, s_len)
    tm_ffn = min(256, s_len)

    rope = _rope_tables(positions, HD_A) + _rope_tables(positions, HD_B)
    w_in_bf = w_in.astype(BF16)
    w_rows = w_in_bf.reshape(depth * D_MODEL, IN_COLS)
    qk_chunk = lambda c0, hd: _permute_groups(w_rows[:, c0:c0 + QK_CHUNK], hd).reshape(depth, D_MODEL, QK_CHUNK)
    w_qk = [qk_chunk(0, HD_A)] + [qk_chunk(TILE_BQ * COL_TILE + k * QK_CHUNK, HD_B)
                                  for k in range(2 * B_COLS // QK_CHUNK)]
    order_a = jnp.asarray(_lane_order(HD_A))
    order_b = jnp.asarray(_lane_order(HD_B))
    group_a = lambda g: jnp.tile(jnp.tile(g.astype(F32), (1, LANES // HD_A))[:, order_a], (1, A_Q_COLS // LANES))
    group_b = lambda g: jnp.tile(g.astype(F32)[:, order_b], (1, B_COLS // LANES))
    ones = lambda w: jnp.ones((depth, w), F32)
    col_gain = jnp.concatenate([
        group_a(qn_a) * (math.log2(math.e) / math.sqrt(HD_A)), group_a(kn_a), ones(WIDTH_A),
        group_b(qn_b), group_b(kn_b),
        ones(B_COLS + WIDTH_C + N_BRANCHES * D_MODEL)], axis=1).reshape(depth, 1, IN_COLS)
    col_bias = jnp.concatenate([jnp.zeros((depth, IN_COLS - N_BRANCHES * D_MODEL), F32),
                                b_gate.astype(F32)], axis=1).reshape(depth, 1, IN_COLS)
    wa_bf, wb_bf, wc_bf, wo_bf = (w.astype(BF16) for w in (w_br_a, w_br_b, w_br_c, w_out))
    wglu_bf, wup_bf, wdown_bf = w_glu.astype(BF16), w_up.astype(BF16), w_down.astype(BF16)

    w8, cd, l8 = jax.vmap(_ssm_operands)(ssm_a_re, ssm_a_im, ssm_log_dt, ssm_b_re, ssm_b_im, ssm_c_re, ssm_c_im)

    x2d = x.reshape(n, D_MODEL)
    for l in range(depth):
        lam_init = 0.8 - 0.6 * math.exp(-0.3 * l)
        qk_a, v_t, *qkv_b, c_u, gates = _in_projection(
            x2d, attn_norm_g[l].reshape(1, D_MODEL).astype(F32), w_in_bf, w_qk, l, col_gain, col_bias, rope,
            bsz, s_len, tm_proj)

        lam_p = jnp.stack([lam_q1[l], lam_k1[l], lam_q2[l], lam_k2[l]]).astype(F32)
        oa = _diff_attention(qk_a.reshape(bsz, s_len, 2 * A_Q_COLS), v_t, lam_p,
                             subln_g[l].reshape(1, 2 * HD_A).astype(F32), lam_init, tq_a)
        oa = oa.reshape(n, WIDTH_A)

        ob_parts, lse_parts = zip(*[_dilated_attention(qkv_b[g], bsz, s_len, g, tq_b) for g in range(N_DIL)])

        oc = _ssm_branch(c_u, bsz, s_len, w8, cd, l8, ssm_d[l].reshape(1, WIDTH_C).astype(F32),
                         wglu_bf, l, b_glu[l].reshape(1, WIDTH_C).astype(F32), t_ssm)

        x2d = _merge_ffn(x2d, oa, ob_parts, lse_parts, oc, gates, wa_bf, wb_bf, wc_bf, wo_bf,
                         ffn_norm_g[l].reshape(1, D_MODEL).astype(F32), wup_bf,
                         conv_w[l].astype(F32), conv_b[l].reshape(1, D_FF).astype(F32), wdown_bf, l,
                         s_len, tm_ffn)
    return x2d.reshape(bsz, s_len, D_MODEL)
```

```python
import functools
import math

import jax
import jax.numpy as jnp
from jax import lax
from jax.experimental import pallas as pl
from jax.experimental.pallas import tpu as pltpu

F32 = jnp.float32
BF16 = jnp.bfloat16

LANES = 128
SUBLANES = 8
ROT_SHIFT = LANES // 2

D_MODEL = 1024
N_HEADS_A = 4
HD_A = 64
N_DIL = 3
DIL_PAIRS = ((128, 1), (512, 4), (2048, 16))
N_HEADS_B = 4
HD_B = 128
WIDTH_B = N_HEADS_B * HD_B
SSM_GROUP = 16
SSM_STATE = 64
WIDTH_C = 512
N_GROUPS_C = WIDTH_C // SSM_GROUP
N_STATE = N_GROUPS_C * SSM_STATE
N_BRANCHES = 3
D_FF = 2816
CONV_WIDTH = 3
ROPE_THETA = 500000.0
ROPE_FRACTION = 4
EPS = 1e-6

COL_TILE = 512
QK_CHUNK = 2 * COL_TILE
A_Q_COLS = 2 * N_HEADS_A * HD_A
WIDTH_A = N_HEADS_A * 2 * HD_A
B_COLS = N_DIL * N_HEADS_B * HD_B
IN_COLS = 2 * A_Q_COLS + WIDTH_A + 3 * B_COLS + WIDTH_C + N_BRANCHES * D_MODEL
N_COL_TILES = IN_COLS // COL_TILE
TILE_AQ, TILE_AK, TILE_AV = 0, 1, 2
TILE_BQ, TILE_BK, TILE_BV = 3, 6, 9
TILE_CU = 12

HEADS_PER_STEP = 4
N_SCORE_BUF = 4
AHEAD = N_SCORE_BUF - 1
VT_PAD = 16
VT_ROWS = 2 * HD_A + VT_PAD
LSE_LANES = LANES // N_HEADS_B
SSM_SLAB = 256
SSM_TAPS = SUBLANES
FFN_TILE = 256

VMEM_LIMIT = 56 * 1024 * 1024


def _cparams(sem):
    return pltpu.CompilerParams(dimension_semantics=sem, vmem_limit_bytes=VMEM_LIMIT)


def _first_head_lanes():
    lane = lax.broadcasted_iota(jnp.int32, (1, LANES), 1)
    return (lane // (ROT_SHIFT // 2)) % 2 == 0


def _norm_rope_tile(acc, gain, seg, c_ref, s_ref):
    cos = c_ref[...]
    sin = s_ref[...]
    first = _first_head_lanes()
    outs = []
    for gi in range(acc.shape[1] // LANES):
        y = acc[:, gi * LANES:(gi + 1) * LANES]
        ysq = y * y
        tot = jnp.sum(ysq, axis=-1, keepdims=True)
        if seg == LANES:
            ssum = tot
        else:
            one = jnp.sum(jnp.where(first, ysq, 0.0), axis=-1, keepdims=True)
            ssum = jnp.where(first, one, tot - one)
        yn = y * lax.rsqrt(ssum * (1.0 / seg) + EPS) * gain[:, gi * LANES:(gi + 1) * LANES]
        outs.append(yn * cos + pltpu.roll(yn, ROT_SHIFT, 1) * sin)
    return jnp.concatenate(outs, axis=1)


def _inproj_kernel(x_ref, g_ref, w_ref, wqa_ref, wqb0_ref, wqb1_ref, wqb2_ref, gain_ref, bias_ref,
                   ca_ref, sa_ref, cb_ref, sb_ref,
                   qk_ref, vt_ref, b0_ref, b1_ref, b2_ref, cu_ref, gate_ref, dil_scr, *, tm):
    x = x_ref[...]
    ms = jnp.mean(x * x, axis=-1, keepdims=True)
    h = (x * lax.rsqrt(ms + EPS) * g_ref[...]).astype(BF16)
    dil_refs = (b0_ref, b1_ref, b2_ref)
    heavy = [TILE_AQ, TILE_AK] + list(range(TILE_BQ, TILE_BV))
    light = [TILE_AV] + list(range(TILE_BV, TILE_CU + 1))
    order = heavy + [j for j in range(N_COL_TILES) if j not in heavy + light] + light
    wqb_refs = (wqb0_ref, wqb1_ref, wqb2_ref)

    def project(j):
        if j in (TILE_AQ, TILE_AK):
            ref, c0 = wqa_ref, j * COL_TILE
        elif TILE_BQ <= j < TILE_BV:
            off = (j - TILE_BQ) * COL_TILE
            ref, c0 = wqb_refs[off // QK_CHUNK], off % QK_CHUNK
        else:
            ref, c0 = w_ref, j * COL_TILE
        return jnp.dot(h, ref[:, c0:c0 + COL_TILE], preferred_element_type=F32)

    nxt = project(order[0])
    for idx, j in enumerate(order):
        cols = slice(j * COL_TILE, (j + 1) * COL_TILE)
        acc = nxt
        if idx + 1 < len(order):
            nxt = project(order[idx + 1])
        if j in (TILE_AQ, TILE_AK):
            qk_ref[:, cols] = _norm_rope_tile(acc, gain_ref[:, cols], HD_A, ca_ref, sa_ref).astype(BF16)
        elif j == TILE_AV:
            acc_t = acc.T.astype(BF16)
            for hd in range(N_HEADS_A):
                vt_ref[hd * VT_ROWS:hd * VT_ROWS + 2 * HD_A, :] = acc_t[hd * 2 * HD_A:(hd + 1) * 2 * HD_A, :]
                vt_ref[hd * VT_ROWS + 2 * HD_A:(hd + 1) * VT_ROWS, :] = jnp.ones((VT_PAD, tm), BF16)
        elif j < TILE_CU:
            part, group = divmod(j - TILE_BQ, N_DIL)
            if part < 2:
                acc = _norm_rope_tile(acc, gain_ref[:, cols], HD_B, cb_ref, sb_ref)
            dil = DIL_PAIRS[group][1]
            out_ref = dil_refs[group]
            if dil == 1:
                out_ref[:, part * WIDTH_B:(part + 1) * WIDTH_B] = acc.astype(BF16)
            else:
                slot = (group - 1) * 3 + part
                for gi in range(WIDTH_B // LANES):
                    dil_scr[slot, gi] = acc[:, gi * LANES:(gi + 1) * LANES]
                for r in range(dil):
                    for gi in range(WIDTH_B // LANES):
                        out_ref[r, part, :, gi * LANES:(gi + 1) * LANES] = (
                            dil_scr[slot, gi, pl.ds(r, tm // dil, stride=dil), :].astype(BF16))
        elif j == TILE_CU:
            cu_ref[...] = acc
        else:
            g0 = (j - TILE_CU - 1) * COL_TILE
            gate_ref[:, g0:g0 + COL_TILE] = jax.nn.sigmoid(acc + bias_ref[:, cols]).astype(gate_ref.dtype)


def _in_projection(x2d, norm_g, w_bf, w_qk, layer, col_gain, col_bias, rope, bsz, s_len, tm):
    n = x2d.shape[0]
    tiles_per_seq = s_len // tm
    row = lambda i: (i, 0)
    const = lambda i: (0, 0)
    d1, d2 = DIL_PAIRS[1][1], DIL_PAIRS[2][1]
    kern = functools.partial(_inproj_kernel, tm=tm)
    resident = lambda cols: pl.BlockSpec((None, D_MODEL, cols), lambda i: (layer, 0, 0),
                                         pipeline_mode=pl.Buffered(1))
    return pl.pallas_call(
        kern,
        grid=(n // tm,),
        in_specs=[
            pl.BlockSpec((tm, D_MODEL), row),
            pl.BlockSpec((1, D_MODEL), const),
            resident(IN_COLS), resident(QK_CHUNK), resident(QK_CHUNK), resident(QK_CHUNK), resident(QK_CHUNK),
            pl.BlockSpec((1, IN_COLS), const),
            pl.BlockSpec((1, IN_COLS), const),
        ] + [pl.BlockSpec((tm, LANES), row)] * len(rope),
        out_specs=[
            pl.BlockSpec((tm, 2 * A_Q_COLS), row),
            pl.BlockSpec((None, N_HEADS_A * VT_ROWS, tm), lambda i: (i // tiles_per_seq, 0, i % tiles_per_seq)),
            pl.BlockSpec((tm, 3 * WIDTH_B), row),
            pl.BlockSpec((d1, 3, tm // d1, WIDTH_B), lambda i: (0, 0, i, 0)),
            pl.BlockSpec((d2, 3, tm // d2, WIDTH_B), lambda i: (0, 0, i, 0)),
            pl.BlockSpec((tm, WIDTH_C), row),
            pl.BlockSpec((tm, N_BRANCHES * D_MODEL), row),
        ],
        out_shape=[jax.ShapeDtypeStruct((n, 2 * A_Q_COLS), BF16),
                   jax.ShapeDtypeStruct((bsz, N_HEADS_A * VT_ROWS, s_len), BF16),
                   jax.ShapeDtypeStruct((n, 3 * WIDTH_B), BF16),
                   jax.ShapeDtypeStruct((d1, 3, n // d1, WIDTH_B), BF16),
                   jax.ShapeDtypeStruct((d2, 3, n // d2, WIDTH_B), BF16),
                   jax.ShapeDtypeStruct((n, WIDTH_C), F32),
                   jax.ShapeDtypeStruct((n, N_BRANCHES * D_MODEL), BF16)],
        scratch_shapes=[pltpu.VMEM((6, WIDTH_B // LANES, tm, LANES), F32)],
        compiler_params=_cparams(("parallel",)),
        name="in_projection",
    )(x2d, norm_g, w_bf, *w_qk, col_gain, col_bias, *rope)


def _diffattn_kernel(q_ref, k_ref, vt_ref, mask_ref, lam_ref, subg_ref, o_ref, acc_scr, s_scr, *, tq, lam_init):
    i = pl.program_id(2)
    q = q_ref[...]
    first = _first_head_lanes()
    nt = (((1,), (1,)), ((), ()))
    qm = []
    for hh in range(HEADS_PER_STEP):
        qh = q[:, hh * 2 * HD_A:(hh + 1) * 2 * HD_A]
        zero = jnp.zeros_like(qh)
        qm += [jnp.where(first, qh, zero), jnp.where(first, zero, qh)]
    n_maps = len(qm)
    acc_scr[...] = jnp.zeros_like(acc_scr)

    def score(j, mi):
        r0 = pl.multiple_of(j * tq, tq)
        hh = mi // 2
        kblk = k_ref[pl.ds(r0, tq), hh * 2 * HD_A:(hh + 1) * 2 * HD_A]
        s_scr[mi % N_SCORE_BUF] = lax.dot_general(kblk, qm[mi], nt, preferred_element_type=F32)

    def consume(j, mi, m_old, diagonal):
        r0 = pl.multiple_of(j * tq, tq)
        hh = mi // 2
        vtblk = vt_ref[hh * VT_ROWS:(hh + 1) * VT_ROWS, pl.ds(r0, tq)]
        st = s_scr[mi % N_SCORE_BUF]
        if diagonal:
            st = st + mask_ref[...]
        m_new = jnp.maximum(m_old, jnp.max(st, axis=0, keepdims=True))
        p = jnp.exp2(st - m_new)
        alpha = jnp.exp2(m_old - m_new)
        acc_scr[mi] = alpha * acc_scr[mi] + jnp.dot(vtblk, p.astype(BF16), preferred_element_type=F32)
        return m_new

    def stage(j, ms, diagonal):
        new = []
        for mi in range(n_maps):
            ahead = mi + AHEAD
            if ahead < n_maps:
                score(j, ahead)
            elif not diagonal:
                score(j + 1, ahead - n_maps)
            new.append(consume(j, mi, ms[mi], diagonal))
        return tuple(new)

    init = tuple(jnp.full((1, tq), -jnp.inf, F32) for _ in range(n_maps))

    for mi in range(AHEAD):
        score(0, mi)
    ms = lax.fori_loop(0, i // 2, lambda t, c: stage(2 * t + 1, stage(2 * t, c, False), False), init)
    odd = lax.rem(i, 2) == 1

    @pl.when(odd)
    def _():
        stage(i, stage(i - 1, ms, False), True)

    @pl.when(jnp.logical_not(odd))
    def _():
        stage(i, ms, True)

    lam_p = lam_ref[...]
    lam = (jnp.exp(jnp.sum(lam_p[0:1] * lam_p[1:2], axis=-1, keepdims=True))
           - jnp.exp(jnp.sum(lam_p[2:3] * lam_p[3:4], axis=-1, keepdims=True)) + lam_init)
    vals = slice(0, 2 * HD_A)
    den = slice(2 * HD_A, 2 * HD_A + 1)
    for hh in range(HEADS_PER_STEP):
        a1, a2 = acc_scr[2 * hh], acc_scr[2 * hh + 1]
        o_t = a1[vals] / a1[den] - lam * (a2[vals] / a2[den])
        msq = jnp.mean(o_t * o_t, axis=0, keepdims=True)
        o = (o_t * lax.rsqrt(msq + EPS)).T
        o_ref[:, hh * 2 * HD_A:(hh + 1) * 2 * HD_A] = ((o * subg_ref[...]) * (1.0 - lam_init)).astype(o_ref.dtype)


def _diff_attention(qk, v_t, lam_p, subln_g, lam_init, tq):
    bsz, s_len, _ = qk.shape
    width = HEADS_PER_STEP * 2 * HD_A
    kern = functools.partial(_diffattn_kernel, tq=tq, lam_init=lam_init)
    key = lax.broadcasted_iota(jnp.int32, (tq, tq), 0)
    qry = lax.broadcasted_iota(jnp.int32, (tq, tq), 1)
    causal = jnp.where(key <= qry, 0.0, -jnp.inf).astype(F32)
    return pl.pallas_call(
        kern,
        grid=(bsz, N_HEADS_A // HEADS_PER_STEP, s_len // tq),
        in_specs=[
            pl.BlockSpec((None, tq, width), lambda b, h, i: (b, i, h)),
            pl.BlockSpec((None, s_len, width), lambda b, h, i: (b, 0, N_HEADS_A // HEADS_PER_STEP + h)),
            pl.BlockSpec((None, HEADS_PER_STEP * VT_ROWS, s_len), lambda b, h, i: (b, h, 0)),
            pl.BlockSpec((tq, tq), lambda b, h, i: (0, 0)),
            pl.BlockSpec((4, HD_A), lambda b, h, i: (0, 0)),
            pl.BlockSpec((1, 2 * HD_A), lambda b, h, i: (0, 0)),
        ],
        out_specs=pl.BlockSpec((None, tq, width), lambda b, h, i: (b, i, h)),
        out_shape=jax.ShapeDtypeStruct((bsz, s_len, WIDTH_A), BF16),
        scratch_shapes=[pltpu.VMEM((2 * HEADS_PER_STEP, VT_ROWS, tq), F32),
                        pltpu.VMEM((N_SCORE_BUF, tq, tq), F32)],
        compiler_params=_cparams(("parallel", "parallel", "arbitrary")),
        name="diff_attention",
    )(qk, qk, v_t, causal, lam_p, subln_g)


def _dilated_kernel(q_ref, k_ref, kp_ref, v_ref, vp_ref, o_ref, lse_ref, *, tq, blk):
    n = pl.program_id(2)
    scale = 1.0 / math.sqrt(HD_B)
    nt = (((1,), (1,)), ((), ()))
    rr = lax.broadcasted_iota(jnp.int32, (blk, 2 * blk), 0)
    cc = lax.broadcasted_iota(jnp.int32, (blk, 2 * blk), 1)
    band = jnp.logical_and(cc >= rr, cc <= rr + blk)
    lane = lax.broadcasted_iota(jnp.int32, (1, 2 * blk), 1)
    first_bias = jnp.where(lane >= blk, 0.0, jnp.where(n > 0, 0.0, -jnp.inf).astype(F32))

    def window(ref, pref, c, cols):
        if c == 0:
            return jnp.concatenate([pref[:, cols], ref[0:blk, cols]], axis=0)
        return ref[(c - 1) * blk:(c + 1) * blk, cols]

    def scores(unit):
        h, c = unit
        cols = slice(h * HD_B, (h + 1) * HD_B)
        qh = q_ref[c * blk:(c + 1) * blk, cols]
        s = lax.dot_general(qh, window(k_ref, kp_ref, c, cols), nt, preferred_element_type=F32) * scale
        s = jnp.where(band, s, -jnp.inf)
        return s + first_bias if c == 0 else s

    head_lane = lax.broadcasted_iota(jnp.int32, (1, LANES), 1) // LSE_LANES

    def finish(unit, s, lse_acc):
        h, c = unit
        cols = slice(h * HD_B, (h + 1) * HD_B)
        rows = slice(c * blk, (c + 1) * blk)
        m = jnp.max(s, axis=-1, keepdims=True)
        p = jnp.exp(s - m)
        den = jnp.sum(p, axis=-1, keepdims=True)
        pv = jnp.dot(p.astype(BF16), window(v_ref, vp_ref, c, cols), preferred_element_type=F32)
        o_ref[rows, cols] = (pv / den).astype(o_ref.dtype)
        lse = jnp.broadcast_to(m + jnp.log(den), (blk, LANES))
        lse_acc = lse if h == 0 else jnp.where(head_lane == h, lse, lse_acc)
        if h == N_HEADS_B - 1:
            lse_ref[rows, :] = lse_acc
        return lse_acc

    units = [(h, c) for c in range(tq // blk) for h in range(N_HEADS_B)]
    s = scores(units[0])
    lse_acc = None
    for idx, unit in enumerate(units):
        s_next = scores(units[idx + 1]) if idx + 1 < len(units) else None
        lse_acc = finish(unit, s, lse_acc)
        s = s_next


def _dilated_attention(qkv, bsz, s_len, group, tq):
    window, dil = DIL_PAIRS[group]
    blk = window // dil
    rows = s_len // dil
    tq = min(tq, rows)
    sub = tq // blk
    prev = lambda n: jnp.maximum(n * sub - 1, 0)
    if dil == 1:
        view = qkv.reshape(bsz, rows, 3 * WIDTH_B)
        spec = lambda part, nrows, which: pl.BlockSpec(
            (None, nrows, WIDTH_B), lambda b, r, n: (b, which(n), part))
    else:
        view = qkv.reshape(dil, 3, bsz, rows, WIDTH_B)
        spec = lambda part, nrows, which: pl.BlockSpec(
            (None, None, None, nrows, WIDTH_B), lambda b, r, n: (r, part, b, which(n), 0))
    own = lambda n: n
    kern = functools.partial(_dilated_kernel, tq=tq, blk=blk)
    o, lse = pl.pallas_call(
        kern,
        grid=(bsz, dil, rows // tq),
        in_specs=[spec(0, tq, own), spec(1, tq, own), spec(1, blk, prev), spec(2, tq, own), spec(2, blk, prev)],
        out_specs=[pl.BlockSpec((None, tq, WIDTH_B), lambda b, r, n: (b, n, r)),
                   pl.BlockSpec((None, tq, LANES), lambda b, r, n: (b, n, r))],
        out_shape=[jax.ShapeDtypeStruct((bsz, rows, dil * WIDTH_B), BF16),
                   jax.ShapeDtypeStruct((bsz, rows, dil * LANES), F32)],
        compiler_params=_cparams(("parallel", "parallel", "arbitrary")),
        name=f"dilated_attention_g{group}",
    )(view, view, view, view, view)
    return o.reshape(bsz * rows, dil * WIDTH_B), lse.reshape(bsz * rows, dil * LANES)


def _ssm_kernel(u_ref, w8_ref, cd_ref, l8_ref, dskip_ref, wglu_ref, bglu_ref, o_ref,
                u_scr, x_scr, carry_scr, *, t_chunk):
    halo = SUBLANES

    @pl.when(pl.program_id(1) == 0)
    def _():
        carry_scr[...] = jnp.zeros_like(carry_scr)
        u_scr[0:halo, :] = jnp.zeros((halo, WIDTH_C), F32)

    @pl.when(pl.program_id(1) != 0)
    def _():
        u_scr[0:halo, :] = u_scr[t_chunk:t_chunk + halo, :]

    u = u_ref[...]
    u_scr[halo:halo + t_chunk, :] = u
    re = slice(0, N_STATE)
    im = slice(N_STATE, 2 * N_STATE)
    ch = WIDTH_C // 2
    sh = N_STATE // 2
    blk_ch = LANES // 2
    blk_st = blk_ch * SSM_STATE // SSM_GROUP
    low = lax.broadcasted_iota(jnp.int32, (1, LANES), 1) < blk_ch
    c_re, c_im = carry_scr[:, re], carry_scr[:, im]
    a8, b8 = l8_ref[:, re], l8_ref[:, im]
    n_slabs = t_chunk // SSM_SLAB
    slab_rows = lambda s: slice(s * SSM_SLAB, (s + 1) * SSM_SLAB)

    def drive(s):
        r0 = halo + s * SSM_SLAB
        for pair in range(WIDTH_C // LANES):
            lanes = slice(pair * LANES, (pair + 1) * LANES)
            taps = [u_scr[r0 - t:r0 - t + SSM_SLAB, lanes] for t in range(SSM_TAPS)]
            for half in range(2):
                cb = 2 * pair + half
                pieces = []
                for a in range(SSM_TAPS // 2):
                    even, odd = taps[2 * a], taps[2 * a + 1]
                    if half == 0:
                        v = jnp.where(low, even, pltpu.roll(odd, blk_ch, 1))
                    else:
                        v = jnp.where(low, pltpu.roll(even, blk_ch, 1), odd)
                    pieces.append(v.astype(BF16))
                w = jnp.dot(jnp.concatenate(pieces, axis=1), w8_ref[cb], preferred_element_type=F32)
                x_scr[slab_rows(s), cb * blk_st:(cb + 1) * blk_st] = w[:, :blk_st]
                x_scr[slab_rows(s), N_STATE + cb * blk_st:N_STATE + (cb + 1) * blk_st] = w[:, blk_st:]

    drive(0)
    if n_slabs > 1:
        drive(1)
    for s in range(n_slabs):
        rows = slab_rows(s)
        if s + 2 < n_slabs:
            drive(s + 2)
        for t in range(SSM_SLAB // SUBLANES):
            r8 = slice(s * SSM_SLAB + t * SUBLANES, s * SSM_SLAB + (t + 1) * SUBLANES)
            x_re = x_scr[r8, re] + (a8 * c_re - b8 * c_im)
            x_im = x_scr[r8, im] + (a8 * c_im + b8 * c_re)
            x_scr[r8, re] = x_re
            x_scr[r8, im] = x_im
            c_re, c_im = x_re, x_im
        halves = []
        for hh in range(2):
            acc = None
            for part in range(2):
                c0 = part * N_STATE + hh * sh
                d = jnp.dot(x_scr[rows, c0:c0 + sh].astype(BF16), cd_ref[c0:c0 + sh, hh * ch:(hh + 1) * ch],
                            preferred_element_type=F32)
                acc = d if acc is None else acc + d
            halves.append(acc)
        y = jnp.concatenate(halves, axis=1) + dskip_ref[...] * u[rows]
        cdf = 0.5 * (1.0 + jnp.tanh(math.sqrt(2.0 / math.pi) * (y + 0.044715 * (y * y * y))))
        z = y * cdf
        gate = jnp.dot(z.astype(BF16), wglu_ref[...], preferred_element_type=F32) + bglu_ref[...]
        o_ref[rows, :] = (z * jax.nn.sigmoid(gate)).astype(o_ref.dtype)
    carry_scr[:, re] = c_re
    carry_scr[:, im] = c_im


def _ssm_operands(a_re, a_im, log_dt, b_re, b_im, c_re, c_im):
    a_re, a_im = a_re.astype(F32), a_im.astype(F32)
    dt = jnp.exp(log_dt.astype(F32))[:, None]
    mag = jnp.exp(a_re * dt)
    lb_re, lb_im = mag * jnp.cos(a_im * dt), mag * jnp.sin(a_im * dt)
    n_re, n_im = lb_re - 1.0, lb_im
    den = a_re * a_re + a_im * a_im
    f_re = (n_re * a_re + n_im * a_im) / den
    f_im = (n_im * a_re - n_re * a_im) / den
    b_re, b_im = b_re.astype(F32), b_im.astype(F32)
    bb_re = f_re[..., None] * b_re - f_im[..., None] * b_im
    bb_im = f_re[..., None] * b_im + f_im[..., None] * b_re
    def blockdiag_out(t):
        flat = t.transpose(0, 2, 1).reshape(N_STATE, SSM_GROUP).astype(BF16)
        col = jnp.arange(WIDTH_C)
        spread = (col[None, :] % SSM_GROUP == jnp.arange(SSM_GROUP)[:, None]).astype(BF16)
        full = jnp.dot(flat, spread, preferred_element_type=F32)
        keep = jnp.arange(N_STATE)[:, None] // SSM_STATE == col[None, :] // SSM_GROUP
        return jnp.where(keep, full, 0.0)

    cd = jnp.concatenate([blockdiag_out(c_re.astype(F32)), -blockdiag_out(c_im.astype(F32))],
                         axis=0).astype(BF16)

    def power(k):
        return (mag ** k) * jnp.cos(a_im * dt * k), (mag ** k) * jnp.sin(a_im * dt * k)

    taps = []
    for s in range(SSM_TAPS):
        p_re, p_im = power(float(s))
        taps.append(jnp.stack([p_re[..., None] * bb_re - p_im[..., None] * bb_im,
                               p_re[..., None] * bb_im + p_im[..., None] * bb_re], axis=-1))
    grp = (LANES // 2) // SSM_GROUP
    n_blk = N_GROUPS_C // grp
    k_blk = SSM_TAPS * grp * SSM_GROUP
    n_col = 2 * grp * SSM_STATE
    v = jnp.stack(taps, axis=0).reshape(SSM_TAPS, n_blk, grp, SSM_STATE, SSM_GROUP, 2)
    dense = v.transpose(1, 0, 2, 4, 5, 3).reshape(n_blk, k_blk, 2 * SSM_STATE).astype(BF16)
    col = jnp.arange(n_col)
    src = (col // (grp * SSM_STATE)) * SSM_STATE + col % SSM_STATE
    spread = (src[None, :] == jnp.arange(2 * SSM_STATE)[:, None]).astype(BF16)
    full = jnp.einsum('bkd,dn->bkn', dense, spread, preferred_element_type=F32)
    row_grp = (jnp.arange(k_blk) // SSM_GROUP) % grp
    col_grp = (col // SSM_STATE) % grp
    w8 = jnp.where(row_grp[:, None] == col_grp[None, :], full, 0.0).astype(BF16)
    p_re, p_im = power(float(SSM_TAPS))
    l8 = jnp.concatenate([p_re.reshape(1, N_STATE), p_im.reshape(1, N_STATE)], axis=1)
    return w8, cd, jnp.broadcast_to(l8, (SUBLANES, 2 * N_STATE))


def _ssm_branch(c_u, bsz, s_len, w8, cd, l8, d_skip, w_glu_bf, layer, b_glu, t_chunk):
    n_chunks = s_len // t_chunk
    const2 = lambda b, c: (0, 0)
    kern = functools.partial(_ssm_kernel, t_chunk=t_chunk)
    return pl.pallas_call(
        kern,
        grid=(bsz, n_chunks),
        in_specs=[
            pl.BlockSpec((t_chunk, WIDTH_C), lambda b, c: (b * n_chunks + c, 0)),
            pl.BlockSpec((None,) + w8.shape[1:], lambda b, c: (layer, 0, 0, 0)),
            pl.BlockSpec((None, 2 * N_STATE, WIDTH_C), lambda b, c: (layer, 0, 0)),
            pl.BlockSpec((None, SUBLANES, 2 * N_STATE), lambda b, c: (layer, 0, 0)),
            pl.BlockSpec((1, WIDTH_C), const2),
            pl.BlockSpec((None, WIDTH_C, WIDTH_C), lambda b, c: (layer, 0, 0)),
            pl.BlockSpec((1, WIDTH_C), const2),
        ],
        out_specs=pl.BlockSpec((t_chunk, WIDTH_C), lambda b, c: (b * n_chunks + c, 0)),
        out_shape=jax.ShapeDtypeStruct((bsz * s_len, WIDTH_C), BF16),
        scratch_shapes=[pltpu.VMEM((t_chunk + SUBLANES, WIDTH_C), F32),
                        pltpu.VMEM((t_chunk, 2 * N_STATE), F32),
                        pltpu.VMEM((SUBLANES, 2 * N_STATE), F32)],
        compiler_params=_cparams(("parallel", "arbitrary")),
        name="s5_scan_glu",
    )(c_u, w8, cd, l8, d_skip, w_glu_bf, b_glu)


def _merge_body(x_ref, oa_ref, o0_ref, o1_ref, o2_ref, l0_ref, l1_ref, l2_ref, oc_ref,
                g0_ref, g1_ref, g2_ref, wa_ref, wb_ref, wc_ref, wo_ref, tok_scr, lse_scr, tm):
    for gi, (o_ref, l_ref) in enumerate(((o1_ref, l1_ref), (o2_ref, l2_ref))):
        dil = DIL_PAIRS[gi + 1][1]
        for r in range(dil):
            lse_scr[gi, pl.ds(r, tm // dil, stride=dil), :] = l_ref[:, r * LANES:(r + 1) * LANES]
            for li in range(WIDTH_B // LANES):
                cols = slice(r * WIDTH_B + li * LANES, r * WIDTH_B + (li + 1) * LANES)
                tok_scr[gi, li, pl.ds(r, tm // dil, stride=dil), :] = o_ref[:, cols].astype(F32)
    tok = lambda k: jnp.concatenate([tok_scr[k, li] for li in range(WIDTH_B // LANES)], axis=1)
    l0, l1, l2 = l0_ref[...], lse_scr[0], lse_scr[1]
    m = jnp.maximum(jnp.maximum(l0, l1), l2)
    e0, e1, e2 = jnp.exp(l0 - m), jnp.exp(l1 - m), jnp.exp(l2 - m)
    tot = e0 + e1 + e2
    sel_r = lax.broadcasted_iota(jnp.int32, (LANES, WIDTH_B), 0)
    sel_c = lax.broadcasted_iota(jnp.int32, (LANES, WIDTH_B), 1)
    select = jnp.where(sel_r == LSE_LANES * (sel_c // HD_B), 1.0, 0.0).astype(BF16)

    def spread(w):
        hi = w.astype(BF16)
        lo = (w - hi.astype(F32)).astype(BF16)
        return (jnp.dot(hi, select, preferred_element_type=F32)
                + jnp.dot(lo, select, preferred_element_type=F32))

    ob = (spread(e0 / tot) * o0_ref[...].astype(F32) + spread(e1 / tot) * tok(0)
          + spread(e2 / tot) * tok(1))
    ya = jnp.dot(oa_ref[...], wa_ref[...], preferred_element_type=F32)
    yb = jnp.dot(ob.astype(BF16), wb_ref[...], preferred_element_type=F32)
    yc = jnp.dot(oc_ref[...], wc_ref[...], preferred_element_type=F32)
    merged = (g0_ref[...].astype(F32) * ya + g1_ref[...].astype(F32) * yb
              + g2_ref[...].astype(F32) * yc)
    return x_ref[...] + jnp.dot(merged.astype(BF16), wo_ref[...], preferred_element_type=F32)


def _ffn_conv_halo(a_scr, tm, tiles_per_seq):
    halo = SUBLANES

    @pl.when(pl.program_id(0) % tiles_per_seq == 0)
    def _():
        a_scr[0:halo, :] = jnp.zeros((halo, D_FF), F32)

    @pl.when(pl.program_id(0) % tiles_per_seq != 0)
    def _():
        a_scr[0:halo, :] = a_scr[tm:tm + halo, :]


def _ffn_body(x, g_ref, wup_ref, cw_ref, cb_ref, wdown_ref, o_ref, a_scr, tm):
    halo = SUBLANES
    ms = jnp.mean(x * x, axis=-1, keepdims=True)
    h = (x * lax.rsqrt(ms + EPS) * g_ref[...]).astype(BF16)

    n_tiles = D_FF // FFN_TILE
    tile_cols = lambda f: slice(f * FFN_TILE, (f + 1) * FFN_TILE)

    def up(f):
        a_scr[halo:halo + tm, tile_cols(f)] = jnp.dot(h, wup_ref[:, tile_cols(f)], preferred_element_type=F32)
        return jnp.dot(h, wup_ref[:, D_FF + f * FFN_TILE:D_FF + (f + 1) * FFN_TILE],
                       preferred_element_type=F32)

    y = x
    gate = up(0)
    for f in range(n_tiles):
        cols = tile_cols(f)
        next_gate = up(f + 1) if f + 1 < n_tiles else None
        conv = (cb_ref[:, cols] + cw_ref[0:1, cols] * a_scr[halo - 2:halo - 2 + tm, cols]
                + cw_ref[1:2, cols] * a_scr[halo - 1:halo - 1 + tm, cols]
                + cw_ref[2:3, cols] * a_scr[halo:halo + tm, cols])
        act = (conv * jax.nn.sigmoid(conv)) * gate
        y = y + jnp.dot(act.astype(BF16), wdown_ref[cols, :], preferred_element_type=F32)
        gate = next_gate
    o_ref[...] = y


N_MERGE_INPUTS = 16


def _merge_ffn_kernel(*refs, tm, tiles_per_seq):
    merge_in = refs[:N_MERGE_INPUTS]
    g_ref, wup_ref, cw_ref, cb_ref, wdown_ref, o_ref, tok_scr, lse_scr, a_scr = refs[N_MERGE_INPUTS:]
    _ffn_conv_halo(a_scr, tm, tiles_per_seq)
    x_mid = _merge_body(*merge_in, tok_scr, lse_scr, tm)
    _ffn_body(x_mid, g_ref, wup_ref, cw_ref, cb_ref, wdown_ref, o_ref, a_scr, tm)


def _merge_ffn(x2d, oa, ob_parts, lse_parts, oc, gates, wa, wb, wc, wo,
               norm_g, wup_bf, conv_w, conv_b, wdown_bf, layer, s_len, tm):
    n = x2d.shape[0]
    row = lambda i: (i, 0)
    const = lambda i: (0, 0)
    half = pl.BlockSpec((tm, WIDTH_C), row)
    full = pl.BlockSpec((tm, D_MODEL), row)
    single = pl.Buffered(1)
    wspec_half = pl.BlockSpec((None, WIDTH_C, D_MODEL), lambda i: (layer, 0, 0), pipeline_mode=single)
    gspec = lambda k: pl.BlockSpec((tm, D_MODEL), lambda i: (i, k))
    dilated = lambda g, width: pl.BlockSpec((tm // DIL_PAIRS[g][1], DIL_PAIRS[g][1] * width), row)
    merge_specs = [full, half, half, dilated(1, WIDTH_B), dilated(2, WIDTH_B),
                   pl.BlockSpec((tm, LANES), row), dilated(1, LANES), dilated(2, LANES), half,
                   gspec(0), gspec(1), gspec(2),
                   wspec_half, wspec_half, wspec_half,
                   pl.BlockSpec((None, D_MODEL, D_MODEL), lambda i: (layer, 0, 0), pipeline_mode=single)]
    assert len(merge_specs) == N_MERGE_INPUTS
    ffn_specs = [
        pl.BlockSpec((1, D_MODEL), const),
        pl.BlockSpec((None, D_MODEL, 2 * D_FF), lambda i: (layer, 0, 0), pipeline_mode=single),
        pl.BlockSpec((CONV_WIDTH, D_FF), const),
        pl.BlockSpec((1, D_FF), const),
        pl.BlockSpec((None, D_FF, D_MODEL), lambda i: (layer, 0, 0), pipeline_mode=single),
    ]
    kern = functools.partial(_merge_ffn_kernel, tm=tm, tiles_per_seq=s_len // tm)
    return pl.pallas_call(
        kern,
        grid=(n // tm,),
        in_specs=merge_specs + ffn_specs,
        out_specs=full,
        out_shape=jax.ShapeDtypeStruct((n, D_MODEL), F32),
        scratch_shapes=[pltpu.VMEM((2, WIDTH_B // LANES, tm, LANES), F32),
                        pltpu.VMEM((2, tm, LANES), F32),
                        pltpu.VMEM((tm + SUBLANES, D_FF), F32)],
        compiler_params=_cparams(("arbitrary",)),
        name="merge_conv_ffn",
    )(x2d, oa, *ob_parts, *lse_parts, oc, gates, gates, gates, wa, wb, wc, wo,
      norm_g, wup_bf, conv_w, conv_b, wdown_bf)


def _lane_order(head_dim):
    half = head_dim // ROPE_FRACTION // 2
    heads = LANES // head_dim
    quarter = ROT_SHIFT // heads
    first, second = [], []
    for hd in range(heads):
        base = hd * head_dim
        first += list(range(base, base + half)) + list(range(base + 2 * half, base + half + quarter))
        second += list(range(base + half, base + 2 * half)) + list(range(base + half + quarter, base + head_dim))
    return first + second


def _permute_groups(w_cols, head_dim):
    order = _lane_order(head_dim)
    groups = w_cols.shape[-1] // LANES
    src = jnp.asarray([g * LANES + o for g in range(groups) for o in order])
    perm = jnp.zeros((groups * LANES,) * 2, F32).at[src, jnp.arange(groups * LANES)].set(1.0).astype(w_cols.dtype)
    return jnp.dot(w_cols, perm, preferred_element_type=F32).astype(w_cols.dtype)


def _rope_tables(positions, head_dim):
    rot = head_dim // ROPE_FRACTION
    half = rot // 2
    heads = LANES // head_dim
    quarter = ROT_SHIFT // heads
    inv = ROPE_THETA ** (-jnp.arange(0, rot, 2, dtype=F32) / rot)
    ang = positions.reshape(-1).astype(F32)[:, None] * inv
    cos, sin = jnp.cos(ang), jnp.sin(ang)
    n = ang.shape[0]
    pad = quarter - half
    c_q = jnp.concatenate([cos, jnp.ones((n, pad), F32)], axis=1)
    s_q = jnp.concatenate([sin, jnp.zeros((n, pad), F32)], axis=1)
    c = jnp.tile(c_q, (1, 2 * heads))
    s = jnp.concatenate([jnp.tile(-s_q, (1, heads)), jnp.tile(s_q, (1, heads))], axis=1)
    return c, s


def kernel(x, positions, attn_norm_g, w_in, b_gate, qn_a, kn_a, lam_q1, lam_k1, lam_q2, lam_k2, subln_g, w_br_a, qn_b, kn_b, w_br_b, ssm_a_re, ssm_a_im, ssm_log_dt, ssm_b_re, ssm_b_im, ssm_c_re, ssm_c_im, ssm_d, w_glu, b_glu, w_br_c, w_out, ffn_norm_g, w_up, conv_w, conv_b, w_down):
    bsz, s_len, d_model = x.shape
    depth = w_in.shape[0]
    assert d_model == D_MODEL and w_in.shape[2] == IN_COLS
    assert s_len % DIL_PAIRS[-1][0] == 0, "sequence must be a multiple of the largest dilated window"
    n = bsz * s_len

    tm_proj = 256
    tq_a = min(512, s_len)
    tq_b = 1024
    t_ssm = min(1024, s_len)
    tm_ffn = min(256, s_len)

    rope = _rope_tables(positions, HD_A) + _rope_tables(positions, HD_B)
    w_in_bf = w_in.astype(BF16)
    w_rows = w_in_bf.reshape(depth * D_MODEL, IN_COLS)
    qk_chunk = lambda c0, hd: _permute_groups(w_rows[:, c0:c0 + QK_CHUNK], hd).reshape(depth, D_MODEL, QK_CHUNK)
    w_qk = [qk_chunk(0, HD_A)] + [qk_chunk(TILE_BQ * COL_TILE + k * QK_CHUNK, HD_B)
                                  for k in range(2 * B_COLS // QK_CHUNK)]
    order_a = jnp.asarray(_lane_order(HD_A))
    order_b = jnp.asarray(_lane_order(HD_B))
    wa_bf, wb_bf, wc_bf, wo_bf = (w.astype(BF16) for w in (w_br_a, w_br_b, w_br_c, w_out))
    wglu_bf, wup_bf, wdown_bf = w_glu.astype(BF16), w_up.astype(BF16), w_down.astype(BF16)

    w8, cd, l8 = jax.vmap(_ssm_operands)(ssm_a_re, ssm_a_im, ssm_log_dt, ssm_b_re, ssm_b_im, ssm_c_re, ssm_c_im)

    x2d = x.reshape(n, D_MODEL)
    for l in range(depth):
        lam_init = 0.8 - 0.6 * math.exp(-0.3 * l)
        ones = lambda w: jnp.ones((w,), F32)
        group_a = lambda g: jnp.tile(jnp.tile(g.astype(F32), LANES // HD_A)[order_a], A_Q_COLS // LANES)
        group_b = lambda g: jnp.tile(g.astype(F32)[order_b], B_COLS // LANES)
        col_gain = jnp.concatenate([
            group_a(qn_a[l]) * (math.log2(math.e) / math.sqrt(HD_A)), group_a(kn_a[l]), ones(WIDTH_A),
            group_b(qn_b[l]), group_b(kn_b[l]),
            ones(B_COLS + WIDTH_C + N_BRANCHES * D_MODEL)]).reshape(1, IN_COLS)
        col_bias = jnp.concatenate([jnp.zeros((IN_COLS - N_BRANCHES * D_MODEL,), F32),
                                    b_gate[l].astype(F32)]).reshape(1, IN_COLS)
        qk_a, v_t, *qkv_b, c_u, gates = _in_projection(
            x2d, attn_norm_g[l].reshape(1, D_MODEL).astype(F32), w_in_bf, w_qk, l, col_gain, col_bias, rope,
            bsz, s_len, tm_proj)

        lam_p = jnp.stack([lam_q1[l], lam_k1[l], lam_q2[l], lam_k2[l]]).astype(F32)
        oa = _diff_attention(qk_a.reshape(bsz, s_len, 2 * A_Q_COLS), v_t, lam_p,
                             subln_g[l].reshape(1, 2 * HD_A).astype(F32), lam_init, tq_a)
        oa = oa.reshape(n, WIDTH_A)

        ob_parts, lse_parts = zip(*[_dilated_attention(qkv_b[g], bsz, s_len, g, tq_b) for g in range(N_DIL)])

        oc = _ssm_branch(c_u, bsz, s_len, w8, cd, l8, ssm_d[l].reshape(1, WIDTH_C).astype(F32),
                         wglu_bf, l, b_glu[l].reshape(1, WIDTH_C).astype(F32), t_ssm)

        x2d = _merge_ffn(x2d, oa, ob_parts, lse_parts, oc, gates, wa_bf, wb_bf, wc_bf, wo_bf,
                         ffn_norm_g[l].reshape(1, D_MODEL).astype(F32), wup_bf,
                         conv_w[l].astype(F32), conv_b[l].reshape(1, D_FF).astype(F32), wdown_bf, l,
                         s_len, tm_ffn)
    return x2d.reshape(bsz, s_len, D_MODEL)
```
